```python
import math
import jax
import jax.numpy as jnp
from jax import lax
import numpy as np

D_MODEL = 2048
BATCH = 16
SEQ = 256
DEPTH = 1
DEC_BATCH = 8
DEC_SEQ = 2048
PAST_LEN = 512

GRID_W = 64
MLA_HEADS = 8
QK_NOPE = 128
QK_ROPE = 64
V_HEAD = 128
Q_LORA = 512
KV_LORA = 256
MLA_WIDTH = MLA_HEADS * V_HEAD
ROPE_THETA = 10000.0
ROPE_FREQS = QK_ROPE // 4
Q_BLOCK = 128
HY_WIDTH = 1024
HY_EMB = 33
HY_BANDS = (HY_EMB - 1) // 2
HY_HIDDEN = 64
HY_FAST_DECAY = 0.3
HY_SLOW_DECAY = 1.5
HY_TARGET = 0.01
N_EXPERTS = 16
EXPERT_FF = 1024
EC_CAPACITY = 2
EPS = 1e-6
ALPHA = (2 * DEPTH) ** 0.25
BETA = (8 * DEPTH) ** -0.25
N_MOD = 6
IN_SPLITS = (Q_LORA,
             Q_LORA + KV_LORA,
             Q_LORA + KV_LORA + QK_ROPE,
             Q_LORA + KV_LORA + QK_ROPE + 3 * HY_WIDTH,
             Q_LORA + KV_LORA + QK_ROPE + 3 * HY_WIDTH + D_MODEL)
IN_WIDTH = IN_SPLITS[-1] + D_MODEL

kernel_name = 'hybrid_mla_hyena_ec_diffusion_step'


def layer_norm(x, g, b):
    xf = x.astype(jnp.float32)
    mu = jnp.mean(xf, axis=-1, keepdims=True)
    var = jnp.mean(jnp.square(xf - mu), axis=-1, keepdims=True)
    y = (xf - mu) * lax.rsqrt(var + EPS) * g.astype(jnp.float32) + b.astype(jnp.float32)
    return y.astype(x.dtype)


def rms_norm(x, g):
    xf = x.astype(jnp.float32)
    y = xf * lax.rsqrt(jnp.mean(jnp.square(xf), axis=-1, keepdims=True) + EPS) * g.astype(jnp.float32)
    return y.astype(x.dtype)


def adaln_mod(cond, w_ada, b_ada):
    mod = (jax.nn.silu(cond) @ w_ada + b_ada)[..., None, :]
    return jnp.split(mod, N_MOD, axis=-1)


def axial_rope_tables(n_tokens, dtype):
    n_rows = n_tokens // GRID_W
    row = jnp.repeat(jnp.arange(n_rows, dtype=jnp.float32), GRID_W)
    col = jnp.tile(jnp.arange(GRID_W, dtype=jnp.float32), n_rows)
    inv = ROPE_THETA ** (-jnp.arange(ROPE_FREQS, dtype=jnp.float32) / ROPE_FREQS)
    ang = jnp.stack([row[:, None] * inv, col[:, None] * inv], axis=1)
    return jnp.cos(ang).astype(dtype), jnp.sin(ang).astype(dtype)


def apply_axial_rope(x, cos, sin):
    xs = x.reshape(x.shape[:-1] + (2, 2, ROPE_FREQS))
    a, b = xs[..., 0, :], xs[..., 1, :]
    out = jnp.stack([a * cos - b * sin, b * cos + a * sin], axis=-2)
    return out.reshape(x.shape)


def mla_attention(q_nope, q_rope, k_nope, k_rope, v):
    B, Lq, H, _ = q_nope.shape
    nb = Lq // Q_BLOCK
    scale = (QK_NOPE + QK_ROPE) ** -0.5
    qn = q_nope.reshape(B, nb, Q_BLOCK, H, QK_NOPE).transpose(1, 0, 2, 3, 4)
    qr = q_rope.reshape(B, nb, Q_BLOCK, H, QK_ROPE).transpose(1, 0, 2, 3, 4)

    def block(args):
        qn_b, qr_b = args
        s = (jnp.einsum('bqhd,bkhd->bhqk', qn_b, k_nope)
             + jnp.einsum('bqhr,bkr->bhqk', qr_b, k_rope))
        p = jax.nn.softmax(s.astype(jnp.float32) * scale, axis=-1).astype(v.dtype)
        return jnp.einsum('bhqk,bkhd->bqhd', p, v)

    o = lax.map(block, (qn, qr))
    return o.transpose(1, 0, 2, 3, 4).reshape(B, Lq, H * V_HEAD)


def short_conv3(u, w, b):
    up = jnp.pad(u, ((0, 0), (1, 1), (0, 0)))
    return up[:, :-2] * w[0] + up[:, 1:-1] * w[1] + up[:, 2:] * w[2] + b


def implicit_filter(L, p):
    f32 = lambda a: a.astype(jnp.float32)
    pos = jnp.arange(L, dtype=jnp.float32)
    t = pos / max(L - 1, 1)
    bands = jnp.linspace(1e-4, HY_BANDS - 1, HY_BANDS, dtype=jnp.float32)
    w = 2.0 * math.pi * pos / L
    feats = jnp.concatenate([t[:, None], jnp.cos(w[:, None] * bands), -jnp.sin(w[:, None] * bands)], axis=-1)
    hdn = jnp.sin(f32(p['hy_filt_freq1']) * (feats @ f32(p['hy_filt_w1']) + f32(p['hy_filt_b1'])))
    hdn = jnp.sin(f32(p['hy_filt_freq2']) * (hdn @ f32(p['hy_filt_w2']) + f32(p['hy_filt_b2'])))
    h = hdn @ f32(p['hy_filt_w3']) + f32(p['hy_filt_b3'])
    deltas = jnp.abs(jnp.linspace(math.log(HY_TARGET) / HY_SLOW_DECAY, math.log(HY_TARGET) / HY_FAST_DECAY,
                                  HY_WIDTH, dtype=jnp.float32))
    window = jnp.exp(-t[:, None] * deltas)
    h_fwd = h[:, :HY_WIDTH] * window
    h_bwd = h[:, HY_WIDTH:] * window
    kernel = jnp.concatenate([h_fwd, jnp.zeros((1, HY_WIDTH), jnp.float32), h_bwd[:0:-1]], axis=0)
    return kernel / (jnp.sum(jnp.abs(kernel), axis=0, keepdims=True) + EPS)


def fft_long_conv(z, kernel):
    L = z.shape[1]
    zf = jnp.fft.rfft(z.astype(jnp.float32), n=2 * L, axis=1)
    kf = jnp.fft.rfft(kernel, n=2 * L, axis=0)
    return jnp.fft.irfft(zf * kf[None], n=2 * L, axis=1)[:, :L].astype(z.dtype)


def hyena_branch(u_proj, p):
    u = short_conv3(u_proj, p['hy_short_w'], p['hy_short_b'])
    x0, x1, v = jnp.split(u, 3, axis=-1)
    z = x1 * v
    kernel = implicit_filter(u.shape[1], p)
    y = fft_long_conv(z, kernel) + z * p['hy_skip']
    return x0 * y


def token_mixer(h, p, ctx_kv_c, ctx_k_rope):
    B, L, _ = h.shape
    proj = h @ p['w_in']
    q_c, kv_c, k_r, hy_in, g_a, g_b = jnp.split(proj, IN_SPLITS, axis=-1)
    q = (rms_norm(q_c, p['q_norm_g']) @ p['w_uq']).reshape(B, L, MLA_HEADS, QK_NOPE + QK_ROPE)
    q_nope, q_rope = q[..., :QK_NOPE], q[..., QK_NOPE:]
    c_kv = rms_norm(kv_c, p['kv_norm_g'])
    if ctx_kv_c is None:
        k_rope = k_r
        kv_all, kr_all = c_kv, k_rope
    else:
        cos, sin = axial_rope_tables(L, h.dtype)
        q_rope = apply_axial_rope(q_rope, cos[:, None], sin[:, None])
        k_rope = apply_axial_rope(k_r, cos, sin)
        kv_all = jnp.concatenate([ctx_kv_c, c_kv], axis=1)
        kr_all = jnp.concatenate([ctx_k_rope, k_rope], axis=1)
    kv = (kv_all @ p['w_ukv']).reshape(B, kv_all.shape[1], MLA_HEADS, QK_NOPE + V_HEAD)
    attn = mla_attention(q_nope, q_rope, kv[..., :QK_NOPE], kr_all, kv[..., QK_NOPE:])
    hy = hyena_branch(hy_in, p)
    merged = jax.nn.sigmoid(g_a) * (attn @ p['w_o_mla']) + jax.nn.sigmoid(g_b) * (hy @ p['w_o_hy'])
    return merged @ p['w_out'], c_kv, k_rope


def expert_choice_ffn(h, p):
    B, L, D = h.shape
    tok = h.reshape(B * L, D)
    cap = EC_CAPACITY * (B * L) // N_EXPERTS
    aff = jax.nn.softmax((tok @ p['w_router']).astype(jnp.float32), axis=-1)
    gate, idx = lax.top_k(aff.T, cap)
    xe = tok[idx]
    hid = jax.nn.silu(jnp.einsum('ecd,edf->ecf', xe, p['w_gate'])) * jnp.einsum('ecd,edf->ecf', xe, p['w_up'])
    ye = jnp.einsum('ecf,efd->ecd', hid, p['w_down']) * gate[..., None].astype(tok.dtype)
    out = jnp.zeros_like(tok).at[idx.reshape(-1)].add(ye.reshape(-1, D))
    return out.reshape(B, L, D)


def trunk_layer(x, mods, p, ctx_kv_c, ctx_k_rope):
    shift1, scale1, gate1, shift2, scale2, gate2 = mods
    mix, c_kv, k_rope = token_mixer(x * (1 + scale1) + shift1, p, ctx_kv_c, ctx_k_rope)
    x = layer_norm(ALPHA * x + gate1 * mix, p['ln1_g'], p['ln1_b'])
    ffn = expert_choice_ffn(x * (1 + scale2) + shift2, p)
    x = layer_norm(ALPHA * x + gate2 * ffn, p['ln2_g'], p['ln2_b'])
    return x, c_kv, k_rope


def setup_inputs(seed: int = 0) -> dict:
    key = jax.random.key(seed)
    keys = iter(jax.random.split(key, 64))

    def nrm(shape, std):
        return jax.random.normal(next(keys), shape, jnp.float32) * std

    def gain(shape):
        return 1.0 + nrm(shape, 0.02)

    hy3 = 3 * HY_WIDTH
    return {
        'x_prompt': nrm((BATCH, SEQ, D_MODEL), 1.0),
        'x_sample': nrm((DEC_BATCH, DEC_SEQ, D_MODEL), 1.0),
        'cache_kv_c': nrm((DEC_BATCH, DEPTH, PAST_LEN, KV_LORA), 1.0),
        'cache_k_rope': nrm((DEC_BATCH, DEPTH, PAST_LEN, QK_ROPE), 1.0),
        'c': nrm((DEC_BATCH, D_MODEL), 1.0),
        'c_ctx': nrm((D_MODEL,), 1.0),
        'w_ada': nrm((DEPTH, D_MODEL, N_MOD * D_MODEL), D_MODEL ** -0.5),
        'b_ada': nrm((DEPTH, N_MOD * D_MODEL), 0.02),
        'w_in': nrm((DEPTH, D_MODEL, IN_WIDTH), D_MODEL ** -0.5),
        'q_norm_g': gain((DEPTH, Q_LORA)),
        'kv_norm_g': gain((DEPTH, KV_LORA)),
        'w_uq': nrm((DEPTH, Q_LORA, MLA_HEADS * (QK_NOPE + QK_ROPE)), Q_LORA ** -0.5),
        'w_ukv': nrm((DEPTH, KV_LORA, MLA_HEADS * (QK_NOPE + V_HEAD)), KV_LORA ** -0.5),
        'w_o_mla': nrm((DEPTH, MLA_WIDTH, D_MODEL), MLA_WIDTH ** -0.5),
        'hy_short_w': nrm((DEPTH, 3, hy3), 3 ** -0.5),
        'hy_short_b': nrm((DEPTH, hy3), 0.02),
        'hy_filt_w1': nrm((DEPTH, HY_EMB, HY_HIDDEN), HY_EMB ** -0.5),
        'hy_filt_b1': nrm((DEPTH, HY_HIDDEN), 0.02),
        'hy_filt_freq1': gain((DEPTH, HY_HIDDEN)),
        'hy_filt_w2': nrm((DEPTH, HY_HIDDEN, HY_HIDDEN), HY_HIDDEN ** -0.5),
        'hy_filt_b2': nrm((DEPTH, HY_HIDDEN), 0.02),
        'hy_filt_freq2': gain((DEPTH, HY_HIDDEN)),
        'hy_filt_w3': nrm((DEPTH, HY_HIDDEN, 2 * HY_WIDTH), HY_HIDDEN ** -0.5),
        'hy_filt_b3': nrm((DEPTH, 2 * HY_WIDTH), 0.02),
        'hy_skip': nrm((DEPTH, HY_WIDTH), 0.5),
        'w_o_hy': nrm((DEPTH, HY_WIDTH, D_MODEL), HY_WIDTH ** -0.5),
        'w_out': nrm((DEPTH, D_MODEL, D_MODEL), BETA * D_MODEL ** -0.5),
        'ln1_g': gain((DEPTH, D_MODEL)),
        'ln1_b': nrm((DEPTH, D_MODEL), 0.02),
        'ln2_g': gain((DEPTH, D_MODEL)),
        'ln2_b': nrm((DEPTH, D_MODEL), 0.02),
        'w_router': nrm((DEPTH, D_MODEL, N_EXPERTS), D_MODEL ** -0.5),
        'w_gate': nrm((DEPTH, N_EXPERTS, D_MODEL, EXPERT_FF), D_MODEL ** -0.5),
        'w_up': nrm((DEPTH, N_EXPERTS, D_MODEL, EXPERT_FF), D_MODEL ** -0.5),
        'w_down': nrm((DEPTH, N_EXPERTS, EXPERT_FF, D_MODEL), BETA * EXPERT_FF ** -0.5),
    }


def reference(x_prompt, x_sample, cache_kv_c, cache_k_rope, c, c_ctx, w_ada, b_ada, w_in,
              q_norm_g, kv_norm_g, w_uq, w_ukv, w_o_mla, hy_short_w, hy_short_b,
              hy_filt_w1, hy_filt_b1, hy_filt_freq1, hy_filt_w2, hy_filt_b2, hy_filt_freq2,
              hy_filt_w3, hy_filt_b3, hy_skip, w_o_hy, w_out, ln1_g, ln1_b, ln2_g, ln2_b,
              w_router, w_gate, w_up, w_down):
    y_prompt = x_prompt
    y_sample = x_sample
    kv_list = []
    kr_list = []
    for l in range(DEPTH):
        p = {
            'w_in': w_in[l], 'q_norm_g': q_norm_g[l], 'kv_norm_g': kv_norm_g[l],
            'w_uq': w_uq[l], 'w_ukv': w_ukv[l], 'w_o_mla': w_o_mla[l],
            'hy_short_w': hy_short_w[l], 'hy_short_b': hy_short_b[l],
            'hy_filt_w1': hy_filt_w1[l], 'hy_filt_b1': hy_filt_b1[l], 'hy_filt_freq1': hy_filt_freq1[l],
            'hy_filt_w2': hy_filt_w2[l], 'hy_filt_b2': hy_filt_b2[l], 'hy_filt_freq2': hy_filt_freq2[l],
            'hy_filt_w3': hy_filt_w3[l], 'hy_filt_b3': hy_filt_b3[l], 'hy_skip': hy_skip[l],
            'w_o_hy': w_o_hy[l], 'w_out': w_out[l],
            'ln1_g': ln1_g[l], 'ln1_b': ln1_b[l], 'ln2_g': ln2_g[l], 'ln2_b': ln2_b[l],
            'w_router': w_router[l], 'w_gate': w_gate[l], 'w_up': w_up[l], 'w_down': w_down[l],
        }
        mods_ctx = adaln_mod(c_ctx, w_ada[l], b_ada[l])
        y_prompt, c_kv, k_rope = trunk_layer(y_prompt, mods_ctx, p, None, None)
        kv_list.append(c_kv)
        kr_list.append(k_rope)
        mods_lat = adaln_mod(c, w_ada[l], b_ada[l])
        y_sample, _, _ = trunk_layer(y_sample, mods_lat, p, cache_kv_c[:, l], cache_k_rope[:, l])
    new_kv_c = jnp.stack(kv_list, axis=1)
    new_k_rope = jnp.stack(kr_list, axis=1)
    return (y_prompt, y_sample, new_kv_c, new_k_rope)
```

```python
import functools
import math

import numpy as np
import jax
import jax.numpy as jnp
from jax import lax
from jax.experimental import pallas as pl
from jax.experimental.pallas import tpu as pltpu

F32 = jnp.float32
BF16 = jnp.bfloat16
I32 = jnp.int32
HIGHEST = lax.Precision.HIGHEST

D_MODEL = 2048
GRID_W = 64
MLA_HEADS = 8
QK_NOPE = 128
QK_ROPE = 64
V_HEAD = 128
Q_LORA = 512
KV_LORA = 256
ROPE_THETA = 10000.0
ROPE_FREQS = QK_ROPE // 4
HY_WIDTH = 1024
HY_EMB = 33
HY_BANDS = (HY_EMB - 1) // 2
HY_HIDDEN = 64
HY_FAST_DECAY = 0.3
HY_SLOW_DECAY = 1.5
HY_TARGET = 0.01
N_EXPERTS = 16
EXPERT_FF = 1024
EC_CAPACITY = 2
EPS = 1e-6
DEPTH = 1
ALPHA = (2 * DEPTH) ** 0.25
N_MOD = 6

LANES = 128
HEAD_PAD = 256
SMALL_W = 1024
BIG_W = 2 * D_MODEL + 3 * HY_WIDTH
VMEM_LIMIT = 56 * 1024 * 1024


def _cparams(sem, vmem=VMEM_LIMIT):
    return pltpu.CompilerParams(dimension_semantics=sem, vmem_limit_bytes=vmem)


def _sigmoid(x):
    return 1.0 / (1.0 + jnp.exp(-x))


def _const_spec(shape):
    nd = len(shape)
    return pl.BlockSpec(shape, lambda *_: (0,) * nd, pipeline_mode=pl.Buffered(1))


def _ada_kernel(c_ref, w_ref, b_ref, o_ref):
    c = c_ref[...]
    s = (c * _sigmoid(c)).astype(BF16)
    o_ref[...] = jnp.dot(s, w_ref[...].astype(BF16), preferred_element_type=F32) + b_ref[...]


def _ada_mod(cond, w_ada, b_ada):
    r, d = cond.shape
    n = w_ada.shape[1]
    tn = 512
    return pl.pallas_call(
        _ada_kernel,
        grid=(n // tn,),
        in_specs=[pl.BlockSpec((r, d), lambda j: (0, 0)),
                  pl.BlockSpec((d, tn), lambda j: (0, j)),
                  pl.BlockSpec((1, tn), lambda j: (0, j))],
        out_specs=pl.BlockSpec((r, tn), lambda j: (0, j)),
        out_shape=jax.ShapeDtypeStruct((r, n), F32),
        compiler_params=_cparams(("arbitrary",)),
        name="ada_mod",
    )(cond, w_ada, b_ada.reshape(1, n))


def _inproj_kernel(x_ref, mod_ref, w_ref, small_ref, big_ref, h_scr, *, n_small):
    j = pl.program_id(1)

    @pl.when(j == 0)
    def _():
        m = mod_ref[0]
        h_scr[...] = (x_ref[...] * (1.0 + m[1:2]) + m[0:1]).astype(BF16)

    acc = jnp.dot(h_scr[...], w_ref[...], preferred_element_type=F32)

    @pl.when(j < n_small)
    def _():
        small_ref[...] = acc

    @pl.when(j >= n_small)
    def _():
        big_ref[...] = acc.astype(BF16)


def _inproj(x, mods, mod_row, w_r, tm):
    n, d = x.shape
    tn = 512
    n_small = SMALL_W // tn
    n_cols = w_r.shape[1] // tn
    return pl.pallas_call(
        functools.partial(_inproj_kernel, n_small=n_small),
        grid=(n // tm, n_cols),
        in_specs=[pl.BlockSpec((tm, d), lambda i, j: (i, 0)),
                  pl.BlockSpec((1, N_MOD, d), lambda i, j: (mod_row(i * tm), 0, 0)),
                  pl.BlockSpec((d, tn), lambda i, j: (0, j))],
        out_specs=[pl.BlockSpec((tm, tn), lambda i, j: (i, jnp.minimum(j, n_small - 1))),
                   pl.BlockSpec((tm, tn), lambda i, j: (i, jnp.maximum(j - n_small, 0)))],
        out_shape=[jax.ShapeDtypeStruct((n, SMALL_W), F32),
                   jax.ShapeDtypeStruct((n, BIG_W), BF16)],
        scratch_shapes=[pltpu.VMEM((tm, d), BF16)],
        compiler_params=_cparams(("arbitrary", "arbitrary")),
        name="inproj",
    )(x, mods, w_r)


def _qprep_kernel(*refs, rope):
    if rope:
        small_ref, qg_ref, kvg_ref, wa_ref, wb_ref, c_ref, s_ref, q_ref, ckv_ref, kr_ref = refs
    else:
        small_ref, qg_ref, kvg_ref, wa_ref, q_ref, ckv_ref, kr_ref = refs
    qc = small_ref[:, 0:Q_LORA]
    kvc = small_ref[:, Q_LORA:Q_LORA + KV_LORA]
    kr = small_ref[:, Q_LORA + KV_LORA:Q_LORA + KV_LORA + LANES]
    qn = (qc * lax.rsqrt(jnp.mean(qc * qc, axis=-1, keepdims=True) + EPS) * qg_ref[...]).astype(BF16)
    ckv_ref[...] = kvc * lax.rsqrt(jnp.mean(kvc * kvc, axis=-1, keepdims=True) + EPS) * kvg_ref[...]
    qa = jnp.dot(qn, wa_ref[...], preferred_element_type=F32)
    if rope:
        krs = small_ref[:, Q_LORA + KV_LORA + LANES:SMALL_W]
        qb = jnp.dot(qn, wb_ref[...], preferred_element_type=F32)
        cs = c_ref[...]
        sn = s_ref[...]
        kr_ref[...] = kr * cs + krs * sn
        for h in range(MLA_HEADS):
            lo = h * HEAD_PAD
            q_ref[:, lo:lo + LANES] = qa[:, lo:lo + LANES].astype(BF16)
            q_ref[:, lo + LANES:lo + HEAD_PAD] = (
                qa[:, lo + LANES:lo + HEAD_PAD] * cs + qb[:, h * LANES:(h + 1) * LANES] * sn).astype(BF16)
    else:
        kr_ref[...] = kr
        q_ref[...] = qa.astype(BF16)


def _qprep(small, qg, kvg, wa, wb, cos128, sin128, seq, rope):
    n = small.shape[0]
    tm = min(512, seq)
    nblk = seq // tm
    in_specs = [pl.BlockSpec((tm, SMALL_W), lambda i: (i, 0)),
                pl.BlockSpec((1, Q_LORA), lambda i: (0, 0)),
                pl.BlockSpec((1, KV_LORA), lambda i: (0, 0)),
                pl.BlockSpec(wa.shape, lambda i: (0, 0))]
    args = [small, qg, kvg, wa]
    if rope:
        in_specs += [pl.BlockSpec(wb.shape, lambda i: (0, 0)),
                     pl.BlockSpec((tm, LANES), lambda i: (i % nblk, 0)),
                     pl.BlockSpec((tm, LANES), lambda i: (i % nblk, 0))]
        args += [wb, cos128, sin128]
    return pl.pallas_call(
        functools.partial(_qprep_kernel, rope=rope),
        grid=(n // tm,),
        in_specs=in_specs,
        out_specs=[pl.BlockSpec((tm, MLA_HEADS * HEAD_PAD), lambda i: (i, 0)),
                   pl.BlockSpec((tm, KV_LORA), lambda i: (i, 0)),
                   pl.BlockSpec((tm, LANES), lambda i: (i, 0))],
        out_shape=[jax.ShapeDtypeStruct((n, MLA_HEADS * HEAD_PAD), BF16),
                   jax.ShapeDtypeStruct((n, KV_LORA), F32),
                   jax.ShapeDtypeStruct((n, LANES), F32)],
        compiler_params=_cparams(("arbitrary",)),
        name="qprep",
    )(*args)


def _kvup_kernel(kv_ref, kr_ref, wk_ref, wv_ref, k_ref, v_ref):
    kv = kv_ref[0].astype(BF16)
    kn = jnp.dot(kv, wk_ref[...], preferred_element_type=F32)
    vv = jnp.dot(kv, wv_ref[...], preferred_element_type=F32)
    krp = kr_ref[0].astype(BF16)
    for h in range(MLA_HEADS):
        k_ref[0, h, :, 0:LANES] = kn[:, h * QK_NOPE:(h + 1) * QK_NOPE].astype(BF16)
        k_ref[0, h, :, LANES:HEAD_PAD] = krp
        v_ref[0, h] = vv[:, h * V_HEAD:(h + 1) * V_HEAD].astype(BF16)


def _kvup(kv_all, kr_all, wk, wv):
    b, lk, _ = kv_all.shape
    tm = 512 if lk % 512 == 0 else 256
    return pl.pallas_call(
        _kvup_kernel,
        grid=(b, lk // tm),
        in_specs=[pl.BlockSpec((1, tm, KV_LORA), lambda bi, i: (bi, i, 0)),
                  pl.BlockSpec((1, tm, LANES), lambda bi, i: (bi, i, 0)),
                  pl.BlockSpec(wk.shape, lambda bi, i: (0, 0)),
                  pl.BlockSpec(wv.shape, lambda bi, i: (0, 0))],
        out_specs=[pl.BlockSpec((1, MLA_HEADS, tm, HEAD_PAD), lambda bi, i: (bi, 0, i, 0)),
                   pl.BlockSpec((1, MLA_HEADS, tm, V_HEAD), lambda bi, i: (bi, 0, i, 0))],
        out_shape=[jax.ShapeDtypeStruct((b, MLA_HEADS, lk, HEAD_PAD), BF16),
                   jax.ShapeDtypeStruct((b, MLA_HEADS, lk, V_HEAD), BF16)],
        compiler_params=_cparams(("arbitrary", "arbitrary")),
        name="kvup",
    )(kv_all, kr_all, wk, wv)


def _attn_kernel(q_ref, k_ref, v_ref, o_ref):
    s = lax.dot_general(q_ref[0], k_ref[0, 0], (((1,), (1,)), ((), ())), preferred_element_type=F32)
    m = jnp.max(s, axis=-1, keepdims=True)
    p = jnp.exp(s - m)
    l = jnp.sum(p, axis=-1, keepdims=True)
    o = jnp.dot(p.astype(BF16), v_ref[0, 0], preferred_element_type=F32)
    o_ref[0] = (o / l).astype(BF16)


def _attention(q, k, v):
    b, l, _ = q.shape
    lk = k.shape[2]
    tq = 256
    return pl.pallas_call(
        _attn_kernel,
        grid=(b, MLA_HEADS, l // tq),
        in_specs=[pl.BlockSpec((1, tq, HEAD_PAD), lambda bi, h, i: (bi, i, h)),
                  pl.BlockSpec((1, 1, lk, HEAD_PAD), lambda bi, h, i: (bi, h, 0, 0)),
                  pl.BlockSpec((1, 1, lk, V_HEAD), lambda bi, h, i: (bi, h, 0, 0))],
        out_specs=pl.BlockSpec((1, tq, V_HEAD), lambda bi, h, i: (bi, i, h)),
        out_shape=jax.ShapeDtypeStruct((b, l, MLA_HEADS * V_HEAD), BF16),
        compiler_params=_cparams(("arbitrary", "arbitrary", "arbitrary")),
        name="attention",
    )(q, k, v)


def _hypre_kernel(x0_ref, x1_ref, v_ref, w_ref, b_ref, x0o_ref, z_ref, zn_ref):
    seq = x0_ref.shape[1]
    row = lax.broadcasted_iota(I32, (seq, 1), 0)

    def conv(u_ref, g):
        u = u_ref[0].astype(F32)
        up = jnp.where(row == 0, 0.0, pltpu.roll(u, 1, axis=0))
        un = jnp.where(row == seq - 1, 0.0, pltpu.roll(u, seq - 1, axis=0))
        w = w_ref[g]
        return up * w[0:1] + u * w[1:2] + un * w[2:3] + b_ref[g]

    x0o_ref[0] = conv(x0_ref, 0).astype(BF16)
    z = conv(x1_ref, 1) * conv(v_ref, 2)
    z_ref[0] = z.astype(BF16)
    sign = (1 - 2 * (row & 1)).astype(F32)
    zn_ref[0] = jnp.sum(z * sign, axis=0, keepdims=True)


def _hypre(big3, w3, b3):
    b, l, _ = big3.shape
    cb = 256
    ncb = HY_WIDTH // cb
    off = 2 * D_MODEL // cb
    return pl.pallas_call(
        _hypre_kernel,
        grid=(b, ncb),
        in_specs=[pl.BlockSpec((1, l, cb), lambda bi, j: (bi, 0, off + j)),
                  pl.BlockSpec((1, l, cb), lambda bi, j: (bi, 0, off + j + ncb)),
                  pl.BlockSpec((1, l, cb), lambda bi, j: (bi, 0, off + j + 2 * ncb)),
                  pl.BlockSpec((3, 3, cb), lambda bi, j: (0, 0, j)),
                  pl.BlockSpec((3, 1, cb), lambda bi, j: (0, 0, j))],
        out_specs=[pl.BlockSpec((1, l, cb), lambda bi, j: (bi, 0, j)),
                   pl.BlockSpec((1, l, cb), lambda bi, j: (bi, 0, j)),
                   pl.BlockSpec((1, 1, cb), lambda bi, j: (bi, 0, j))],
        out_shape=[jax.ShapeDtypeStruct((b, l, HY_WIDTH), BF16),
                   jax.ShapeDtypeStruct((b, l, HY_WIDTH), BF16),
                   jax.ShapeDtypeStruct((b, 1, HY_WIDTH), F32)],
        compiler_params=_cparams(("arbitrary", "arbitrary")),
        name="hyena_pre",
    )(big3, big3, big3, w3, b3)


def _filter_kernel(feat_ref, w1_ref, b1_ref, f1_ref, w2_ref, b2_ref, f2_ref,
                   w3f_ref, w3b_ref, b3f_ref, b3b_ref, dl_ref, a_ref, d_ref, kn_ref):
    seq = feat_ref.shape[0]
    hd = jnp.sin(f1_ref[...] * (jnp.dot(feat_ref[...], w1_ref[...], precision=HIGHEST,
                                        preferred_element_type=F32) + b1_ref[...]))
    hd = jnp.sin(f2_ref[...] * (jnp.dot(hd, w2_ref[...], precision=HIGHEST,
                                        preferred_element_type=F32) + b2_ref[...]))
    row = lax.broadcasted_iota(I32, (seq, 1), 0)
    t = row.astype(F32) / float(max(seq - 1, 1))
    window = jnp.exp(-t * dl_ref[...])
    hf = (jnp.dot(hd, w3f_ref[...], precision=HIGHEST, preferred_element_type=F32) + b3f_ref[...]) * window
    hb = (jnp.dot(hd, w3b_ref[...], precision=HIGHEST, preferred_element_type=F32) + b3b_ref[...]) * window
    hb = jnp.where(row == 0, 0.0, hb)
    den = jnp.sum(jnp.abs(hf) + jnp.abs(hb), axis=0, keepdims=True) + EPS
    a = (hf + hb) / den
    a_ref[...] = a.astype(BF16)
    d_ref[...] = ((hf - hb) / den).astype(BF16)
    sign = (1 - 2 * (row & 1)).astype(F32)
    kn_ref[...] = jnp.sum(a * sign, axis=0, keepdims=True)


def _filter_td(seq, p):
    pos = np.arange(seq, dtype=np.float64)
    t = pos / max(seq - 1, 1)
    bands = np.linspace(1e-4, HY_BANDS - 1, HY_BANDS)
    w = 2.0 * math.pi * pos / seq
    feats = np.concatenate([t[:, None], np.cos(w[:, None] * bands), -np.sin(w[:, None] * bands)], axis=-1)
    feats = jnp.asarray(np.pad(feats, ((0, 0), (0, LANES - HY_EMB))), F32)
    w1 = jnp.pad(p['hy_filt_w1'], ((0, LANES - HY_EMB), (0, 0)))
    deltas = jnp.asarray(np.abs(np.linspace(math.log(HY_TARGET) / HY_SLOW_DECAY, math.log(HY_TARGET) / HY_FAST_DECAY,
                                            HY_WIDTH)).reshape(1, HY_WIDTH), F32)
    cb = 256
    ncb = HY_WIDTH // cb
    row = lambda v: v.reshape(1, -1)
    c2 = lambda j: (0, 0)
    return pl.pallas_call(
        _filter_kernel,
        grid=(ncb,),
        in_specs=[pl.BlockSpec((seq, LANES), c2),
                  pl.BlockSpec((LANES, HY_HIDDEN), c2), pl.BlockSpec((1, HY_HIDDEN), c2),
                  pl.BlockSpec((1, HY_HIDDEN), c2),
                  pl.BlockSpec((HY_HIDDEN, HY_HIDDEN), c2), pl.BlockSpec((1, HY_HIDDEN), c2),
                  pl.BlockSpec((1, HY_HIDDEN), c2),
                  pl.BlockSpec((HY_HIDDEN, cb), lambda j: (0, j)),
                  pl.BlockSpec((HY_HIDDEN, cb), lambda j: (0, j + ncb)),
                  pl.BlockSpec((1, cb), lambda j: (0, j)),
                  pl.BlockSpec((1, cb), lambda j: (0, j + ncb)),
                  pl.BlockSpec((1, cb), lambda j: (0, j))],
        out_specs=[pl.BlockSpec((seq, cb), lambda j: (0, j)),
                   pl.BlockSpec((seq, cb), lambda j: (0, j)),
                   pl.BlockSpec((1, cb), lambda j: (0, j))],
        out_shape=[jax.ShapeDtypeStruct((seq, HY_WIDTH), BF16),
                   jax.ShapeDtypeStruct((seq, HY_WIDTH), BF16),
                   jax.ShapeDtypeStruct((1, HY_WIDTH), F32)],
        compiler_params=_cparams(("arbitrary",)),
        name="hyena_filter",
    )(feats, w1, row(p['hy_filt_b1']), row(p['hy_filt_freq1']),
      p['hy_filt_w2'], row(p['hy_filt_b2']), row(p['hy_filt_freq2']),
      p['hy_filt_w3'], p['hy_filt_w3'], row(p['hy_filt_b3']), row(p['hy_filt_b3']), deltas)


def _kspec_kernel(c_ref, s_ref, a_ref, d_ref, kc_ref, ks_ref, *, seq):
    tf = c_ref.shape[0]
    f = pl.program_id(0) * tf + lax.broadcasted_iota(I32, (tf, 1), 0)
    wgt = jnp.where(f == 0, 1.0, 2.0) / float(2 * seq)
    kc_ref[...] = jnp.dot(c_ref[...], a_ref[...], preferred_element_type=F32) * wgt
    ks_ref[...] = jnp.dot(s_ref[...], d_ref[...], preferred_element_type=F32) * wgt


def _kspec(cmat, smat, a, d):
    seq = cmat.shape[0]
    tf = min(512, seq)
    cb = 512
    return pl.pallas_call(
        functools.partial(_kspec_kernel, seq=seq),
        grid=(seq // tf, HY_WIDTH // cb),
        in_specs=[pl.BlockSpec((tf, seq), lambda i, j: (i, 0)),
                  pl.BlockSpec((tf, seq), lambda i, j: (i, 0)),
                  pl.BlockSpec((seq, cb), lambda i, j: (0, j)),
                  pl.BlockSpec((seq, cb), lambda i, j: (0, j))],
        out_specs=[pl.BlockSpec((tf, cb), lambda i, j: (i, j)),
                   pl.BlockSpec((tf, cb), lambda i, j: (i, j))],
        out_shape=[jax.ShapeDtypeStruct((seq, HY_WIDTH), F32),
                   jax.ShapeDtypeStruct((seq, HY_WIDTH), F32)],
        compiler_params=_cparams(("arbitrary", "arbitrary")),
        name="hyena_kspec",
    )(cmat, smat, a, d)


def _fwd_kernel(c_ref, s_ref, z_ref, kc_ref, ks_ref, pc_ref, ps_ref):
    z = z_ref[0]
    zc = jnp.dot(c_ref[...], z, preferred_element_type=F32)
    zs = jnp.dot(s_ref[...], z, preferred_element_type=F32)
    kc = kc_ref[...]
    ks = ks_ref[...]
    pc_ref[0] = (zc * kc - zs * ks).astype(BF16)
    ps_ref[0] = (zc * ks + zs * kc).astype(BF16)


def _hy_fwd(cmat, smat, z, kc, ks):
    b, seq, _ = z.shape
    tf = min(1024, seq)
    cb = 512
    return pl.pallas_call(
        _fwd_kernel,
        grid=(HY_WIDTH // cb, seq // tf, b),
        in_specs=[pl.BlockSpec((tf, seq), lambda j, i, bi: (i, 0)),
                  pl.BlockSpec((tf, seq), lambda j, i, bi: (i, 0)),
                  pl.BlockSpec((1, seq, cb), lambda j, i, bi: (bi, 0, j)),
                  pl.BlockSpec((tf, cb), lambda j, i, bi: (i, j)),
                  pl.BlockSpec((tf, cb), lambda j, i, bi: (i, j))],
        out_specs=[pl.BlockSpec((1, tf, cb), lambda j, i, bi: (bi, i, j)),
                   pl.BlockSpec((1, tf, cb), lambda j, i, bi: (bi, i, j))],
        out_shape=[jax.ShapeDtypeStruct((b, seq, HY_WIDTH), BF16),
                   jax.ShapeDtypeStruct((b, seq, HY_WIDTH), BF16)],
        compiler_params=_cparams(("arbitrary", "arbitrary", "arbitrary")),
        name="hyena_fwd",
    )(cmat, smat, z, kc, ks)


def _inv_kernel(c_ref, s_ref, pc_ref, ps_ref, z_ref, x0_ref, zn_ref, kn_ref, skip_ref, o_ref, *, seq):
    tt = c_ref.shape[0]
    y = jnp.dot(c_ref[...], pc_ref[0], preferred_element_type=F32)
    y = y + jnp.dot(s_ref[...], ps_ref[0], preferred_element_type=F32)
    t = pl.program_id(1) * tt + lax.broadcasted_iota(I32, (tt, 1), 0)
    sign = (1 - 2 * (t & 1)).astype(F32)
    nyq = zn_ref[0] * kn_ref[...] * (1.0 / float(2 * seq))
    y = y + sign * nyq + z_ref[0].astype(F32) * skip_ref[...]
    o_ref[0] = (x0_ref[0].astype(F32) * y).astype(BF16)


def _hy_inv(cmat, smat, pc, ps, z, x0, zn, kn, skip):
    b, seq, _ = z.shape
    tt = min(1024, seq)
    cb = 512
    return pl.pallas_call(
        functools.partial(_inv_kernel, seq=seq),
        grid=(HY_WIDTH // cb, seq // tt, b),
        in_specs=[pl.BlockSpec((tt, seq), lambda j, i, bi: (i, 0)),
                  pl.BlockSpec((tt, seq), lambda j, i, bi: (i, 0)),
                  pl.BlockSpec((1, seq, cb), lambda j, i, bi: (bi, 0, j)),
                  pl.BlockSpec((1, seq, cb), lambda j, i, bi: (bi, 0, j)),
                  pl.BlockSpec((1, tt, cb), lambda j, i, bi: (bi, i, j)),
                  pl.BlockSpec((1, tt, cb), lambda j, i, bi: (bi, i, j)),
                  pl.BlockSpec((1, 1, cb), lambda j, i, bi: (bi, 0, j)),
                  pl.BlockSpec((1, cb), lambda j, i, bi: (0, j)),
                  pl.BlockSpec((1, cb), lambda j, i, bi: (0, j))],
        out_specs=pl.BlockSpec((1, tt, cb), lambda j, i, bi: (bi, i, j)),
        out_shape=jax.ShapeDtypeStruct((b, seq, HY_WIDTH), BF16),
        compiler_params=_cparams(("arbitrary", "arbitrary", "arbitrary")),
        name="hyena_inv",
    )(cmat, smat, pc, ps, z, x0, zn, kn, skip)


def _layer_norm(v, g, b):
    mu = jnp.mean(v, axis=-1, keepdims=True)
    vc = v - mu
    var = jnp.mean(vc * vc, axis=-1, keepdims=True)
    return vc * lax.rsqrt(var + EPS) * g + b


def _merge_kernel(at_ref, hy_ref, ga_ref, gb_ref, woa_ref, woh_ref, o_ref):
    a = jnp.dot(at_ref[...], woa_ref[...], preferred_element_type=F32)
    hh = jnp.dot(hy_ref[...], woh_ref[...], preferred_element_type=F32)
    merged = _sigmoid(ga_ref[...].astype(F32)) * a + _sigmoid(gb_ref[...].astype(F32)) * hh
    o_ref[...] = merged.astype(BF16)


def _merge(attn, hy, big, woa, woh):
    n = attn.shape[0]
    d = woa.shape[1]
    tm = 512
    return pl.pallas_call(
        _merge_kernel,
        grid=(n // tm,),
        in_specs=[pl.BlockSpec((tm, attn.shape[1]), lambda i: (i, 0)),
                  pl.BlockSpec((tm, hy.shape[1]), lambda i: (i, 0)),
                  pl.BlockSpec((tm, d), lambda i: (i, 0)),
                  pl.BlockSpec((tm, d), lambda i: (i, 1)),
                  _const_spec(woa.shape), _const_spec(woh.shape)],
        out_specs=pl.BlockSpec((tm, d), lambda i: (i, 0)),
        out_shape=jax.ShapeDtypeStruct((n, d), BF16),
        compiler_params=_cparams(("arbitrary",)),
        name="merge",
    )(attn, hy, big, big, woa, woh)


def _ln1_kernel(x_ref, mg_ref, mod_ref, wout_ref, g_ref, b_ref, wrh_ref, wrl_ref, x1_ref, h2_ref, lg_ref):
    m = mod_ref[0]
    mix = jnp.dot(mg_ref[...], wout_ref[...], preferred_element_type=F32)
    x1 = _layer_norm(ALPHA * x_ref[...] + m[2:3] * mix, g_ref[...], b_ref[...])
    x1_ref[...] = x1
    h2 = x1 * (1.0 + m[4:5]) + m[3:4]
    h2_ref[...] = h2
    hi = h2.astype(BF16)
    lo = (h2 - hi.astype(F32)).astype(BF16)
    lg = jnp.dot(hi, wrh_ref[...], preferred_element_type=F32)
    lg = lg + jnp.dot(lo, wrh_ref[...], preferred_element_type=F32)
    lg_ref[...] = lg + jnp.dot(hi, wrl_ref[...], preferred_element_type=F32)


def _ln1(x, merged, mods, mod_row, wout, g, b, wr_hi, wr_lo):
    n, d = x.shape
    tm = 512
    return pl.pallas_call(
        _ln1_kernel,
        grid=(n // tm,),
        in_specs=[pl.BlockSpec((tm, d), lambda i: (i, 0)),
                  pl.BlockSpec((tm, d), lambda i: (i, 0)),
                  pl.BlockSpec((1, N_MOD, d), lambda i: (mod_row(i * tm), 0, 0)),
                  _const_spec(wout.shape), _const_spec((1, d)), _const_spec((1, d)),
                  _const_spec(wr_hi.shape), _const_spec(wr_lo.shape)],
        out_specs=[pl.BlockSpec((tm, d), lambda i: (i, 0)),
                   pl.BlockSpec((tm, d), lambda i: (i, 0)),
                   pl.BlockSpec((tm, LANES), lambda i: (i, 0))],
        out_shape=[jax.ShapeDtypeStruct((n, d), F32),
                   jax.ShapeDtypeStruct((n, d), F32),
                   jax.ShapeDtypeStruct((n, LANES), F32)],
        compiler_params=_cparams(("arbitrary",)),
        name="ln1_router",
    )(x, merged, mods, wout, g, b, wr_hi, wr_lo)


def _split3(t):
    t1 = t.astype(BF16)
    r = t - t1.astype(F32)
    t2 = r.astype(BF16)
    t3 = (r - t2.astype(F32)).astype(BF16)
    return t1, t2, t3


def _dot_exact(sel01, table):
    t1, t2, t3 = _split3(table)
    out = jnp.dot(sel01, t1, preferred_element_type=F32)
    out = out + jnp.dot(sel01, t2, preferred_element_type=F32)
    return out + jnp.dot(sel01, t3, preferred_element_type=F32)


def _route_kernel(lg_ref, idx_ref, q_ref, gate_ref, start_ref, aff_scr, sel_scr, pos_scr, *, cap):
    ne, nc, _ = lg_ref.shape
    lg = lg_ref[...]
    mx = jnp.max(lg, axis=0, keepdims=True)
    ex = jnp.exp(lg - mx)
    aff = ex / jnp.sum(ex, axis=0, keepdims=True)
    aff_scr[...] = aff

    def tbody(k, thr):
        cand = thr | lax.shift_left(jnp.int32(1), 30 - k)
        ge = jnp.where(aff >= lax.bitcast_convert_type(cand, F32), 1.0, 0.0)
        cnt = jnp.sum(jnp.sum(ge, axis=1, keepdims=True), axis=2, keepdims=True)
        return jnp.where(cnt >= float(cap), cand, thr)

    thr = lax.fori_loop(0, 31, tbody, jnp.zeros((ne, 1, 1), I32))
    lo = lax.bitcast_convert_type(thr, F32)
    above = aff >= lax.bitcast_convert_type(thr + 1, F32)
    sel_scr[...] = jnp.where(above, 1.0, 0.0)
    pos_scr[...] = jnp.where((aff >= lo) & jnp.logical_not(above), aff, -1.0)

    r_i = lax.broadcasted_iota(I32, (LANES, LANES), 0)
    c_i = lax.broadcasted_iota(I32, (LANES, LANES), 1)
    upper_incl = jnp.where(r_i <= c_i, 1.0, 0.0).astype(BF16)
    ones_sq = jnp.ones((LANES, LANES), BF16)
    rr_i = lax.broadcasted_iota(I32, (nc, nc), 0)
    cc_i = lax.broadcasted_iota(I32, (nc, nc), 1)
    lower_strict = jnp.where(cc_i < rr_i, 1.0, 0.0).astype(BF16)
    upper_rows = jnp.where(rr_i <= cc_i, 1.0, 0.0).astype(BF16)

    def prefix(x01):
        xb = x01.astype(BF16)
        p1 = jnp.dot(xb, upper_incl, preferred_element_type=F32)
        totb = jnp.dot(xb, ones_sq, preferred_element_type=F32)
        offs = jnp.dot(lower_strict, totb.astype(BF16), preferred_element_type=F32)
        return p1, offs

    tok = (lax.broadcasted_iota(I32, (nc, LANES), 0) * LANES
           + lax.broadcasted_iota(I32, (nc, LANES), 1)).astype(F32)

    def reduce2(fn, v):
        return fn(fn(v, axis=0, keepdims=True), axis=1, keepdims=True)

    def pass1(e, start):
        above_e = sel_scr[e]
        need = cap - jnp.sum(above_e).astype(I32)

        def pick(_, carry):
            vals, picked = carry
            first = reduce2(jnp.min, jnp.where(vals == reduce2(jnp.max, vals), tok, float(nc * LANES)))
            hit = tok == first
            return jnp.where(hit, -1.0, vals), picked + jnp.where(hit, 1.0, 0.0)

        _, picked = lax.fori_loop(0, need, pick, (pos_scr[e], jnp.zeros((nc, LANES), F32)))
        sel = above_e + picked
        sel_scr[e] = sel
        s1, soffs = prefix(sel)
        return start + (s1 + soffs - sel)

    start = lax.fori_loop(0, ne, pass1, jnp.zeros((nc, LANES), F32))
    start_ref[...] = start.astype(I32)

    p_col = lax.broadcasted_iota(I32, (cap, 1), 0).astype(F32)
    lane_row = lax.broadcasted_iota(I32, (1, LANES), 1).astype(F32)
    chunk_row = lax.broadcasted_iota(I32, (1, nc), 1).astype(F32)
    ones8 = jnp.ones((8, LANES), BF16)

    def row_sums(m):
        ones = jnp.ones((8, m.shape[1]), BF16)
        nt = lambda part: lax.dot_general(ones, part, (((1,), (1,)), ((), ())), preferred_element_type=F32)
        m1, m2, m3 = _split3(m)
        return (nt(m1) + nt(m2) + nt(m3))[0:1]

    def pass2(e, before):
        sel = sel_scr[e]
        selb = sel.astype(BF16)
        p1, offs = prefix(sel)
        tot_row = lax.dot_general(ones8, selb, (((1,), (1,)), ((), ())), preferred_element_type=F32)
        cum_row = jnp.dot(tot_row.astype(BF16), upper_rows, preferred_element_type=F32)
        cum1 = cum_row[0:1]
        prev1 = cum1 - tot_row[0:1]
        in_chunk = (prev1 <= p_col) & (p_col < cum1)
        ohc = jnp.where(in_chunk, 1.0, 0.0).astype(BF16)
        keyg = jnp.dot(ohc, (p1 * sel).astype(BF16), preferred_element_type=F32)
        offg = _dot_exact(ohc, offs)
        ohl = keyg == (p_col - offg + 1.0)
        l_p = row_sums(jnp.where(ohl, lane_row, 0.0))
        c_p = row_sums(jnp.where(in_chunk, chunk_row, 0.0))
        idx_ref[e] = (c_p * float(LANES) + l_p).astype(I32)
        qtab = _dot_exact(ohc, start) + jnp.dot(ohc, before.astype(BF16), preferred_element_type=F32)
        q_ref[e] = row_sums(jnp.where(ohl, qtab, 0.0)).astype(I32)
        afg = _dot_exact(ohc, aff_scr[e])
        gate_ref[e] = row_sums(jnp.where(ohl, afg, 0.0))
        return before + sel

    lax.fori_loop(0, ne, pass2, jnp.zeros((nc, LANES), F32))


def _route(lg3, cap):
    ne, nc, _ = lg3.shape
    full = lambda s: pl.BlockSpec(s, lambda: (0,) * len(s))
    return pl.pallas_call(
        functools.partial(_route_kernel, cap=cap),
        in_specs=[full(lg3.shape)],
        out_specs=[full((ne, 1, cap)), full((ne, 1, cap)), full((ne, 1, cap)), full((nc, LANES))],
        out_shape=[jax.ShapeDtypeStruct((ne, 1, cap), I32),
                   jax.ShapeDtypeStruct((ne, 1, cap), I32),
                   jax.ShapeDtypeStruct((ne, 1, cap), F32),
                   jax.ShapeDtypeStruct((nc, LANES), I32)],
        scratch_shapes=[pltpu.VMEM((ne, nc, LANES), F32),
                        pltpu.VMEM((ne, nc, LANES), F32),
                        pltpu.VMEM((ne, nc, LANES), F32)],
        compiler_params=pltpu.CompilerParams(vmem_limit_bytes=VMEM_LIMIT),
        name="route",
    )(lg3)


def _ffn_kernel(idx_ref, q_ref, gate_ref, wg_ref, wu_ref, wd_ref, h_hbm, y_hbm,
                xbuf, ybuf, gsem, ssem, *, tm, nt, total):
    step = pl.program_id(0) * nt + pl.program_id(1)
    slot = step % 2
    other = 1 - slot

    def gather_row(base, r, sl):
        return pltpu.make_async_copy(h_hbm.at[pl.ds(idx_ref[base + r], 1), :],
                                     xbuf.at[sl, pl.ds(r, 1), :], gsem.at[sl])

    def scatter_row(base, r, sl):
        return pltpu.make_async_copy(ybuf.at[sl, pl.ds(r, 1), :],
                                     y_hbm.at[pl.ds(q_ref[base + r], 1), :], ssem.at[sl])

    def wait_gather(sl):
        pltpu.make_async_copy(h_hbm.at[pl.ds(0, tm), :], xbuf.at[sl], gsem.at[sl]).wait()

    def wait_scatter(sl):
        pltpu.make_async_copy(ybuf.at[sl], y_hbm.at[pl.ds(0, tm), :], ssem.at[sl]).wait()

    @pl.when(step == 0)
    def _():
        ybuf[1] = jnp.zeros((tm, ybuf.shape[2]), F32)

        def body(r, carry):
            gather_row(0, r, 0).start()
            return carry

        lax.fori_loop(0, tm, body, 0)

    wait_gather(slot)

    @pl.when(step >= 1)
    def _():
        wait_scatter(slot)

    nbase = jnp.minimum(step + 1, total - 1) * tm
    pbase = jnp.maximum(step - 1, 0) * tm
    for r in range(tm):
        gather_row(nbase, r, other).start()
    for r in range(tm):
        scatter_row(pbase, r, other).start()

    x = xbuf[slot].astype(BF16)
    g = jnp.dot(x, wg_ref[0], preferred_element_type=F32)
    u = jnp.dot(x, wu_ref[0], preferred_element_type=F32)
    hid = (g * _sigmoid(g) * u).astype(BF16)
    ybuf[slot] = jnp.dot(hid, wd_ref[0], preferred_element_type=F32) * gate_ref[...]

    @pl.when(step == total - 1)
    def _():
        wait_gather(other)
        wait_scatter(other)
        base = step * tm

        def body(r, carry):
            scatter_row(base, r, slot).start()
            return carry

        lax.fori_loop(0, tm, body, 0)
        wait_scatter(slot)


def _ffn(h2, idx, qpos, gate, wg, wu, wd):
    n, d = h2.shape
    ne = wg.shape[0]
    cap = idx.shape[0] // ne
    tm = min(512, cap)
    nt = cap // tm
    total = ne * nt
    grid_spec = pltpu.PrefetchScalarGridSpec(
        num_scalar_prefetch=2,
        grid=(ne, nt),
        in_specs=[pl.BlockSpec((tm, 1), lambda e, i, *_: (e * nt + i, 0)),
                  pl.BlockSpec((1, d, EXPERT_FF), lambda e, i, *_: (e, 0, 0)),
                  pl.BlockSpec((1, d, EXPERT_FF), lambda e, i, *_: (e, 0, 0)),
                  pl.BlockSpec((1, EXPERT_FF, d), lambda e, i, *_: (e, 0, 0)),
                  pl.BlockSpec(memory_space=pl.ANY)],
        out_specs=pl.BlockSpec(memory_space=pl.ANY),
        scratch_shapes=[pltpu.VMEM((2, tm, d), F32),
                        pltpu.VMEM((2, tm, d), F32),
                        pltpu.SemaphoreType.DMA((2,)),
                        pltpu.SemaphoreType.DMA((2,))],
    )
    return pl.pallas_call(
        functools.partial(_ffn_kernel, tm=tm, nt=nt, total=total),
        grid_spec=grid_spec,
        out_shape=jax.ShapeDtypeStruct((ne * cap, d), F32),
        compiler_params=_cparams(("arbitrary", "arbitrary")),
        name="expert_ffn",
    )(idx, qpos, gate, wg, wu, wd, h2)


def _combine_kernel(cs_ref, x1_ref, mod_ref, st_ref, en_ref, g_ref, b_ref, y_hbm, o_ref,
                    ybuf, acc, sem, *, win, total, nsteps):
    i = pl.program_id(0)
    slot = i % 2
    lo0 = cs_ref[i]
    hi = cs_ref[i + 1]
    adv = win - 8

    def win_start(lo):
        return pl.multiple_of((jnp.minimum(lo, total - win) // 8) * 8, 8)

    def window(lo, sl):
        return pltpu.make_async_copy(y_hbm.at[pl.ds(win_start(lo), win), :], ybuf.at[sl], sem.at[sl])

    @pl.when(i == 0)
    def _():
        window(lo0, 0).start()

    @pl.when(i + 1 < nsteps)
    def _():
        window(hi, 1 - slot).start()

    ntrip = jnp.maximum((hi - lo0 + adv - 1) // adv, 1)
    acc[...] = jnp.zeros_like(acc)
    st = st_ref[...]
    en = en_ref[...]

    def body(k, carry):
        lo = lo0 + k * adv

        @pl.when(k > 0)
        def _():
            window(lo, slot).start()

        window(lo, slot).wait()
        qj = win_start(lo) + lax.broadcasted_iota(I32, (1, win), 1)
        seg = (qj >= st) & (qj < en) & (qj >= lo) & (qj < lo + adv)
        acc[...] += jnp.dot(jnp.where(seg, 1.0, 0.0).astype(BF16), ybuf[slot].astype(BF16),
                            preferred_element_type=F32)
        return carry

    lax.fori_loop(0, ntrip, body, 0)
    m = mod_ref[0]
    o_ref[...] = _layer_norm(ALPHA * x1_ref[...] + m[5:6] * acc[...], g_ref[...], b_ref[...])


def _combine(x1, mods, mod_row, start, pairs, g, b):
    n, d = x1.shape
    total = pairs.shape[0]
    tc = 256
    win = 640
    st = start.reshape(n, 1)
    en = jnp.concatenate([start[1:], jnp.full((1,), total, I32)]).reshape(n, 1)
    cs = jnp.concatenate([start[::tc], jnp.full((1,), total, I32)])
    grid_spec = pltpu.PrefetchScalarGridSpec(
        num_scalar_prefetch=1,
        grid=(n // tc,),
        in_specs=[pl.BlockSpec((tc, d), lambda i, *_: (i, 0)),
                  pl.BlockSpec((1, N_MOD, d), lambda i, *_: (mod_row(i * tc), 0, 0)),
                  pl.BlockSpec((tc, 1), lambda i, *_: (i, 0)),
                  pl.BlockSpec((tc, 1), lambda i, *_: (i, 0)),
                  pl.BlockSpec((1, d), lambda i, *_: (0, 0)),
                  pl.BlockSpec((1, d), lambda i, *_: (0, 0)),
                  pl.BlockSpec(memory_space=pl.ANY)],
        out_specs=pl.BlockSpec((tc, d), lambda i, *_: (i, 0)),
        scratch_shapes=[pltpu.VMEM((2, win, d), F32),
                        pltpu.VMEM((tc, d), F32),
                        pltpu.SemaphoreType.DMA((2,))],
    )
    return pl.pallas_call(
        functools.partial(_combine_kernel, win=win, total=total, nsteps=n // tc),
        grid_spec=grid_spec,
        out_shape=jax.ShapeDtypeStruct((n, d), F32),
        compiler_params=_cparams(("arbitrary",)),
        name="combine_ln2",
    )(cs, x1, mods, st, en, g, b, pairs)


def _swap_pairs(w):
    f = ROPE_FREQS
    return jnp.concatenate([w[..., f:2 * f], w[..., 0:f], w[..., 3 * f:4 * f], w[..., 2 * f:3 * f]], axis=-1)


def _rope_tables(seq):
    n_rows = seq // GRID_W
    row = np.repeat(np.arange(n_rows, dtype=np.float64), GRID_W)
    col = np.tile(np.arange(GRID_W, dtype=np.float64), n_rows)
    inv = ROPE_THETA ** (-np.arange(ROPE_FREQS, dtype=np.float64) / ROPE_FREQS)
    ar = row[:, None] * inv
    ac = col[:, None] * inv
    z = np.zeros((seq, LANES - QK_ROPE))
    cos128 = np.concatenate([np.cos(ar), np.cos(ar), np.cos(ac), np.cos(ac), z], axis=1)
    sin128 = np.concatenate([-np.sin(ar), np.sin(ar), -np.sin(ac), np.sin(ac), z], axis=1)
    return jnp.asarray(cos128, F32), jnp.asarray(sin128, F32)


def _dft_tables(seq):
    m = 2 * seq
    blk = 64
    s = np.arange(seq, dtype=np.int64)
    ang1 = ((np.arange(0, seq, blk, dtype=np.int64)[:, None] * s[None, :]) % m) * (2.0 * math.pi / m)
    ang0 = ((np.arange(blk, dtype=np.int64)[:, None] * s[None, :]) % m) * (2.0 * math.pi / m)
    c1, s1 = jnp.asarray(np.cos(ang1), F32)[:, None, :], jnp.asarray(np.sin(ang1), F32)[:, None, :]
    c0, s0 = jnp.asarray(np.cos(ang0), F32)[None, :, :], jnp.asarray(np.sin(ang0), F32)[None, :, :]
    cmat = (c1 * c0 - s1 * s0).reshape(seq, seq).astype(BF16)
    smat = (s1 * c0 + c1 * s0).reshape(seq, seq).astype(BF16)
    return cmat, smat


def _prep_weights(p):
    w = {}
    w_in = p['w_in']
    s0, s1, s2, s3, s4 = (Q_LORA, Q_LORA + KV_LORA, Q_LORA + KV_LORA + QK_ROPE,
                          Q_LORA + KV_LORA + QK_ROPE + 3 * HY_WIDTH,
                          Q_LORA + KV_LORA + QK_ROPE + 3 * HY_WIDTH + D_MODEL)
    kr = w_in[:, s1:s2]
    z64 = jnp.zeros((D_MODEL, LANES - QK_ROPE), F32)
    w['w_in'] = jnp.concatenate([w_in[:, :s1], kr, z64, _swap_pairs(kr), z64,
                                 w_in[:, s3:], w_in[:, s2:s3]], axis=1).astype(BF16)
    scale = (QK_NOPE + QK_ROPE) ** -0.5
    wq = (p['w_uq'] * scale).reshape(Q_LORA, MLA_HEADS, QK_NOPE + QK_ROPE)
    zq = jnp.zeros((Q_LORA, MLA_HEADS, HEAD_PAD - QK_NOPE - QK_ROPE), F32)
    w['wa'] = jnp.concatenate([wq, zq], axis=-1).reshape(Q_LORA, MLA_HEADS * HEAD_PAD).astype(BF16)
    w['wb'] = jnp.concatenate([_swap_pairs(wq[..., QK_NOPE:]), zq], axis=-1).reshape(
        Q_LORA, MLA_HEADS * LANES).astype(BF16)
    wkv = p['w_ukv'].reshape(KV_LORA, MLA_HEADS, QK_NOPE + V_HEAD)
    w['wk'] = wkv[..., :QK_NOPE].reshape(KV_LORA, MLA_HEADS * QK_NOPE).astype(BF16)
    w['wv'] = wkv[..., QK_NOPE:].reshape(KV_LORA, MLA_HEADS * V_HEAD).astype(BF16)
    w['woa'] = p['w_o_mla'].astype(BF16)
    w['woh'] = p['w_o_hy'].astype(BF16)
    w['wout'] = p['w_out'].astype(BF16)
    wr = jnp.pad(p['w_router'], ((0, 0), (0, LANES - N_EXPERTS)))
    w['wr_hi'] = wr.astype(BF16)
    w['wr_lo'] = (wr - w['wr_hi'].astype(F32)).astype(BF16)
    w['wg'] = p['w_gate'].astype(BF16)
    w['wu'] = p['w_up'].astype(BF16)
    w['wd'] = p['w_down'].astype(BF16)
    w['hy_w'] = p['hy_short_w'].reshape(3, 3, HY_WIDTH).transpose(1, 0, 2)
    w['hy_b'] = p['hy_short_b'].reshape(3, 1, HY_WIDTH)
    return w


def _trunk(x3, mods, mod_row, tm_in, p, w, cache_kv, cache_kr, rope):
    b, seq, d = x3.shape
    n = b * seq
    x = x3.reshape(n, d)
    small, big = _inproj(x, mods, mod_row, w['w_in'], tm_in)
    cos128, sin128 = _rope_tables(seq) if rope else (None, None)
    q, ckv, kr = _qprep(small, p['q_norm_g'].reshape(1, -1), p['kv_norm_g'].reshape(1, -1),
                        w['wa'], w['wb'], cos128, sin128, seq, rope)
    ckv3 = ckv.reshape(b, seq, KV_LORA)
    kr3 = kr.reshape(b, seq, LANES)
    if cache_kv is None:
        kv_all, kr_all = ckv3, kr3
    else:
        kv_all = jnp.concatenate([cache_kv, ckv3], axis=1)
        kr_all = jnp.concatenate([jnp.pad(cache_kr, ((0, 0), (0, 0), (0, LANES - QK_ROPE))), kr3], axis=1)
    kh, vh = _kvup(kv_all, kr_all, w['wk'], w['wv'])
    attn = _attention(q.reshape(b, seq, MLA_HEADS * HEAD_PAD), kh, vh).reshape(n, MLA_HEADS * V_HEAD)

    x0c, z, zn = _hypre(big.reshape(b, seq, BIG_W), w['hy_w'], w['hy_b'])
    a_td, d_td, kn = _filter_td(seq, p)
    cmat, smat = _dft_tables(seq)
    kc, ks = _kspec(cmat, smat, a_td, d_td)
    pc, ps = _hy_fwd(cmat, smat, z, kc, ks)
    hy = _hy_inv(cmat, smat, pc, ps, z, x0c, zn, kn, p['hy_skip'].reshape(1, -1)).reshape(n, HY_WIDTH)

    merged = _merge(attn, hy, big, w['woa'], w['woh'])
    x1, h2, lg = _ln1(x, merged, mods, mod_row, w['wout'], p['ln1_g'].reshape(1, -1),
                      p['ln1_b'].reshape(1, -1), w['wr_hi'], w['wr_lo'])
    cap = EC_CAPACITY * n // N_EXPERTS
    lg3 = lg[:, :N_EXPERTS].T.reshape(N_EXPERTS, n // LANES, LANES)
    idx, qpos, gate, start = _route(lg3, cap)
    pairs = _ffn(h2, idx.reshape(-1), qpos.reshape(-1), gate.reshape(-1, 1), w['wg'], w['wu'], w['wd'])
    y = _combine(x1, mods, mod_row, start.reshape(-1), pairs,
                 p['ln2_g'].reshape(1, -1), p['ln2_b'].reshape(1, -1))
    return y.reshape(b, seq, d), ckv3, kr3[..., :QK_ROPE]


def kernel(x_prompt, x_sample, cache_kv_c, cache_k_rope, c, c_ctx, w_ada, b_ada, w_in, q_norm_g, kv_norm_g, w_uq, w_ukv, w_o_mla, hy_short_w, hy_short_b, hy_filt_w1, hy_filt_b1, hy_filt_freq1, hy_filt_w2, hy_filt_b2, hy_filt_freq2, hy_filt_w3, hy_filt_b3, hy_skip, w_o_hy, w_out, ln1_g, ln1_b, ln2_g, ln2_b, w_router, w_gate, w_up, w_down):
    params = dict(w_in=w_in, q_norm_g=q_norm_g, kv_norm_g=kv_norm_g, w_uq=w_uq, w_ukv=w_ukv, w_o_mla=w_o_mla,
                  hy_short_w=hy_short_w, hy_short_b=hy_short_b, hy_filt_w1=hy_filt_w1, hy_filt_b1=hy_filt_b1,
                  hy_filt_freq1=hy_filt_freq1, hy_filt_w2=hy_filt_w2, hy_filt_b2=hy_filt_b2,
                  hy_filt_freq2=hy_filt_freq2, hy_filt_w3=hy_filt_w3, hy_filt_b3=hy_filt_b3, hy_skip=hy_skip,
                  w_o_hy=w_o_hy, w_out=w_out, ln1_g=ln1_g, ln1_b=ln1_b, ln2_g=ln2_g, ln2_b=ln2_b,
                  w_router=w_router, w_gate=w_gate, w_up=w_up, w_down=w_down)
    depth = w_in.shape[0]
    dec_b, dec_seq, d = x_sample.shape
    n_rows = 16
    cond = jnp.zeros((n_rows, d), F32).at[:dec_b].set(c).at[dec_b].set(c_ctx)
    y_prompt, y_sample = x_prompt, x_sample
    kv_list, kr_list = [], []
    for l in range(depth):
        p = {k: v[l] for k, v in params.items()}
        w = _prep_weights(p)
        mods = _ada_mod(cond, w_ada[l], b_ada[l]).reshape(n_rows, N_MOD, d)
        n_prompt = x_prompt.shape[0] * x_prompt.shape[1]
        y_prompt, c_kv, k_rope = _trunk(y_prompt, mods, lambda r: dec_b, min(1024, n_prompt), p, w,
                                        None, None, False)
        kv_list.append(c_kv)
        kr_list.append(k_rope)
        y_sample, _, _ = _trunk(y_sample, mods, lambda r: r // dec_seq, min(1024, dec_seq), p, w,
                                cache_kv_c[:, l], cache_k_rope[:, l], True)
    return (y_prompt, y_sample, jnp.stack(kv_list, axis=1), jnp.stack(kr_list, axis=1))
```

```python
import functools
import math

import numpy as np
import jax
import jax.numpy as jnp
from jax import lax
from jax.experimental import pallas as pl
from jax.experimental.pallas import tpu as pltpu

F32 = jnp.float32
BF16 = jnp.bfloat16
I32 = jnp.int32
HIGHEST = lax.Precision.HIGHEST

D_MODEL = 2048
GRID_W = 64
MLA_HEADS = 8
QK_NOPE = 128
QK_ROPE = 64
V_HEAD = 128
Q_LORA = 512
KV_LORA = 256
ROPE_THETA = 10000.0
ROPE_FREQS = QK_ROPE // 4
HY_WIDTH = 1024
HY_EMB = 33
HY_BANDS = (HY_EMB - 1) // 2
HY_HIDDEN = 64
HY_FAST_DECAY = 0.3
HY_SLOW_DECAY = 1.5
HY_TARGET = 0.01
N_EXPERTS = 16
EXPERT_FF = 1024
EC_CAPACITY = 2
EPS = 1e-6
DEPTH = 1
ALPHA = (2 * DEPTH) ** 0.25
N_MOD = 6

LANES = 128
HEAD_PAD = 256
SMALL_W = 1024
BIG_W = 2 * D_MODEL + 3 * HY_WIDTH
VMEM_LIMIT = 56 * 1024 * 1024


def _cparams(sem, vmem=VMEM_LIMIT):
    return pltpu.CompilerParams(dimension_semantics=sem, vmem_limit_bytes=vmem)


def _sigmoid(x):
    return 1.0 / (1.0 + jnp.exp(-x))


def _const_spec(shape):
    nd = len(shape)
    return pl.BlockSpec(shape, lambda *_: (0,) * nd, pipeline_mode=pl.Buffered(1))


def _ada_kernel(c_ref, w_ref, b_ref, o_ref):
    c = c_ref[...]
    s = (c * _sigmoid(c)).astype(BF16)
    o_ref[...] = jnp.dot(s, w_ref[...].astype(BF16), preferred_element_type=F32) + b_ref[...]


def _ada_mod(cond, w_ada, b_ada):
    r, d = cond.shape
    n = w_ada.shape[1]
    tn = 512
    return pl.pallas_call(
        _ada_kernel,
        grid=(n // tn,),
        in_specs=[pl.BlockSpec((r, d), lambda j: (0, 0)),
                  pl.BlockSpec((d, tn), lambda j: (0, j)),
                  pl.BlockSpec((1, tn), lambda j: (0, j))],
        out_specs=pl.BlockSpec((r, tn), lambda j: (0, j)),
        out_shape=jax.ShapeDtypeStruct((r, n), F32),
        compiler_params=_cparams(("arbitrary",)),
        name="ada_mod",
    )(cond, w_ada, b_ada.reshape(1, n))


def _inproj_kernel(x_ref, mod_ref, w_ref, small_ref, big_ref, h_scr, *, n_small):
    j = pl.program_id(1)

    @pl.when(j == 0)
    def _():
        m = mod_ref[0]
        h_scr[...] = (x_ref[...] * (1.0 + m[1:2]) + m[0:1]).astype(BF16)

    acc = jnp.dot(h_scr[...], w_ref[...], preferred_element_type=F32)

    @pl.when(j < n_small)
    def _():
        small_ref[...] = acc

    @pl.when(j >= n_small)
    def _():
        big_ref[...] = acc.astype(BF16)


def _inproj(x, mods, mod_row, w_r, tm):
    n, d = x.shape
    tn = 512
    n_small = SMALL_W // tn
    n_cols = w_r.shape[1] // tn
    return pl.pallas_call(
        functools.partial(_inproj_kernel, n_small=n_small),
        grid=(n // tm, n_cols),
        in_specs=[pl.BlockSpec((tm, d), lambda i, j: (i, 0)),
                  pl.BlockSpec((1, N_MOD, d), lambda i, j: (mod_row(i * tm), 0, 0)),
                  pl.BlockSpec((d, tn), lambda i, j: (0, j))],
        out_specs=[pl.BlockSpec((tm, tn), lambda i, j: (i, jnp.minimum(j, n_small - 1))),
                   pl.BlockSpec((tm, tn), lambda i, j: (i, jnp.maximum(j - n_small, 0)))],
        out_shape=[jax.ShapeDtypeStruct((n, SMALL_W), F32),
                   jax.ShapeDtypeStruct((n, BIG_W), BF16)],
        scratch_shapes=[pltpu.VMEM((tm, d), BF16)],
        compiler_params=_cparams(("arbitrary", "arbitrary")),
        name="inproj",
    )(x, mods, w_r)


def _qprep_kernel(*refs, rope):
    if rope:
        small_ref, qg_ref, kvg_ref, wa_ref, wb_ref, c_ref, s_ref, q_ref, ckv_ref, kr_ref = refs
    else:
        small_ref, qg_ref, kvg_ref, wa_ref, q_ref, ckv_ref, kr_ref = refs
    qc = small_ref[:, 0:Q_LORA]
    kvc = small_ref[:, Q_LORA:Q_LORA + KV_LORA]
    kr = small_ref[:, Q_LORA + KV_LORA:Q_LORA + KV_LORA + LANES]
    qn = (qc * lax.rsqrt(jnp.mean(qc * qc, axis=-1, keepdims=True) + EPS) * qg_ref[...]).astype(BF16)
    ckv_ref[...] = kvc * lax.rsqrt(jnp.mean(kvc * kvc, axis=-1, keepdims=True) + EPS) * kvg_ref[...]
    qa = jnp.dot(qn, wa_ref[...], preferred_element_type=F32)
    if rope:
        krs = small_ref[:, Q_LORA + KV_LORA + LANES:SMALL_W]
        qb = jnp.dot(qn, wb_ref[...], preferred_element_type=F32)
        cs = c_ref[...]
        sn = s_ref[...]
        kr_ref[...] = kr * cs + krs * sn
        for h in range(MLA_HEADS):
            lo = h * HEAD_PAD
            q_ref[:, lo:lo + LANES] = qa[:, lo:lo + LANES].astype(BF16)
            q_ref[:, lo + LANES:lo + HEAD_PAD] = (
                qa[:, lo + LANES:lo + HEAD_PAD] * cs + qb[:, h * LANES:(h + 1) * LANES] * sn).astype(BF16)
    else:
        kr_ref[...] = kr
        q_ref[...] = qa.astype(BF16)


def _qprep(small, qg, kvg, wa, wb, cos128, sin128, seq, rope):
    n = small.shape[0]
    tm = min(512, seq)
    nblk = seq // tm
    in_specs = [pl.BlockSpec((tm, SMALL_W), lambda i: (i, 0)),
                pl.BlockSpec((1, Q_LORA), lambda i: (0, 0)),
                pl.BlockSpec((1, KV_LORA), lambda i: (0, 0)),
                pl.BlockSpec(wa.shape, lambda i: (0, 0))]
    args = [small, qg, kvg, wa]
    if rope:
        in_specs += [pl.BlockSpec(wb.shape, lambda i: (0, 0)),
                     pl.BlockSpec((tm, LANES), lambda i: (i % nblk, 0)),
                     pl.BlockSpec((tm, LANES), lambda i: (i % nblk, 0))]
        args += [wb, cos128, sin128]
    return pl.pallas_call(
        functools.partial(_qprep_kernel, rope=rope),
        grid=(n // tm,),
        in_specs=in_specs,
        out_specs=[pl.BlockSpec((tm, MLA_HEADS * HEAD_PAD), lambda i: (i, 0)),
                   pl.BlockSpec((tm, KV_LORA), lambda i: (i, 0)),
                   pl.BlockSpec((tm, LANES), lambda i: (i, 0))],
        out_shape=[jax.ShapeDtypeStruct((n, MLA_HEADS * HEAD_PAD), BF16),
                   jax.ShapeDtypeStruct((n, KV_LORA), F32),
                   jax.ShapeDtypeStruct((n, LANES), F32)],
        compiler_params=_cparams(("arbitrary",)),
        name="qprep",
    )(*args)


def _kvup_kernel(kv_ref, kr_ref, wk_ref, wv_ref, k_ref, v_ref):
    kv = kv_ref[0].astype(BF16)
    kn = jnp.dot(kv, wk_ref[...], preferred_element_type=F32)
    vv = jnp.dot(kv, wv_ref[...], preferred_element_type=F32)
    krp = kr_ref[0].astype(BF16)
    lane = lax.broadcasted_iota(I32, (kv.shape[0], LANES), 1)
    ones_col = jnp.where(lane == 0, 1.0, 0.0).astype(BF16)
    for h in range(MLA_HEADS):
        k_ref[0, h, :, 0:LANES] = kn[:, h * QK_NOPE:(h + 1) * QK_NOPE].astype(BF16)
        k_ref[0, h, :, LANES:HEAD_PAD] = krp
        v_ref[0, h, :, 0:V_HEAD] = vv[:, h * V_HEAD:(h + 1) * V_HEAD].astype(BF16)
        v_ref[0, h, :, V_HEAD:V_HEAD + LANES] = ones_col


def _kvup(kv_all, kr_all, wk, wv):
    b, lk, _ = kv_all.shape
    tm = 512 if lk % 512 == 0 else 256
    return pl.pallas_call(
        _kvup_kernel,
        grid=(b, lk // tm),
        in_specs=[pl.BlockSpec((1, tm, KV_LORA), lambda bi, i: (bi, i, 0)),
                  pl.BlockSpec((1, tm, LANES), lambda bi, i: (bi, i, 0)),
                  pl.BlockSpec(wk.shape, lambda bi, i: (0, 0)),
                  pl.BlockSpec(wv.shape, lambda bi, i: (0, 0))],
        out_specs=[pl.BlockSpec((1, MLA_HEADS, tm, HEAD_PAD), lambda bi, i: (bi, 0, i, 0)),
                   pl.BlockSpec((1, MLA_HEADS, tm, V_HEAD + LANES), lambda bi, i: (bi, 0, i, 0))],
        out_shape=[jax.ShapeDtypeStruct((b, MLA_HEADS, lk, HEAD_PAD), BF16),
                   jax.ShapeDtypeStruct((b, MLA_HEADS, lk, V_HEAD + LANES), BF16)],
        compiler_params=_cparams(("arbitrary", "arbitrary")),
        name="kvup",
    )(kv_all, kr_all, wk, wv)


def _attn_kernel(q_ref, k_ref, v_ref, o_ref, *, n_sub):
    k = k_ref[0, 0]
    v = v_ref[0, 0]
    rows = q_ref.shape[1] // n_sub
    for t in range(n_sub):
        r0 = t * rows
        s = lax.dot_general(q_ref[0, r0:r0 + rows, :], k, (((1,), (1,)), ((), ())), preferred_element_type=F32)
        p = jnp.exp2(s - jnp.max(s, axis=-1, keepdims=True)).astype(BF16)
        o = jnp.dot(p, v, preferred_element_type=F32)
        o_ref[0, r0:r0 + rows, :] = (o[:, 0:V_HEAD] / o[:, V_HEAD:V_HEAD + 1]).astype(BF16)


def _attention(q, k, v):
    b, l, _ = q.shape
    lk = k.shape[2]
    tq = 512 if l % 512 == 0 else 256
    return pl.pallas_call(
        functools.partial(_attn_kernel, n_sub=tq // 256),
        grid=(b, MLA_HEADS, l // tq),
        in_specs=[pl.BlockSpec((1, tq, HEAD_PAD), lambda bi, h, i: (bi, i, h)),
                  pl.BlockSpec((1, 1, lk, HEAD_PAD), lambda bi, h, i: (bi, h, 0, 0)),
                  pl.BlockSpec((1, 1, lk, V_HEAD + LANES), lambda bi, h, i: (bi, h, 0, 0))],
        out_specs=pl.BlockSpec((1, tq, V_HEAD), lambda bi, h, i: (bi, i, h)),
        out_shape=jax.ShapeDtypeStruct((b, l, MLA_HEADS * V_HEAD), BF16),
        compiler_params=_cparams(("arbitrary", "arbitrary", "arbitrary")),
        name="attention",
    )(q, k, v)


def _hypre_kernel(x0_ref, x1_ref, v_ref, w_ref, b_ref, x0o_ref, z_ref, zn_ref):
    seq = x0_ref.shape[1]
    row = lax.broadcasted_iota(I32, (seq, 1), 0)

    def conv(u_ref, g):
        u = u_ref[0].astype(F32)
        up = jnp.where(row == 0, 0.0, pltpu.roll(u, 1, axis=0))
        un = jnp.where(row == seq - 1, 0.0, pltpu.roll(u, seq - 1, axis=0))
        w = w_ref[g]
        return up * w[0:1] + u * w[1:2] + un * w[2:3] + b_ref[g]

    x0o_ref[0] = conv(x0_ref, 0).astype(BF16)
    z = conv(x1_ref, 1) * conv(v_ref, 2)
    z_ref[0] = z.astype(BF16)
    sign = (1 - 2 * (row & 1)).astype(F32)
    zn_ref[0] = jnp.sum(z * sign, axis=0, keepdims=True)


def _hypre(big3, w3, b3):
    b, l, _ = big3.shape
    cb = 256
    ncb = HY_WIDTH // cb
    off = 2 * D_MODEL // cb
    return pl.pallas_call(
        _hypre_kernel,
        grid=(b, ncb),
        in_specs=[pl.BlockSpec((1, l, cb), lambda bi, j: (bi, 0, off + j)),
                  pl.BlockSpec((1, l, cb), lambda bi, j: (bi, 0, off + j + ncb)),
                  pl.BlockSpec((1, l, cb), lambda bi, j: (bi, 0, off + j + 2 * ncb)),
                  pl.BlockSpec((3, 3, cb), lambda bi, j: (0, 0, j)),
                  pl.BlockSpec((3, 1, cb), lambda bi, j: (0, 0, j))],
        out_specs=[pl.BlockSpec((1, l, cb), lambda bi, j: (bi, 0, j)),
                   pl.BlockSpec((1, l, cb), lambda bi, j: (bi, 0, j)),
                   pl.BlockSpec((1, 1, cb), lambda bi, j: (bi, 0, j))],
        out_shape=[jax.ShapeDtypeStruct((b, l, HY_WIDTH), BF16),
                   jax.ShapeDtypeStruct((b, l, HY_WIDTH), BF16),
                   jax.ShapeDtypeStruct((b, 1, HY_WIDTH), F32)],
        compiler_params=_cparams(("arbitrary", "arbitrary")),
        name="hyena_pre",
    )(big3, big3, big3, w3, b3)


def _filter_kernel(feat_ref, w1_ref, b1_ref, f1_ref, w2_ref, b2_ref, f2_ref,
                   w3f_ref, w3b_ref, b3f_ref, b3b_ref, dl_ref, a_ref, d_ref, kn_ref, hd_scr):
    seq = feat_ref.shape[0]

    @pl.when(pl.program_id(0) == 0)
    def _():
        h1 = jnp.sin(f1_ref[...] * (jnp.dot(feat_ref[...], w1_ref[...], precision=HIGHEST,
                                            preferred_element_type=F32) + b1_ref[...]))
        hd_scr[...] = jnp.sin(f2_ref[...] * (jnp.dot(h1, w2_ref[...], precision=HIGHEST,
                                                     preferred_element_type=F32) + b2_ref[...]))

    hd = hd_scr[...]
    row = lax.broadcasted_iota(I32, (seq, 1), 0)
    t = row.astype(F32) / float(max(seq - 1, 1))
    window = jnp.exp(-t * dl_ref[...])
    hf = (jnp.dot(hd, w3f_ref[...], precision=HIGHEST, preferred_element_type=F32) + b3f_ref[...]) * window
    hb = (jnp.dot(hd, w3b_ref[...], precision=HIGHEST, preferred_element_type=F32) + b3b_ref[...]) * window
    hb = jnp.where(row == 0, 0.0, hb)
    den = jnp.sum(jnp.abs(hf) + jnp.abs(hb), axis=0, keepdims=True) + EPS
    a = (hf + hb) / den
    a_ref[...] = a.astype(BF16)
    d_ref[...] = ((hf - hb) / den).astype(BF16)
    sign = (1 - 2 * (row & 1)).astype(F32)
    kn_ref[...] = jnp.sum(a * sign, axis=0, keepdims=True)


def _filter_td(seq, p):
    pos = np.arange(seq, dtype=np.float64)
    t = pos / max(seq - 1, 1)
    bands = np.linspace(1e-4, HY_BANDS - 1, HY_BANDS)
    w = 2.0 * math.pi * pos / seq
    feats = np.concatenate([t[:, None], np.cos(w[:, None] * bands), -np.sin(w[:, None] * bands)], axis=-1)
    feats = jnp.asarray(np.pad(feats, ((0, 0), (0, LANES - HY_EMB))), F32)
    w1 = jnp.pad(p['hy_filt_w1'], ((0, LANES - HY_EMB), (0, 0)))
    deltas = jnp.asarray(np.abs(np.linspace(math.log(HY_TARGET) / HY_SLOW_DECAY, math.log(HY_TARGET) / HY_FAST_DECAY,
                                            HY_WIDTH)).reshape(1, HY_WIDTH), F32)
    cb = 256
    ncb = HY_WIDTH // cb
    row = lambda v: v.reshape(1, -1)
    c2 = lambda j: (0, 0)
    return pl.pallas_call(
        _filter_kernel,
        grid=(ncb,),
        in_specs=[pl.BlockSpec((seq, LANES), c2),
                  pl.BlockSpec((LANES, HY_HIDDEN), c2), pl.BlockSpec((1, HY_HIDDEN), c2),
                  pl.BlockSpec((1, HY_HIDDEN), c2),
                  pl.BlockSpec((HY_HIDDEN, HY_HIDDEN), c2), pl.BlockSpec((1, HY_HIDDEN), c2),
                  pl.BlockSpec((1, HY_HIDDEN), c2),
                  pl.BlockSpec((HY_HIDDEN, cb), lambda j: (0, j)),
                  pl.BlockSpec((HY_HIDDEN, cb), lambda j: (0, j + ncb)),
                  pl.BlockSpec((1, cb), lambda j: (0, j)),
                  pl.BlockSpec((1, cb), lambda j: (0, j + ncb)),
                  pl.BlockSpec((1, cb), lambda j: (0, j))],
        out_specs=[pl.BlockSpec((seq, cb), lambda j: (0, j)),
                   pl.BlockSpec((seq, cb), lambda j: (0, j)),
                   pl.BlockSpec((1, cb), lambda j: (0, j))],
        out_shape=[jax.ShapeDtypeStruct((seq, HY_WIDTH), BF16),
                   jax.ShapeDtypeStruct((seq, HY_WIDTH), BF16),
                   jax.ShapeDtypeStruct((1, HY_WIDTH), F32)],
        scratch_shapes=[pltpu.VMEM((seq, HY_HIDDEN), F32)],
        compiler_params=_cparams(("arbitrary",)),
        name="hyena_filter",
    )(feats, w1, row(p['hy_filt_b1']), row(p['hy_filt_freq1']),
      p['hy_filt_w2'], row(p['hy_filt_b2']), row(p['hy_filt_freq2']),
      p['hy_filt_w3'], p['hy_filt_w3'], row(p['hy_filt_b3']), row(p['hy_filt_b3']), deltas)


def _kspec_kernel(c_ref, s_ref, a_ref, d_ref, kc_ref, ks_ref, *, seq):
    tf = c_ref.shape[0]
    f = pl.program_id(0) * tf + lax.broadcasted_iota(I32, (tf, 1), 0)
    wgt = jnp.where(f == 0, 1.0, 2.0) / float(2 * seq)
    kc_ref[...] = jnp.dot(c_ref[...], a_ref[...], preferred_element_type=F32) * wgt
    ks_ref[...] = jnp.dot(s_ref[...], d_ref[...], preferred_element_type=F32) * wgt


def _kspec(cmat, smat, a, d):
    seq = cmat.shape[0]
    tf = min(512, seq)
    cb = 512
    return pl.pallas_call(
        functools.partial(_kspec_kernel, seq=seq),
        grid=(seq // tf, HY_WIDTH // cb),
        in_specs=[pl.BlockSpec((tf, seq), lambda i, j: (i, 0)),
                  pl.BlockSpec((tf, seq), lambda i, j: (i, 0)),
                  pl.BlockSpec((seq, cb), lambda i, j: (0, j)),
                  pl.BlockSpec((seq, cb), lambda i, j: (0, j))],
        out_specs=[pl.BlockSpec((tf, cb), lambda i, j: (i, j)),
                   pl.BlockSpec((tf, cb), lambda i, j: (i, j))],
        out_shape=[jax.ShapeDtypeStruct((seq, HY_WIDTH), F32),
                   jax.ShapeDtypeStruct((seq, HY_WIDTH), F32)],
        compiler_params=_cparams(("arbitrary", "arbitrary")),
        name="hyena_kspec",
    )(cmat, smat, a, d)


def _fwd_kernel(c_ref, s_ref, z_ref, kc_ref, ks_ref, pc_ref, ps_ref):
    z = z_ref[0]
    zc = jnp.dot(c_ref[...], z, preferred_element_type=F32)
    zs = jnp.dot(s_ref[...], z, preferred_element_type=F32)
    kc = kc_ref[...]
    ks = ks_ref[...]
    pc_ref[0] = (zc * kc - zs * ks).astype(BF16)
    ps_ref[0] = (zc * ks + zs * kc).astype(BF16)


def _hy_fwd(cmat, smat, z, kc, ks):
    b, seq, _ = z.shape
    tf = min(1024, seq)
    cb = 512
    return pl.pallas_call(
        _fwd_kernel,
        grid=(HY_WIDTH // cb, seq // tf, b),
        in_specs=[pl.BlockSpec((tf, seq), lambda j, i, bi: (i, 0)),
                  pl.BlockSpec((tf, seq), lambda j, i, bi: (i, 0)),
                  pl.BlockSpec((1, seq, cb), lambda j, i, bi: (bi, 0, j)),
                  pl.BlockSpec((tf, cb), lambda j, i, bi: (i, j)),
                  pl.BlockSpec((tf, cb), lambda j, i, bi: (i, j))],
        out_specs=[pl.BlockSpec((1, tf, cb), lambda j, i, bi: (bi, i, j)),
                   pl.BlockSpec((1, tf, cb), lambda j, i, bi: (bi, i, j))],
        out_shape=[jax.ShapeDtypeStruct((b, seq, HY_WIDTH), BF16),
                   jax.ShapeDtypeStruct((b, seq, HY_WIDTH), BF16)],
        compiler_params=_cparams(("arbitrary", "arbitrary", "arbitrary")),
        name="hyena_fwd",
    )(cmat, smat, z, kc, ks)


def _inv_kernel(c_ref, s_ref, pc_ref, ps_ref, z_ref, x0_ref, zn_ref, kn_ref, skip_ref, o_ref, *, seq):
    tt = c_ref.shape[0]
    y = jnp.dot(c_ref[...], pc_ref[0], preferred_element_type=F32)
    y = y + jnp.dot(s_ref[...], ps_ref[0], preferred_element_type=F32)
    t = pl.program_id(1) * tt + lax.broadcasted_iota(I32, (tt, 1), 0)
    sign = (1 - 2 * (t & 1)).astype(F32)
    nyq = zn_ref[0] * kn_ref[...] * (1.0 / float(2 * seq))
    y = y + sign * nyq + z_ref[0].astype(F32) * skip_ref[...]
    o_ref[0] = (x0_ref[0].astype(F32) * y).astype(BF16)


def _hy_inv(cmat, smat, pc, ps, z, x0, zn, kn, skip):
    b, seq, _ = z.shape
    tt = min(1024, seq)
    cb = 512
    return pl.pallas_call(
        functools.partial(_inv_kernel, seq=seq),
        grid=(HY_WIDTH // cb, seq // tt, b),
        in_specs=[pl.BlockSpec((tt, seq), lambda j, i, bi: (i, 0)),
                  pl.BlockSpec((tt, seq), lambda j, i, bi: (i, 0)),
                  pl.BlockSpec((1, seq, cb), lambda j, i, bi: (bi, 0, j)),
                  pl.BlockSpec((1, seq, cb), lambda j, i, bi: (bi, 0, j)),
                  pl.BlockSpec((1, tt, cb), lambda j, i, bi: (bi, i, j)),
                  pl.BlockSpec((1, tt, cb), lambda j, i, bi: (bi, i, j)),
                  pl.BlockSpec((1, 1, cb), lambda j, i, bi: (bi, 0, j)),
                  pl.BlockSpec((1, cb), lambda j, i, bi: (0, j)),
                  pl.BlockSpec((1, cb), lambda j, i, bi: (0, j))],
        out_specs=pl.BlockSpec((1, tt, cb), lambda j, i, bi: (bi, i, j)),
        out_shape=jax.ShapeDtypeStruct((b, seq, HY_WIDTH), BF16),
        compiler_params=_cparams(("arbitrary", "arbitrary", "arbitrary")),
        name="hyena_inv",
    )(cmat, smat, pc, ps, z, x0, zn, kn, skip)


def _layer_norm(v, g, b):
    mu = jnp.mean(v, axis=-1, keepdims=True)
    vc = v - mu
    var = jnp.mean(vc * vc, axis=-1, keepdims=True)
    return vc * lax.rsqrt(var + EPS) * g + b


def _merge_kernel(at_ref, hy_ref, ga_ref, gb_ref, woa_ref, woh_ref, o_ref):
    a = jnp.dot(at_ref[...], woa_ref[...], preferred_element_type=F32)
    hh = jnp.dot(hy_ref[...], woh_ref[...], preferred_element_type=F32)
    merged = _sigmoid(ga_ref[...].astype(F32)) * a + _sigmoid(gb_ref[...].astype(F32)) * hh
    o_ref[...] = merged.astype(BF16)


def _merge(attn, hy, big, woa, woh):
    n = attn.shape[0]
    d = woa.shape[1]
    tm = 512
    return pl.pallas_call(
        _merge_kernel,
        grid=(n // tm,),
        in_specs=[pl.BlockSpec((tm, attn.shape[1]), lambda i: (i, 0)),
                  pl.BlockSpec((tm, hy.shape[1]), lambda i: (i, 0)),
                  pl.BlockSpec((tm, d), lambda i: (i, 0)),
                  pl.BlockSpec((tm, d), lambda i: (i, 1)),
                  _const_spec(woa.shape), _const_spec(woh.shape)],
        out_specs=pl.BlockSpec((tm, d), lambda i: (i, 0)),
        out_shape=jax.ShapeDtypeStruct((n, d), BF16),
        compiler_params=_cparams(("arbitrary",)),
        name="merge",
    )(attn, hy, big, big, woa, woh)


def _ln1_kernel(x_ref, mg_ref, mod_ref, wout_ref, g_ref, b_ref, wr_ref, x1_ref, h2_ref, lg_ref, *, n_sub):
    m = mod_ref[0]
    rows = x_ref.shape[0] // n_sub
    for t in range(n_sub):
        rs = slice(t * rows, (t + 1) * rows)
        mix = jnp.dot(mg_ref[rs, :], wout_ref[...], preferred_element_type=F32)
        x1 = _layer_norm(ALPHA * x_ref[rs, :] + m[2:3] * mix, g_ref[...], b_ref[...])
        x1_ref[rs, :] = x1
        h2 = x1 * (1.0 + m[4:5]) + m[3:4]
        h2_ref[rs, :] = h2
        hi = h2.astype(BF16)
        lo = (h2 - hi.astype(F32)).astype(BF16)
        lg = (jnp.dot(hi, wr_ref[...], preferred_element_type=F32)
              + jnp.dot(lo, wr_ref[...], preferred_element_type=F32))
        lg_ref[rs, :] = lg + pltpu.roll(lg, LANES - N_EXPERTS, axis=1)


def _ln1(x, merged, mods, mod_row, wout, g, b, wr_pack):
    n, d = x.shape
    tm = 512
    return pl.pallas_call(
        functools.partial(_ln1_kernel, n_sub=2),
        grid=(n // tm,),
        in_specs=[pl.BlockSpec((tm, d), lambda i: (i, 0)),
                  pl.BlockSpec((tm, d), lambda i: (i, 0)),
                  pl.BlockSpec((1, N_MOD, d), lambda i: (mod_row(i * tm), 0, 0)),
                  _const_spec(wout.shape), _const_spec((1, d)), _const_spec((1, d)),
                  _const_spec(wr_pack.shape)],
        out_specs=[pl.BlockSpec((tm, d), lambda i: (i, 0)),
                   pl.BlockSpec((tm, d), lambda i: (i, 0)),
                   pl.BlockSpec((tm, LANES), lambda i: (i, 0))],
        out_shape=[jax.ShapeDtypeStruct((n, d), F32),
                   jax.ShapeDtypeStruct((n, d), F32),
                   jax.ShapeDtypeStruct((n, LANES), F32)],
        compiler_params=_cparams(("arbitrary",)),
        name="ln1_router",
    )(x, merged, mods, wout, g, b, wr_pack)


def _split3(t):
    t1 = t.astype(BF16)
    r = t - t1.astype(F32)
    t2 = r.astype(BF16)
    t3 = (r - t2.astype(F32)).astype(BF16)
    return t1, t2, t3


def _dot_exact(sel01, table):
    t1, t2, t3 = _split3(table)
    out = jnp.dot(sel01, t1, preferred_element_type=F32)
    out = out + jnp.dot(sel01, t2, preferred_element_type=F32)
    return out + jnp.dot(sel01, t3, preferred_element_type=F32)


def _route_kernel(lg_ref, idx_ref, q_ref, gate_ref, start_ref, aff_scr, sel_scr, pos_scr, *, cap):
    ne, nc, _ = lg_ref.shape
    lg = lg_ref[...]
    mx = jnp.max(lg, axis=0, keepdims=True)
    ex = jnp.exp(lg - mx)
    aff = ex / jnp.sum(ex, axis=0, keepdims=True)
    aff_scr[...] = aff

    def tbody(k, thr):
        cand = thr | lax.shift_left(jnp.int32(1), 30 - k)
        ge = jnp.where(aff >= lax.bitcast_convert_type(cand, F32), 1.0, 0.0)
        cnt = jnp.sum(jnp.sum(ge, axis=1, keepdims=True), axis=2, keepdims=True)
        return jnp.where(cnt >= float(cap), cand, thr)

    thr = lax.fori_loop(0, 31, tbody, jnp.zeros((ne, 1, 1), I32))
    lo = lax.bitcast_convert_type(thr, F32)
    above = aff >= lax.bitcast_convert_type(thr + 1, F32)
    sel_scr[...] = jnp.where(above, 1.0, 0.0)
    pos_scr[...] = jnp.where((aff >= lo) & jnp.logical_not(above), aff, -1.0)

    r_i = lax.broadcasted_iota(I32, (LANES, LANES), 0)
    c_i = lax.broadcasted_iota(I32, (LANES, LANES), 1)
    upper_incl = jnp.where(r_i <= c_i, 1.0, 0.0).astype(BF16)
    ones_sq = jnp.ones((LANES, LANES), BF16)
    rr_i = lax.broadcasted_iota(I32, (nc, nc), 0)
    cc_i = lax.broadcasted_iota(I32, (nc, nc), 1)
    lower_strict = jnp.where(cc_i < rr_i, 1.0, 0.0).astype(BF16)
    upper_rows = jnp.where(rr_i <= cc_i, 1.0, 0.0).astype(BF16)

    def prefix(x01):
        xb = x01.astype(BF16)
        p1 = jnp.dot(xb, upper_incl, preferred_element_type=F32)
        totb = jnp.dot(xb, ones_sq, preferred_element_type=F32)
        offs = jnp.dot(lower_strict, totb.astype(BF16), preferred_element_type=F32)
        return p1, offs

    tok = (lax.broadcasted_iota(I32, (nc, LANES), 0) * LANES
           + lax.broadcasted_iota(I32, (nc, LANES), 1)).astype(F32)

    def reduce2(fn, v):
        return fn(fn(v, axis=0, keepdims=True), axis=1, keepdims=True)

    def pass1(e, start):
        above_e = sel_scr[e]
        need = cap - jnp.sum(above_e).astype(I32)

        def pick(_, carry):
            vals, picked = carry
            first = reduce2(jnp.min, jnp.where(vals == reduce2(jnp.max, vals), tok, float(nc * LANES)))
            hit = tok == first
            return jnp.where(hit, -1.0, vals), picked + jnp.where(hit, 1.0, 0.0)

        _, picked = lax.fori_loop(0, need, pick, (pos_scr[e], jnp.zeros((nc, LANES), F32)))
        sel = above_e + picked
        sel_scr[e] = sel
        s1, soffs = prefix(sel)
        return start + (s1 + soffs - sel)

    start = lax.fori_loop(0, ne, pass1, jnp.zeros((nc, LANES), F32))
    start_ref[...] = start.astype(I32)

    p_col = lax.broadcasted_iota(I32, (cap, 1), 0).astype(F32)
    lane_row = lax.broadcasted_iota(I32, (1, LANES), 1).astype(F32)
    chunk_row = lax.broadcasted_iota(I32, (1, nc), 1).astype(F32)
    ones8 = jnp.ones((8, LANES), BF16)

    def row_sums(m):
        ones = jnp.ones((8, m.shape[1]), BF16)
        nt = lambda part: lax.dot_general(ones, part, (((1,), (1,)), ((), ())), preferred_element_type=F32)
        m1, m2, m3 = _split3(m)
        return (nt(m1) + nt(m2) + nt(m3))[0:1]

    def pass2(e, before):
        sel = sel_scr[e]
        selb = sel.astype(BF16)
        p1, offs = prefix(sel)
        tot_row = lax.dot_general(ones8, selb, (((1,), (1,)), ((), ())), preferred_element_type=F32)
        cum_row = jnp.dot(tot_row.astype(BF16), upper_rows, preferred_element_type=F32)
        cum1 = cum_row[0:1]
        prev1 = cum1 - tot_row[0:1]
        in_chunk = (prev1 <= p_col) & (p_col < cum1)
        ohc = jnp.where(in_chunk, 1.0, 0.0).astype(BF16)
        keyg = jnp.dot(ohc, (p1 * sel).astype(BF16), preferred_element_type=F32)
        offg = _dot_exact(ohc, offs)
        ohl = keyg == (p_col - offg + 1.0)
        l_p = row_sums(jnp.where(ohl, lane_row, 0.0))
        c_p = row_sums(jnp.where(in_chunk, chunk_row, 0.0))
        idx_ref[e] = (c_p * float(LANES) + l_p).astype(I32)
        qtab = _dot_exact(ohc, start) + jnp.dot(ohc, before.astype(BF16), preferred_element_type=F32)
        q_ref[e] = row_sums(jnp.where(ohl, qtab, 0.0)).astype(I32)
        afg = _dot_exact(ohc, aff_scr[e])
        gate_ref[e] = row_sums(jnp.where(ohl, afg, 0.0))
        return before + sel

    lax.fori_loop(0, ne, pass2, jnp.zeros((nc, LANES), F32))


def _route(lg3, cap):
    ne, nc, _ = lg3.shape
    full = lambda s: pl.BlockSpec(s, lambda: (0,) * len(s))
    return pl.pallas_call(
        functools.partial(_route_kernel, cap=cap),
        in_specs=[full(lg3.shape)],
        out_specs=[full((ne, 1, cap)), full((ne, 1, cap)), full((ne, 1, cap)), full((nc, LANES))],
        out_shape=[jax.ShapeDtypeStruct((ne, 1, cap), I32),
                   jax.ShapeDtypeStruct((ne, 1, cap), I32),
                   jax.ShapeDtypeStruct((ne, 1, cap), F32),
                   jax.ShapeDtypeStruct((nc, LANES), I32)],
        scratch_shapes=[pltpu.VMEM((ne, nc, LANES), F32),
                        pltpu.VMEM((ne, nc, LANES), F32),
                        pltpu.VMEM((ne, nc, LANES), F32)],
        compiler_params=pltpu.CompilerParams(vmem_limit_bytes=VMEM_LIMIT),
        name="route",
    )(lg3)


def _ffn_kernel(idx_ref, q_ref, gate_ref, wg_ref, wu_ref, wd_ref, h_hbm, y_hbm,
                xb, xa, yb, ya, gsem, ssem, *, tm, npair, total):
    step = pl.program_id(0) * npair + pl.program_id(1)
    t0 = 2 * step
    t1 = t0 + 1

    def gather_row(base, r, buf, s):
        return pltpu.make_async_copy(h_hbm.at[pl.ds(idx_ref[base + r], 1), :], buf.at[pl.ds(r, 1), :], gsem.at[s])

    def scatter_row(base, r, buf, s):
        return pltpu.make_async_copy(buf.at[pl.ds(r, 1), :], y_hbm.at[pl.ds(q_ref[base + r], 1), :], ssem.at[s])

    def wait_gather(buf, s):
        pltpu.make_async_copy(h_hbm.at[pl.ds(0, tm), :], buf, gsem.at[s]).wait()

    def wait_scatter(buf, s):
        pltpu.make_async_copy(buf, y_hbm.at[pl.ds(0, tm), :], ssem.at[s]).wait()

    n_chunk = 4
    fc = EXPERT_FF // n_chunk
    per = tm // n_chunk

    def compute(x_buf, y_buf, r0, gather_base, gather_buf, scatter_base, scatter_buf, s):
        x = x_buf[...].astype(BF16)
        gate = gate_ref[r0:r0 + tm, :]
        for c in range(n_chunk):
            for r in range(c * per, (c + 1) * per):
                gather_row(gather_base, r, gather_buf, s).start()
            for r in range(c * per, (c + 1) * per):
                scatter_row(scatter_base, r, scatter_buf, s).start()
            g = jnp.dot(x, wg_ref[0, :, c * fc:(c + 1) * fc], preferred_element_type=F32)
            u = jnp.dot(x, wu_ref[0, :, c * fc:(c + 1) * fc], preferred_element_type=F32)
            hid = (g * _sigmoid(g) * u * gate).astype(BF16)
            part = jnp.dot(hid, wd_ref[0, c * fc:(c + 1) * fc, :], preferred_element_type=F32)
            if c == 0:
                y_buf[...] = part
            else:
                y_buf[...] += part

    @pl.when(step == 0)
    def _():
        yb[...] = jnp.zeros(yb.shape, F32)

        def body(r, carry):
            gather_row(0, r, xa, 0).start()
            return carry

        lax.fori_loop(0, tm, body, 0)

    wait_gather(xa, 0)

    @pl.when(step >= 1)
    def _():
        wait_scatter(ya, 0)

    base1 = t1 * tm
    pbase = jnp.maximum(t0 - 1, 0) * tm
    compute(xa, ya, 0, base1, xb, pbase, yb, 1)

    wait_gather(xb, 1)
    wait_scatter(yb, 1)
    nbase = jnp.minimum(t1 + 1, total - 1) * tm
    base0 = t0 * tm
    compute(xb, yb, tm, nbase, xa, base0, ya, 0)

    @pl.when(t1 == total - 1)
    def _():
        wait_gather(xa, 0)
        wait_scatter(ya, 0)

        def body(r, carry):
            scatter_row(base1, r, yb, 1).start()
            return carry

        lax.fori_loop(0, tm, body, 0)
        wait_scatter(yb, 1)


def _ffn(h2, idx, qpos, gate, wg, wu, wd):
    n, d = h2.shape
    ne = wg.shape[0]
    cap = idx.shape[0] // ne
    tm = min(512, cap // 2)
    npair = cap // (2 * tm)
    total = 2 * ne * npair
    grid_spec = pltpu.PrefetchScalarGridSpec(
        num_scalar_prefetch=2,
        grid=(ne, npair),
        in_specs=[pl.BlockSpec((2 * tm, 1), lambda e, i, *_: (e * npair + i, 0)),
                  pl.BlockSpec((1, d, EXPERT_FF), lambda e, i, *_: (e, 0, 0)),
                  pl.BlockSpec((1, d, EXPERT_FF), lambda e, i, *_: (e, 0, 0)),
                  pl.BlockSpec((1, EXPERT_FF, d), lambda e, i, *_: (e, 0, 0)),
                  pl.BlockSpec(memory_space=pl.ANY)],
        out_specs=pl.BlockSpec(memory_space=pl.ANY),
        scratch_shapes=[pltpu.VMEM((tm, d), F32), pltpu.VMEM((tm, d), F32),
                        pltpu.VMEM((tm, d), F32), pltpu.VMEM((tm, d), F32),
                        pltpu.SemaphoreType.DMA((2,)),
                        pltpu.SemaphoreType.DMA((2,))],
    )
    return pl.pallas_call(
        functools.partial(_ffn_kernel, tm=tm, npair=npair, total=total),
        grid_spec=grid_spec,
        out_shape=jax.ShapeDtypeStruct((ne * cap, d), F32),
        compiler_params=_cparams(("arbitrary", "arbitrary")),
        name="expert_ffn",
    )(idx, qpos, gate, wg, wu, wd, h2)


def _combine_kernel(cs_ref, x1_ref, mod_ref, st_ref, en_ref, g_ref, b_ref, y_hbm, o_ref,
                    ybuf, acc, sem, *, win, total, nsteps):
    i = pl.program_id(0)
    slot = i % 2
    lo0 = cs_ref[i]
    hi = cs_ref[i + 1]
    adv = win - 8

    def win_start(lo):
        return pl.multiple_of((jnp.minimum(lo, total - win) // 8) * 8, 8)

    def window(lo, sl):
        return pltpu.make_async_copy(y_hbm.at[pl.ds(win_start(lo), win), :], ybuf.at[sl], sem.at[sl])

    @pl.when(i == 0)
    def _():
        window(lo0, 0).start()

    @pl.when(i + 1 < nsteps)
    def _():
        window(hi, 1 - slot).start()

    ntrip = jnp.maximum((hi - lo0 + adv - 1) // adv, 1)
    acc[...] = jnp.zeros_like(acc)
    st = st_ref[...]
    en = en_ref[...]

    def body(k, carry):
        lo = lo0 + k * adv

        @pl.when(k > 0)
        def _():
            window(lo, slot).start()

        window(lo, slot).wait()
        qj = win_start(lo) + lax.broadcasted_iota(I32, (1, win), 1)
        seg = (qj >= st) & (qj < en) & (qj >= lo) & (qj < lo + adv)
        acc[...] += jnp.dot(jnp.where(seg, 1.0, 0.0).astype(BF16), ybuf[slot].astype(BF16),
                            preferred_element_type=F32)
        return carry

    lax.fori_loop(0, ntrip, body, 0)
    m = mod_ref[0]
    o_ref[...] = _layer_norm(ALPHA * x1_ref[...] + m[5:6] * acc[...], g_ref[...], b_ref[...])


def _combine(x1, mods, mod_row, start, pairs, g, b):
    n, d = x1.shape
    total = pairs.shape[0]
    tc = 256
    win = 640
    st = start.reshape(n, 1)
    en = jnp.concatenate([start[1:], jnp.full((1,), total, I32)]).reshape(n, 1)
    cs = jnp.concatenate([start[::tc], jnp.full((1,), total, I32)])
    grid_spec = pltpu.PrefetchScalarGridSpec(
        num_scalar_prefetch=1,
        grid=(n // tc,),
        in_specs=[pl.BlockSpec((tc, d), lambda i, *_: (i, 0)),
                  pl.BlockSpec((1, N_MOD, d), lambda i, *_: (mod_row(i * tc), 0, 0)),
                  pl.BlockSpec((tc, 1), lambda i, *_: (i, 0)),
                  pl.BlockSpec((tc, 1), lambda i, *_: (i, 0)),
                  pl.BlockSpec((1, d), lambda i, *_: (0, 0)),
                  pl.BlockSpec((1, d), lambda i, *_: (0, 0)),
                  pl.BlockSpec(memory_space=pl.ANY)],
        out_specs=pl.BlockSpec((tc, d), lambda i, *_: (i, 0)),
        scratch_shapes=[pltpu.VMEM((2, win, d), F32),
                        pltpu.VMEM((tc, d), F32),
                        pltpu.SemaphoreType.DMA((2,))],
    )
    return pl.pallas_call(
        functools.partial(_combine_kernel, win=win, total=total, nsteps=n // tc),
        grid_spec=grid_spec,
        out_shape=jax.ShapeDtypeStruct((n, d), F32),
        compiler_params=_cparams(("arbitrary",)),
        name="combine_ln2",
    )(cs, x1, mods, st, en, g, b, pairs)


def _swap_pairs(w):
    f = ROPE_FREQS
    return jnp.concatenate([w[..., f:2 * f], w[..., 0:f], w[..., 3 * f:4 * f], w[..., 2 * f:3 * f]], axis=-1)


def _rope_tables(seq):
    n_rows = seq // GRID_W
    row = np.repeat(np.arange(n_rows, dtype=np.float64), GRID_W)
    col = np.tile(np.arange(GRID_W, dtype=np.float64), n_rows)
    inv = ROPE_THETA ** (-np.arange(ROPE_FREQS, dtype=np.float64) / ROPE_FREQS)
    ar = row[:, None] * inv
    ac = col[:, None] * inv
    z = np.zeros((seq, LANES - QK_ROPE))
    cos128 = np.concatenate([np.cos(ar), np.cos(ar), np.cos(ac), np.cos(ac), z], axis=1)
    sin128 = np.concatenate([-np.sin(ar), np.sin(ar), -np.sin(ac), np.sin(ac), z], axis=1)
    return jnp.asarray(cos128, F32), jnp.asarray(sin128, F32)


def _dft_tables(seq):
    m = 2 * seq
    blk = 64
    s = np.arange(seq, dtype=np.int64)
    ang1 = ((np.arange(0, seq, blk, dtype=np.int64)[:, None] * s[None, :]) % m) * (2.0 * math.pi / m)
    ang0 = ((np.arange(blk, dtype=np.int64)[:, None] * s[None, :]) % m) * (2.0 * math.pi / m)
    c1, s1 = jnp.asarray(np.cos(ang1), F32)[:, None, :], jnp.asarray(np.sin(ang1), F32)[:, None, :]
    c0, s0 = jnp.asarray(np.cos(ang0), F32)[None, :, :], jnp.asarray(np.sin(ang0), F32)[None, :, :]
    cmat = (c1 * c0 - s1 * s0).reshape(seq, seq).astype(BF16)
    smat = (s1 * c0 + c1 * s0).reshape(seq, seq).astype(BF16)
    return cmat, smat


def _prep_weights(p):
    w = {}
    w_in = p['w_in']
    s0, s1, s2, s3, s4 = (Q_LORA, Q_LORA + KV_LORA, Q_LORA + KV_LORA + QK_ROPE,
                          Q_LORA + KV_LORA + QK_ROPE + 3 * HY_WIDTH,
                          Q_LORA + KV_LORA + QK_ROPE + 3 * HY_WIDTH + D_MODEL)
    w_in = w_in.astype(BF16)
    kr = w_in[:, s1:s2]
    z64 = jnp.zeros((D_MODEL, LANES - QK_ROPE), BF16)
    w['w_in'] = jnp.concatenate([w_in[:, :s1], kr, z64, _swap_pairs(kr), z64,
                                 w_in[:, s3:], w_in[:, s2:s3]], axis=1)
    scale = (QK_NOPE + QK_ROPE) ** -0.5 * math.log2(math.e)
    wq = (p['w_uq'] * scale).reshape(Q_LORA, MLA_HEADS, QK_NOPE + QK_ROPE)
    zq = jnp.zeros((Q_LORA, MLA_HEADS, HEAD_PAD - QK_NOPE - QK_ROPE), F32)
    w['wa'] = jnp.concatenate([wq, zq], axis=-1).reshape(Q_LORA, MLA_HEADS * HEAD_PAD).astype(BF16)
    w['wb'] = jnp.concatenate([_swap_pairs(wq[..., QK_NOPE:]), zq], axis=-1).reshape(
        Q_LORA, MLA_HEADS * LANES).astype(BF16)
    wkv = p['w_ukv'].reshape(KV_LORA, MLA_HEADS, QK_NOPE + V_HEAD)
    w['wk'] = wkv[..., :QK_NOPE].reshape(KV_LORA, MLA_HEADS * QK_NOPE).astype(BF16)
    w['wv'] = wkv[..., QK_NOPE:].reshape(KV_LORA, MLA_HEADS * V_HEAD).astype(BF16)
    w['woa'] = p['w_o_mla'].astype(BF16)
    w['woh'] = p['w_o_hy'].astype(BF16)
    w['wout'] = p['w_out'].astype(BF16)
    wr_hi = p['w_router'].astype(BF16)
    wr_lo = (p['w_router'] - wr_hi.astype(F32)).astype(BF16)
    w['wr_pack'] = jnp.pad(jnp.concatenate([wr_hi, wr_lo], axis=1), ((0, 0), (0, LANES - 2 * N_EXPERTS)))
    w['wg'] = p['w_gate'].astype(BF16)
    w['wu'] = p['w_up'].astype(BF16)
    w['wd'] = p['w_down'].astype(BF16)
    w['hy_w'] = p['hy_short_w'].reshape(3, 3, HY_WIDTH).transpose(1, 0, 2)
    w['hy_b'] = p['hy_short_b'].reshape(3, 1, HY_WIDTH)
    return w


def _trunk(x3, mods, mod_row, tm_in, p, w, cache_kv, cache_kr, rope):
    b, seq, d = x3.shape
    n = b * seq
    x = x3.reshape(n, d)
    small, big = _inproj(x, mods, mod_row, w['w_in'], tm_in)
    cos128, sin128 = _rope_tables(seq) if rope else (None, None)
    q, ckv, kr = _qprep(small, p['q_norm_g'].reshape(1, -1), p['kv_norm_g'].reshape(1, -1),
                        w['wa'], w['wb'], cos128, sin128, seq, rope)
    ckv3 = ckv.reshape(b, seq, KV_LORA)
    kr3 = kr.reshape(b, seq, LANES)
    if cache_kv is None:
        kv_all, kr_all = ckv3, kr3
    else:
        kv_all = jnp.concatenate([cache_kv, ckv3], axis=1)
        kr_all = jnp.concatenate([jnp.pad(cache_kr, ((0, 0), (0, 0), (0, LANES - QK_ROPE))), kr3], axis=1)
    kh, vh = _kvup(kv_all, kr_all, w['wk'], w['wv'])
    attn = _attention(q.reshape(b, seq, MLA_HEADS * HEAD_PAD), kh, vh).reshape(n, MLA_HEADS * V_HEAD)

    x0c, z, zn = _hypre(big.reshape(b, seq, BIG_W), w['hy_w'], w['hy_b'])
    a_td, d_td, kn = _filter_td(seq, p)
    cmat, smat = _dft_tables(seq)
    kc, ks = _kspec(cmat, smat, a_td, d_td)
    pc, ps = _hy_fwd(cmat, smat, z, kc, ks)
    hy = _hy_inv(cmat, smat, pc, ps, z, x0c, zn, kn, p['hy_skip'].reshape(1, -1)).reshape(n, HY_WIDTH)

    merged = _merge(attn, hy, big, w['woa'], w['woh'])
    x1, h2, lg = _ln1(x, merged, mods, mod_row, w['wout'], p['ln1_g'].reshape(1, -1),
                      p['ln1_b'].reshape(1, -1), w['wr_pack'])
    cap = EC_CAPACITY * n // N_EXPERTS
    lg3 = lg[:, :N_EXPERTS].T.reshape(N_EXPERTS, n // LANES, LANES)
    idx, qpos, gate, start = _route(lg3, cap)
    pairs = _ffn(h2, idx.reshape(-1), qpos.reshape(-1), gate.reshape(-1, 1), w['wg'], w['wu'], w['wd'])
    y = _combine(x1, mods, mod_row, start.reshape(-1), pairs,
                 p['ln2_g'].reshape(1, -1), p['ln2_b'].reshape(1, -1))
    return y.reshape(b, seq, d), ckv3, kr3[..., :QK_ROPE]


def kernel(x_prompt, x_sample, cache_kv_c, cache_k_rope, c, c_ctx, w_ada, b_ada, w_in, q_norm_g, kv_norm_g, w_uq, w_ukv, w_o_mla, hy_short_w, hy_short_b, hy_filt_w1, hy_filt_b1, hy_filt_freq1, hy_filt_w2, hy_filt_b2, hy_filt_freq2, hy_filt_w3, hy_filt_b3, hy_skip, w_o_hy, w_out, ln1_g, ln1_b, ln2_g, ln2_b, w_router, w_gate, w_up, w_down):
    params = dict(w_in=w_in, q_norm_g=q_norm_g, kv_norm_g=kv_norm_g, w_uq=w_uq, w_ukv=w_ukv, w_o_mla=w_o_mla,
                  hy_short_w=hy_short_w, hy_short_b=hy_short_b, hy_filt_w1=hy_filt_w1, hy_filt_b1=hy_filt_b1,
                  hy_filt_freq1=hy_filt_freq1, hy_filt_w2=hy_filt_w2, hy_filt_b2=hy_filt_b2,
                  hy_filt_freq2=hy_filt_freq2, hy_filt_w3=hy_filt_w3, hy_filt_b3=hy_filt_b3, hy_skip=hy_skip,
                  w_o_hy=w_o_hy, w_out=w_out, ln1_g=ln1_g, ln1_b=ln1_b, ln2_g=ln2_g, ln2_b=ln2_b,
                  w_router=w_router, w_gate=w_gate, w_up=w_up, w_down=w_down)
    depth = w_in.shape[0]
    dec_b, dec_seq, d = x_sample.shape
    n_rows = 16
    cond = jnp.zeros((n_rows, d), F32).at[:dec_b].set(c).at[dec_b].set(c_ctx)
    y_prompt, y_sample = x_prompt, x_sample
    kv_list, kr_list = [], []
    for l in range(depth):
        p = {k: v[l] for k, v in params.items()}
        w = _prep_weights(p)
        mods = _ada_mod(cond, w_ada[l], b_ada[l]).reshape(n_rows, N_MOD, d)
        n_prompt = x_prompt.shape[0] * x_prompt.shape[1]
        y_prompt, c_kv, k_rope = _trunk(y_prompt, mods, lambda r: dec_b, min(1024, n_prompt), p, w,
                                        None, None, False)
        kv_list.append(c_kv)
        kr_list.append(k_rope)
        y_sample, _, _ = _trunk(y_sample, mods, lambda r: r // dec_seq, min(1024, dec_seq), p, w,
                                cache_kv_c[:, l], cache_k_rope[:, l], True)
    return (y_prompt, y_sample, jnp.stack(kv_list, axis=1), jnp.stack(kr_list, axis=1))
```

```python
import functools
import math

import numpy as np
import jax
import jax.numpy as jnp
from jax import lax
from jax.experimental import pallas as pl
from jax.experimental.pallas import tpu as pltpu

F32 = jnp.float32
BF16 = jnp.bfloat16
I32 = jnp.int32
HIGHEST = lax.Precision.HIGHEST

D_MODEL = 2048
GRID_W = 64
MLA_HEADS = 8
QK_NOPE = 128
QK_ROPE = 64
V_HEAD = 128
Q_LORA = 512
KV_LORA = 256
ROPE_THETA = 10000.0
ROPE_FREQS = QK_ROPE // 4
HY_WIDTH = 1024
HY_EMB = 33
HY_BANDS = (HY_EMB - 1) // 2
HY_HIDDEN = 64
HY_FAST_DECAY = 0.3
HY_SLOW_DECAY = 1.5
HY_TARGET = 0.01
N_EXPERTS = 16
EXPERT_FF = 1024
EC_CAPACITY = 2
EPS = 1e-6
DEPTH = 1
ALPHA = (2 * DEPTH) ** 0.25
N_MOD = 6

LANES = 128
HEAD_PAD = 256
SMALL_W = 1024
BIG_W = 2 * D_MODEL + 3 * HY_WIDTH
VMEM_LIMIT = 56 * 1024 * 1024
HY_TIME_BLOCKS = 4


def _cparams(sem, vmem=VMEM_LIMIT):
    return pltpu.CompilerParams(dimension_semantics=sem, vmem_limit_bytes=vmem)


def _sigmoid(x):
    return 1.0 / (1.0 + jnp.exp(-x))


def _const_spec(shape):
    nd = len(shape)
    return pl.BlockSpec(shape, lambda *_: (0,) * nd, pipeline_mode=pl.Buffered(1))


def _ada_kernel(c_ref, w_ref, b_ref, o_ref):
    c = c_ref[...]
    s = (c * _sigmoid(c)).astype(BF16)
    o_ref[...] = jnp.dot(s, w_ref[...].astype(BF16), preferred_element_type=F32) + b_ref[...]


def _ada_mod(cond, w_ada, b_ada):
    r, d = cond.shape
    n = w_ada.shape[1]
    tn = 512
    return pl.pallas_call(
        _ada_kernel,
        grid=(n // tn,),
        in_specs=[pl.BlockSpec((r, d), lambda j: (0, 0)),
                  pl.BlockSpec((d, tn), lambda j: (0, j)),
                  pl.BlockSpec((1, tn), lambda j: (0, j))],
        out_specs=pl.BlockSpec((r, tn), lambda j: (0, j)),
        out_shape=jax.ShapeDtypeStruct((r, n), F32),
        compiler_params=_cparams(("arbitrary",)),
        name="ada_mod",
    )(cond, w_ada, b_ada.reshape(1, n))


def _inproj_kernel(x_ref, mod_ref, w_ref, small_ref, big_ref, h_scr, *, n_small):
    j = pl.program_id(1)

    @pl.when(j == 0)
    def _():
        m = mod_ref[0]
        h_scr[...] = (x_ref[...] * (1.0 + m[1:2]) + m[0:1]).astype(BF16)

    acc = jnp.dot(h_scr[...], w_ref[...], preferred_element_type=F32)

    @pl.when(j < n_small)
    def _():
        small_ref[...] = acc

    @pl.when(j >= n_small)
    def _():
        big_ref[...] = acc.astype(BF16)


def _inproj(x, mods, mod_row, w_r, tm):
    n, d = x.shape
    tn = 512
    n_small = SMALL_W // tn
    n_cols = w_r.shape[1] // tn
    return pl.pallas_call(
        functools.partial(_inproj_kernel, n_small=n_small),
        grid=(n // tm, n_cols),
        in_specs=[pl.BlockSpec((tm, d), lambda i, j: (i, 0)),
                  pl.BlockSpec((1, N_MOD, d), lambda i, j: (mod_row(i * tm), 0, 0)),
                  pl.BlockSpec((d, tn), lambda i, j: (0, j))],
        out_specs=[pl.BlockSpec((tm, tn), lambda i, j: (i, jnp.minimum(j, n_small - 1))),
                   pl.BlockSpec((tm, tn), lambda i, j: (i, jnp.maximum(j - n_small, 0)))],
        out_shape=[jax.ShapeDtypeStruct((n, SMALL_W), F32),
                   jax.ShapeDtypeStruct((n, BIG_W), BF16)],
        scratch_shapes=[pltpu.VMEM((tm, d), BF16)],
        compiler_params=_cparams(("arbitrary", "arbitrary")),
        name="inproj",
    )(x, mods, w_r)


def _qprep_kernel(*refs, rope):
    if rope:
        small_ref, qg_ref, kvg_ref, wa_ref, wb_ref, c_ref, s_ref, q_ref, ckv_ref, kr_ref = refs
    else:
        small_ref, qg_ref, kvg_ref, wa_ref, q_ref, ckv_ref, kr_ref = refs
    qc = small_ref[:, 0:Q_LORA]
    kvc = small_ref[:, Q_LORA:Q_LORA + KV_LORA]
    kr = small_ref[:, Q_LORA + KV_LORA:Q_LORA + KV_LORA + LANES]
    qn = (qc * lax.rsqrt(jnp.mean(qc * qc, axis=-1, keepdims=True) + EPS) * qg_ref[...]).astype(BF16)
    ckv_ref[...] = kvc * lax.rsqrt(jnp.mean(kvc * kvc, axis=-1, keepdims=True) + EPS) * kvg_ref[...]
    qa = jnp.dot(qn, wa_ref[...], preferred_element_type=F32)
    if rope:
        krs = small_ref[:, Q_LORA + KV_LORA + LANES:SMALL_W]
        qb = jnp.dot(qn, wb_ref[...], preferred_element_type=F32)
        cs = c_ref[...]
        sn = s_ref[...]
        kr_ref[...] = kr * cs + krs * sn
        for h in range(MLA_HEADS):
            lo = h * HEAD_PAD
            q_ref[:, lo:lo + LANES] = qa[:, lo:lo + LANES].astype(BF16)
            q_ref[:, lo + LANES:lo + HEAD_PAD] = (
                qa[:, lo + LANES:lo + HEAD_PAD] * cs + qb[:, h * LANES:(h + 1) * LANES] * sn).astype(BF16)
    else:
        kr_ref[...] = kr
        q_ref[...] = qa.astype(BF16)


def _qprep(small, qg, kvg, wa, wb, cos128, sin128, seq, rope):
    n = small.shape[0]
    tm = min(512, seq)
    nblk = seq // tm
    in_specs = [pl.BlockSpec((tm, SMALL_W), lambda i: (i, 0)),
                pl.BlockSpec((1, Q_LORA), lambda i: (0, 0)),
                pl.BlockSpec((1, KV_LORA), lambda i: (0, 0)),
                pl.BlockSpec(wa.shape, lambda i: (0, 0))]
    args = [small, qg, kvg, wa]
    if rope:
        in_specs += [pl.BlockSpec(wb.shape, lambda i: (0, 0)),
                     pl.BlockSpec((tm, LANES), lambda i: (i % nblk, 0)),
                     pl.BlockSpec((tm, LANES), lambda i: (i % nblk, 0))]
        args += [wb, cos128, sin128]
    return pl.pallas_call(
        functools.partial(_qprep_kernel, rope=rope),
        grid=(n // tm,),
        in_specs=in_specs,
        out_specs=[pl.BlockSpec((tm, MLA_HEADS * HEAD_PAD), lambda i: (i, 0)),
                   pl.BlockSpec((tm, KV_LORA), lambda i: (i, 0)),
                   pl.BlockSpec((tm, LANES), lambda i: (i, 0))],
        out_shape=[jax.ShapeDtypeStruct((n, MLA_HEADS * HEAD_PAD), BF16),
                   jax.ShapeDtypeStruct((n, KV_LORA), F32),
                   jax.ShapeDtypeStruct((n, LANES), F32)],
        compiler_params=_cparams(("arbitrary",)),
        name="qprep",
    )(*args)


def _kvup_kernel(kv_ref, kr_ref, wk_ref, wv_ref, k_ref, v_ref):
    kv = kv_ref[0].astype(BF16)
    kn = jnp.dot(kv, wk_ref[...], preferred_element_type=F32)
    vv = jnp.dot(kv, wv_ref[...], preferred_element_type=F32)
    krp = kr_ref[0].astype(BF16)
    lane = lax.broadcasted_iota(I32, (kv.shape[0], LANES), 1)
    ones_col = jnp.where(lane == 0, 1.0, 0.0).astype(BF16)
    for h in range(MLA_HEADS):
        k_ref[0, h, :, 0:LANES] = kn[:, h * QK_NOPE:(h + 1) * QK_NOPE].astype(BF16)
        k_ref[0, h, :, LANES:HEAD_PAD] = krp
        v_ref[0, h, :, 0:V_HEAD] = vv[:, h * V_HEAD:(h + 1) * V_HEAD].astype(BF16)
        v_ref[0, h, :, V_HEAD:V_HEAD + LANES] = ones_col


def _kvup(kv_all, kr_all, wk, wv):
    b, lk, _ = kv_all.shape
    tm = 512 if lk % 512 == 0 else 256
    return pl.pallas_call(
        _kvup_kernel,
        grid=(b, lk // tm),
        in_specs=[pl.BlockSpec((1, tm, KV_LORA), lambda bi, i: (bi, i, 0)),
                  pl.BlockSpec((1, tm, LANES), lambda bi, i: (bi, i, 0)),
                  pl.BlockSpec(wk.shape, lambda bi, i: (0, 0)),
                  pl.BlockSpec(wv.shape, lambda bi, i: (0, 0))],
        out_specs=[pl.BlockSpec((1, MLA_HEADS, tm, HEAD_PAD), lambda bi, i: (bi, 0, i, 0)),
                   pl.BlockSpec((1, MLA_HEADS, tm, V_HEAD + LANES), lambda bi, i: (bi, 0, i, 0))],
        out_shape=[jax.ShapeDtypeStruct((b, MLA_HEADS, lk, HEAD_PAD), BF16),
                   jax.ShapeDtypeStruct((b, MLA_HEADS, lk, V_HEAD + LANES), BF16)],
        compiler_params=_cparams(("arbitrary", "arbitrary")),
        name="kvup",
    )(kv_all, kr_all, wk, wv)


def _attn_kernel(q_ref, k_ref, v_ref, o_ref, *, rows):
    for h in range(k_ref.shape[1]):
        k = k_ref[0, h]
        v = v_ref[0, h]
        for r0 in range(0, q_ref.shape[1], rows):
            q = q_ref[0, r0:r0 + rows, h * HEAD_PAD:(h + 1) * HEAD_PAD]
            s = lax.dot_general(q, k, (((1,), (1,)), ((), ())), preferred_element_type=F32)
            p = jnp.exp2(s - jnp.max(s, axis=-1, keepdims=True)).astype(BF16)
            o = jnp.dot(p, v, preferred_element_type=F32)
            o_ref[0, r0:r0 + rows, h * V_HEAD:(h + 1) * V_HEAD] = (
                o[:, 0:V_HEAD] / o[:, V_HEAD:V_HEAD + 1]).astype(BF16)


def _attention(q, k, v):
    b, l, _ = q.shape
    lk = k.shape[2]
    rows = 256
    chains = 8
    tq = min(l, chains * rows)
    hb = max(1, chains * rows // tq)
    return pl.pallas_call(
        functools.partial(_attn_kernel, rows=rows),
        grid=(b, MLA_HEADS // hb, l // tq),
        in_specs=[pl.BlockSpec((1, tq, hb * HEAD_PAD), lambda bi, h, i: (bi, i, h)),
                  pl.BlockSpec((1, hb, lk, HEAD_PAD), lambda bi, h, i: (bi, h, 0, 0)),
                  pl.BlockSpec((1, hb, lk, V_HEAD + LANES), lambda bi, h, i: (bi, h, 0, 0))],
        out_specs=pl.BlockSpec((1, tq, hb * V_HEAD), lambda bi, h, i: (bi, i, h)),
        out_shape=jax.ShapeDtypeStruct((b, l, MLA_HEADS * V_HEAD), BF16),
        compiler_params=_cparams(("arbitrary", "arbitrary", "arbitrary")),
        name="attention",
    )(q, k, v)


def _hypre_kernel(x0_ref, x1_ref, v_ref, w_ref, b_ref, x0o_ref, z_ref, zn_ref, *, nb):
    seq = x0_ref.shape[1]
    tb = seq // nb
    row = lax.broadcasted_iota(I32, (seq, 1), 0)

    def conv(u_ref, g):
        u = u_ref[0].astype(F32)
        up = jnp.where(row == 0, 0.0, pltpu.roll(u, 1, axis=0))
        un = jnp.where(row == seq - 1, 0.0, pltpu.roll(u, seq - 1, axis=0))
        w = w_ref[g]
        return up * w[0:1] + u * w[1:2] + un * w[2:3] + b_ref[g]

    x0o_ref[0] = conv(x0_ref, 0).astype(BF16)
    z = conv(x1_ref, 1) * conv(v_ref, 2)
    z_ref[0] = z.astype(BF16)
    zs = z * (1 - 2 * (row & 1)).astype(F32)
    for j in range(nb):
        zn_ref[0, j:j + 1, :] = jnp.sum(zs[j * tb:(j + 1) * tb], axis=0, keepdims=True)


def _hypre(big3, w3, b3, nb):
    b, l, _ = big3.shape
    cb = 256
    ncb = HY_WIDTH // cb
    off = 2 * D_MODEL // cb
    return pl.pallas_call(
        functools.partial(_hypre_kernel, nb=nb),
        grid=(b, ncb),
        in_specs=[pl.BlockSpec((1, l, cb), lambda bi, j: (bi, 0, off + j)),
                  pl.BlockSpec((1, l, cb), lambda bi, j: (bi, 0, off + j + ncb)),
                  pl.BlockSpec((1, l, cb), lambda bi, j: (bi, 0, off + j + 2 * ncb)),
                  pl.BlockSpec((3, 3, cb), lambda bi, j: (0, 0, j)),
                  pl.BlockSpec((3, 1, cb), lambda bi, j: (0, 0, j))],
        out_specs=[pl.BlockSpec((1, l, cb), lambda bi, j: (bi, 0, j)),
                   pl.BlockSpec((1, l, cb), lambda bi, j: (bi, 0, j)),
                   pl.BlockSpec((1, nb, cb), lambda bi, j: (bi, 0, j))],
        out_shape=[jax.ShapeDtypeStruct((b, l, HY_WIDTH), BF16),
                   jax.ShapeDtypeStruct((b, l, HY_WIDTH), BF16),
                   jax.ShapeDtypeStruct((b, nb, HY_WIDTH), F32)],
        compiler_params=_cparams(("arbitrary", "arbitrary")),
        name="hyena_pre",
    )(big3, big3, big3, w3, b3)


def _filter_kernel(feat_ref, w1_ref, b1_ref, f1_ref, w2_ref, b2_ref, f2_ref,
                   w3f_ref, w3b_ref, b3f_ref, b3b_ref, dl_ref, hf_ref, hb_ref, ext_ref, hd_scr, *, nb):
    seq = feat_ref.shape[0]
    tb = seq // nb

    @pl.when(pl.program_id(0) == 0)
    def _():
        h1 = jnp.sin(f1_ref[...] * (jnp.dot(feat_ref[...], w1_ref[...], precision=HIGHEST,
                                            preferred_element_type=F32) + b1_ref[...]))
        hd_scr[...] = jnp.sin(f2_ref[...] * (jnp.dot(h1, w2_ref[...], precision=HIGHEST,
                                                     preferred_element_type=F32) + b2_ref[...]))

    hd = hd_scr[...]
    row = lax.broadcasted_iota(I32, (seq, 1), 0)
    t = row.astype(F32) / float(max(seq - 1, 1))
    window = jnp.exp(-t * dl_ref[...])
    hf = (jnp.dot(hd, w3f_ref[...], precision=HIGHEST, preferred_element_type=F32) + b3f_ref[...]) * window
    hb = (jnp.dot(hd, w3b_ref[...], precision=HIGHEST, preferred_element_type=F32) + b3b_ref[...]) * window
    hb = jnp.where(row == 0, 0.0, hb)
    den = jnp.sum(jnp.abs(hf) + jnp.abs(hb), axis=0, keepdims=True) + EPS
    hf = hf / den
    hb = hb / den
    hf_ref[...] = hf.astype(BF16)
    hb_ref[...] = hb.astype(BF16)
    sign = (1 - 2 * (row & 1)).astype(F32)
    for j in range(nb):
        sl = slice(j * tb, (j + 1) * tb)
        ext_ref[0, j:j + 1, :] = hf[j * tb:j * tb + 1].astype(BF16).astype(F32)
        ext_ref[1, j:j + 1, :] = hb[j * tb:j * tb + 1].astype(BF16).astype(F32)
        ext_ref[2, j:j + 1, :] = jnp.sum(hf[sl] * sign[sl], axis=0, keepdims=True)
        ext_ref[3, j:j + 1, :] = jnp.sum(hb[sl] * sign[sl], axis=0, keepdims=True)


def _filter_td(seq, p, nb):
    pos = np.arange(seq, dtype=np.float64)
    t = pos / max(seq - 1, 1)
    bands = np.linspace(1e-4, HY_BANDS - 1, HY_BANDS)
    w = 2.0 * math.pi * pos / seq
    feats = np.concatenate([t[:, None], np.cos(w[:, None] * bands), -np.sin(w[:, None] * bands)], axis=-1)
    feats = jnp.asarray(np.pad(feats, ((0, 0), (0, LANES - HY_EMB))), F32)
    w1 = jnp.pad(p['hy_filt_w1'], ((0, LANES - HY_EMB), (0, 0)))
    deltas = jnp.asarray(np.abs(np.linspace(math.log(HY_TARGET) / HY_SLOW_DECAY, math.log(HY_TARGET) / HY_FAST_DECAY,
                                            HY_WIDTH)).reshape(1, HY_WIDTH), F32)
    cb = 256
    ncb = HY_WIDTH // cb
    row = lambda v: v.reshape(1, -1)
    c2 = lambda j: (0, 0)
    return pl.pallas_call(
        functools.partial(_filter_kernel, nb=nb),
        grid=(ncb,),
        in_specs=[pl.BlockSpec((seq, LANES), c2),
                  pl.BlockSpec((LANES, HY_HIDDEN), c2), pl.BlockSpec((1, HY_HIDDEN), c2),
                  pl.BlockSpec((1, HY_HIDDEN), c2),
                  pl.BlockSpec((HY_HIDDEN, HY_HIDDEN), c2), pl.BlockSpec((1, HY_HIDDEN), c2),
                  pl.BlockSpec((1, HY_HIDDEN), c2),
                  pl.BlockSpec((HY_HIDDEN, cb), lambda j: (0, j)),
                  pl.BlockSpec((HY_HIDDEN, cb), lambda j: (0, j + ncb)),
                  pl.BlockSpec((1, cb), lambda j: (0, j)),
                  pl.BlockSpec((1, cb), lambda j: (0, j + ncb)),
                  pl.BlockSpec((1, cb), lambda j: (0, j))],
        out_specs=[pl.BlockSpec((seq, cb), lambda j: (0, j)),
                   pl.BlockSpec((seq, cb), lambda j: (0, j)),
                   pl.BlockSpec((4, nb, cb), lambda j: (0, 0, j))],
        out_shape=[jax.ShapeDtypeStruct((seq, HY_WIDTH), BF16),
                   jax.ShapeDtypeStruct((seq, HY_WIDTH), BF16),
                   jax.ShapeDtypeStruct((4, nb, HY_WIDTH), F32)],
        scratch_shapes=[pltpu.VMEM((seq, HY_HIDDEN), F32)],
        compiler_params=_cparams(("arbitrary",)),
        name="hyena_filter",
    )(feats, w1, row(p['hy_filt_b1']), row(p['hy_filt_freq1']),
      p['hy_filt_w2'], row(p['hy_filt_b2']), row(p['hy_filt_freq2']),
      p['hy_filt_w3'], p['hy_filt_w3'], row(p['hy_filt_b3']), row(p['hy_filt_b3']), deltas)


def _kspec_kernel(c_ref, s_ref, hf_ref, hb_ref, ext_ref, kc_ref, ks_ref, kn_ref, *, nb):
    tb = c_ref.shape[0]
    f = lax.broadcasted_iota(I32, (tb, 1), 0)
    wgt = jnp.where(f == 0, 1.0, 2.0) / float(2 * tb)
    sig = (1 - 2 * (f & 1)).astype(F32)
    cm = c_ref[...]
    sm = s_ref[...]
    blk = lambda ref, j: ref[j * tb:(j + 1) * tb, :]
    chf = [jnp.dot(cm, blk(hf_ref, j), preferred_element_type=F32) for j in range(nb)]
    shf = [jnp.dot(sm, blk(hf_ref, j), preferred_element_type=F32) for j in range(nb)]
    chb = [jnp.dot(cm, blk(hb_ref, j), preferred_element_type=F32) for j in range(nb)]
    shb = [jnp.dot(sm, blk(hb_ref, j), preferred_element_type=F32) for j in range(nb)]
    first_f = lambda j: ext_ref[0, j:j + 1, :]
    first_b = lambda j: ext_ref[1, j:j + 1, :]
    alt_f = lambda j: ext_ref[2, j:j + 1, :]
    alt_b = lambda j: ext_ref[3, j:j + 1, :]
    for d in range(-(nb - 1), nb):
        k = d + nb - 1
        if d == 0:
            kc, ks, kn = chf[0] + chb[0], shf[0] - shb[0], alt_f(0) + alt_b(0)
        elif d > 0:
            kc = chf[d] + sig * (chf[d - 1] - first_f(d - 1))
            ks = shf[d] + sig * shf[d - 1]
            kn = alt_f(d) + alt_f(d - 1) - first_f(d - 1)
        else:
            e = -d
            kc = chb[e] + sig * (chb[e - 1] - first_b(e - 1))
            ks = -(shb[e] + sig * shb[e - 1])
            kn = alt_b(e) + alt_b(e - 1) - first_b(e - 1)
        kc_ref[k] = kc * wgt
        ks_ref[k] = ks * wgt
        kn_ref[k:k + 1, :] = kn * (1.0 / float(2 * tb))


def _kspec(cmat, smat, hf, hb, ext, nb):
    tb = cmat.shape[0]
    seq = hf.shape[0]
    nd = 2 * nb - 1
    cb = 256
    return pl.pallas_call(
        functools.partial(_kspec_kernel, nb=nb),
        grid=(HY_WIDTH // cb,),
        in_specs=[pl.BlockSpec((tb, tb), lambda j: (0, 0)),
                  pl.BlockSpec((tb, tb), lambda j: (0, 0)),
                  pl.BlockSpec((seq, cb), lambda j: (0, j)),
                  pl.BlockSpec((seq, cb), lambda j: (0, j)),
                  pl.BlockSpec((4, nb, cb), lambda j: (0, 0, j))],
        out_specs=[pl.BlockSpec((nd, tb, cb), lambda j: (0, 0, j)),
                   pl.BlockSpec((nd, tb, cb), lambda j: (0, 0, j)),
                   pl.BlockSpec((nd, cb), lambda j: (0, j))],
        out_shape=[jax.ShapeDtypeStruct((nd, tb, HY_WIDTH), F32),
                   jax.ShapeDtypeStruct((nd, tb, HY_WIDTH), F32),
                   jax.ShapeDtypeStruct((nd, HY_WIDTH), F32)],
        compiler_params=_cparams(("arbitrary",)),
        name="hyena_kspec",
    )(cmat, smat, hf, hb, ext)


def _fwd_kernel(c_ref, s_ref, z_ref, kc_ref, ks_ref, pc_ref, ps_ref, *, nb):
    tb = c_ref.shape[0]
    cm = c_ref[...]
    sm = s_ref[...]
    zc = [jnp.dot(cm, z_ref[0, j * tb:(j + 1) * tb, :], preferred_element_type=F32) for j in range(nb)]
    zs = [jnp.dot(sm, z_ref[0, j * tb:(j + 1) * tb, :], preferred_element_type=F32) for j in range(nb)]
    for i in range(nb):
        pc = None
        ps = None
        for j in range(nb):
            kc = kc_ref[i - j + nb - 1]
            ks = ks_ref[i - j + nb - 1]
            tc = zc[j] * kc - zs[j] * ks
            ts = zc[j] * ks + zs[j] * kc
            pc = tc if pc is None else pc + tc
            ps = ts if ps is None else ps + ts
        pc_ref[0, i * tb:(i + 1) * tb, :] = pc.astype(BF16)
        ps_ref[0, i * tb:(i + 1) * tb, :] = ps.astype(BF16)


def _hy_fwd(cmat, smat, z, kc, ks, nb):
    b, seq, _ = z.shape
    tb = seq // nb
    nd = 2 * nb - 1
    cb = 256
    return pl.pallas_call(
        functools.partial(_fwd_kernel, nb=nb),
        grid=(HY_WIDTH // cb, b),
        in_specs=[pl.BlockSpec((tb, tb), lambda j, bi: (0, 0)),
                  pl.BlockSpec((tb, tb), lambda j, bi: (0, 0)),
                  pl.BlockSpec((1, seq, cb), lambda j, bi: (bi, 0, j)),
                  pl.BlockSpec((nd, tb, cb), lambda j, bi: (0, 0, j)),
                  pl.BlockSpec((nd, tb, cb), lambda j, bi: (0, 0, j))],
        out_specs=[pl.BlockSpec((1, seq, cb), lambda j, bi: (bi, 0, j)),
                   pl.BlockSpec((1, seq, cb), lambda j, bi: (bi, 0, j))],
        out_shape=[jax.ShapeDtypeStruct((b, seq, HY_WIDTH), BF16),
                   jax.ShapeDtypeStruct((b, seq, HY_WIDTH), BF16)],
        compiler_params=_cparams(("arbitrary", "arbitrary")),
        name="hyena_fwd",
    )(cmat, smat, z, kc, ks)


def _inv_kernel(c_ref, s_ref, pc_ref, ps_ref, z_ref, x0_ref, zn_ref, kn_ref, skip_ref, o_ref, *, nb):
    tb = c_ref.shape[0]
    cm = c_ref[...]
    sm = s_ref[...]
    t = lax.broadcasted_iota(I32, (tb, 1), 0)
    sign = (1 - 2 * (t & 1)).astype(F32)
    for i in range(nb):
        rs = slice(i * tb, (i + 1) * tb)
        y = jnp.dot(cm, pc_ref[0, rs, :], preferred_element_type=F32)
        y = y + jnp.dot(sm, ps_ref[0, rs, :], preferred_element_type=F32)
        nyq = None
        for j in range(nb):
            k = i - j + nb - 1
            term = zn_ref[0, j:j + 1, :] * kn_ref[k:k + 1, :]
            nyq = term if nyq is None else nyq + term
        y = y + sign * nyq + z_ref[0, rs, :].astype(F32) * skip_ref[...]
        o_ref[0, rs, :] = (x0_ref[0, rs, :].astype(F32) * y).astype(BF16)


def _hy_inv(cmat, smat, pc, ps, z, x0, zn, kn, skip, nb):
    b, seq, _ = z.shape
    tb = seq // nb
    nd = 2 * nb - 1
    cb = 256
    blk = pl.BlockSpec((1, seq, cb), lambda j, bi: (bi, 0, j))
    return pl.pallas_call(
        functools.partial(_inv_kernel, nb=nb),
        grid=(HY_WIDTH // cb, b),
        in_specs=[pl.BlockSpec((tb, tb), lambda j, bi: (0, 0)),
                  pl.BlockSpec((tb, tb), lambda j, bi: (0, 0)),
                  blk, blk, blk, blk,
                  pl.BlockSpec((1, nb, cb), lambda j, bi: (bi, 0, j)),
                  pl.BlockSpec((nd, cb), lambda j, bi: (0, j)),
                  pl.BlockSpec((1, cb), lambda j, bi: (0, j))],
        out_specs=blk,
        out_shape=jax.ShapeDtypeStruct((b, seq, HY_WIDTH), BF16),
        compiler_params=_cparams(("arbitrary", "arbitrary")),
        name="hyena_inv",
    )(cmat, smat, pc, ps, z, x0, zn, kn, skip)


def _layer_norm(v, g, b):
    mu = jnp.mean(v, axis=-1, keepdims=True)
    vc = v - mu
    var = jnp.mean(vc * vc, axis=-1, keepdims=True)
    return vc * lax.rsqrt(var + EPS) * g + b


def _merge_kernel(at_ref, hy_ref, ga_ref, gb_ref, woa_ref, woh_ref, o_ref):
    a = jnp.dot(at_ref[...], woa_ref[...], preferred_element_type=F32)
    hh = jnp.dot(hy_ref[...], woh_ref[...], preferred_element_type=F32)
    merged = _sigmoid(ga_ref[...].astype(F32)) * a + _sigmoid(gb_ref[...].astype(F32)) * hh
    o_ref[...] = merged.astype(BF16)


def _merge(attn, hy, big, woa, woh):
    n = attn.shape[0]
    d = woa.shape[1]
    tm = 512
    return pl.pallas_call(
        _merge_kernel,
        grid=(n // tm,),
        in_specs=[pl.BlockSpec((tm, attn.shape[1]), lambda i: (i, 0)),
                  pl.BlockSpec((tm, hy.shape[1]), lambda i: (i, 0)),
                  pl.BlockSpec((tm, d), lambda i: (i, 0)),
                  pl.BlockSpec((tm, d), lambda i: (i, 1)),
                  _const_spec(woa.shape), _const_spec(woh.shape)],
        out_specs=pl.BlockSpec((tm, d), lambda i: (i, 0)),
        out_shape=jax.ShapeDtypeStruct((n, d), BF16),
        compiler_params=_cparams(("arbitrary",)),
        name="merge",
    )(attn, hy, big, big, woa, woh)


def _ln1_kernel(x_ref, mg_ref, mod_ref, wout_ref, g_ref, b_ref, wr_ref, x1_ref, h2_ref, lg_ref, *, n_sub):
    m = mod_ref[0]
    rows = x_ref.shape[0] // n_sub
    for t in range(n_sub):
        rs = slice(t * rows, (t + 1) * rows)
        mix = jnp.dot(mg_ref[rs, :], wout_ref[...], preferred_element_type=F32)
        x1 = _layer_norm(ALPHA * x_ref[rs, :] + m[2:3] * mix, g_ref[...], b_ref[...])
        x1_ref[rs, :] = x1
        h2 = x1 * (1.0 + m[4:5]) + m[3:4]
        h2_ref[rs, :] = h2
        hi = h2.astype(BF16)
        lo = (h2 - hi.astype(F32)).astype(BF16)
        lg = (jnp.dot(hi, wr_ref[...], preferred_element_type=F32)
              + jnp.dot(lo, wr_ref[...], preferred_element_type=F32))
        lg_ref[rs, :] = lg + pltpu.roll(lg, LANES - N_EXPERTS, axis=1)


def _ln1(x, merged, mods, mod_row, wout, g, b, wr_pack):
    n, d = x.shape
    tm = 512
    return pl.pallas_call(
        functools.partial(_ln1_kernel, n_sub=2),
        grid=(n // tm,),
        in_specs=[pl.BlockSpec((tm, d), lambda i: (i, 0)),
                  pl.BlockSpec((tm, d), lambda i: (i, 0)),
                  pl.BlockSpec((1, N_MOD, d), lambda i: (mod_row(i * tm), 0, 0)),
                  _const_spec(wout.shape), _const_spec((1, d)), _const_spec((1, d)),
                  _const_spec(wr_pack.shape)],
        out_specs=[pl.BlockSpec((tm, d), lambda i: (i, 0)),
                   pl.BlockSpec((tm, d), lambda i: (i, 0)),
                   pl.BlockSpec((tm, LANES), lambda i: (i, 0))],
        out_shape=[jax.ShapeDtypeStruct((n, d), F32),
                   jax.ShapeDtypeStruct((n, d), F32),
                   jax.ShapeDtypeStruct((n, LANES), F32)],
        compiler_params=_cparams(("arbitrary",)),
        name="ln1_router",
    )(x, merged, mods, wout, g, b, wr_pack)


def _split3(t):
    t1 = t.astype(BF16)
    r = t - t1.astype(F32)
    t2 = r.astype(BF16)
    t3 = (r - t2.astype(F32)).astype(BF16)
    return t1, t2, t3


def _dot_exact(sel01, table):
    t1, t2, t3 = _split3(table)
    out = jnp.dot(sel01, t1, preferred_element_type=F32)
    out = out + jnp.dot(sel01, t2, preferred_element_type=F32)
    return out + jnp.dot(sel01, t3, preferred_element_type=F32)


def _route_kernel(lg_ref, idx_ref, q_ref, gate_ref, start_ref, aff_scr, sel_scr, pos_scr, *, cap):
    ne, nc, _ = lg_ref.shape
    lg = lg_ref[...]
    mx = jnp.max(lg, axis=0, keepdims=True)
    ex = jnp.exp(lg - mx)
    aff = ex / jnp.sum(ex, axis=0, keepdims=True)
    aff_scr[...] = aff

    def tbody(k, thr):
        cand = thr | lax.shift_left(jnp.int32(1), 30 - k)
        ge = jnp.where(aff >= lax.bitcast_convert_type(cand, F32), 1.0, 0.0)
        cnt = jnp.sum(jnp.sum(ge, axis=1, keepdims=True), axis=2, keepdims=True)
        return jnp.where(cnt >= float(cap), cand, thr)

    thr = lax.fori_loop(0, 31, tbody, jnp.zeros((ne, 1, 1), I32))
    lo = lax.bitcast_convert_type(thr, F32)
    above = aff >= lax.bitcast_convert_type(thr + 1, F32)
    sel_scr[...] = jnp.where(above, 1.0, 0.0)
    pos_scr[...] = jnp.where((aff >= lo) & jnp.logical_not(above), aff, -1.0)

    r_i = lax.broadcasted_iota(I32, (LANES, LANES), 0)
    c_i = lax.broadcasted_iota(I32, (LANES, LANES), 1)
    upper_incl = jnp.where(r_i <= c_i, 1.0, 0.0).astype(BF16)
    ones_sq = jnp.ones((LANES, LANES), BF16)
    rr_i = lax.broadcasted_iota(I32, (nc, nc), 0)
    cc_i = lax.broadcasted_iota(I32, (nc, nc), 1)
    lower_strict = jnp.where(cc_i < rr_i, 1.0, 0.0).astype(BF16)
    upper_rows = jnp.where(rr_i <= cc_i, 1.0, 0.0).astype(BF16)

    def prefix(x01):
        xb = x01.astype(BF16)
        p1 = jnp.dot(xb, upper_incl, preferred_element_type=F32)
        totb = jnp.dot(xb, ones_sq, preferred_element_type=F32)
        offs = jnp.dot(lower_strict, totb.astype(BF16), preferred_element_type=F32)
        return p1, offs

    tok = (lax.broadcasted_iota(I32, (nc, LANES), 0) * LANES
           + lax.broadcasted_iota(I32, (nc, LANES), 1)).astype(F32)

    def reduce2(fn, v):
        return fn(fn(v, axis=0, keepdims=True), axis=1, keepdims=True)

    def pass1(e, start):
        above_e = sel_scr[e]
        need = cap - jnp.sum(above_e).astype(I32)

        def pick(_, carry):
            vals, picked = carry
            first = reduce2(jnp.min, jnp.where(vals == reduce2(jnp.max, vals), tok, float(nc * LANES)))
            hit = tok == first
            return jnp.where(hit, -1.0, vals), picked + jnp.where(hit, 1.0, 0.0)

        _, picked = lax.fori_loop(0, need, pick, (pos_scr[e], jnp.zeros((nc, LANES), F32)))
        sel = above_e + picked
        sel_scr[e] = sel
        s1, soffs = prefix(sel)
        return start + (s1 + soffs - sel)

    start = lax.fori_loop(0, ne, pass1, jnp.zeros((nc, LANES), F32))
    start_ref[...] = start.astype(I32)

    p_col = lax.broadcasted_iota(I32, (cap, 1), 0).astype(F32)
    lane_row = lax.broadcasted_iota(I32, (1, LANES), 1).astype(F32)
    chunk_row = lax.broadcasted_iota(I32, (1, nc), 1).astype(F32)
    ones8 = jnp.ones((8, LANES), BF16)

    def row_sums(m):
        ones = jnp.ones((8, m.shape[1]), BF16)
        nt = lambda part: lax.dot_general(ones, part, (((1,), (1,)), ((), ())), preferred_element_type=F32)
        m1, m2, m3 = _split3(m)
        return (nt(m1) + nt(m2) + nt(m3))[0:1]

    def pass2(e, before):
        sel = sel_scr[e]
        selb = sel.astype(BF16)
        p1, offs = prefix(sel)
        tot_row = lax.dot_general(ones8, selb, (((1,), (1,)), ((), ())), preferred_element_type=F32)
        cum_row = jnp.dot(tot_row.astype(BF16), upper_rows, preferred_element_type=F32)
        cum1 = cum_row[0:1]
        prev1 = cum1 - tot_row[0:1]
        in_chunk = (prev1 <= p_col) & (p_col < cum1)
        ohc = jnp.where(in_chunk, 1.0, 0.0).astype(BF16)
        keyg = jnp.dot(ohc, (p1 * sel).astype(BF16), preferred_element_type=F32)
        offg = _dot_exact(ohc, offs)
        ohl = keyg == (p_col - offg + 1.0)
        l_p = row_sums(jnp.where(ohl, lane_row, 0.0))
        c_p = row_sums(jnp.where(in_chunk, chunk_row, 0.0))
        idx_ref[e] = (c_p * float(LANES) + l_p).astype(I32)
        qtab = _dot_exact(ohc, start) + jnp.dot(ohc, before.astype(BF16), preferred_element_type=F32)
        q_ref[e] = row_sums(jnp.where(ohl, qtab, 0.0)).astype(I32)
        afg = _dot_exact(ohc, aff_scr[e])
        gate_ref[e] = row_sums(jnp.where(ohl, afg, 0.0))
        return before + sel

    lax.fori_loop(0, ne, pass2, jnp.zeros((nc, LANES), F32))


def _route(lg3, cap):
    ne, nc, _ = lg3.shape
    full = lambda s: pl.BlockSpec(s, lambda: (0,) * len(s))
    return pl.pallas_call(
        functools.partial(_route_kernel, cap=cap),
        in_specs=[full(lg3.shape)],
        out_specs=[full((ne, 1, cap)), full((ne, 1, cap)), full((ne, 1, cap)), full((nc, LANES))],
        out_shape=[jax.ShapeDtypeStruct((ne, 1, cap), I32),
                   jax.ShapeDtypeStruct((ne, 1, cap), I32),
                   jax.ShapeDtypeStruct((ne, 1, cap), F32),
                   jax.ShapeDtypeStruct((nc, LANES), I32)],
        scratch_shapes=[pltpu.VMEM((ne, nc, LANES), F32),
                        pltpu.VMEM((ne, nc, LANES), F32),
                        pltpu.VMEM((ne, nc, LANES), F32)],
        compiler_params=pltpu.CompilerParams(vmem_limit_bytes=VMEM_LIMIT),
        name="route",
    )(lg3)


def _ffn_kernel(idx_ref, q_ref, gate_ref, wg_ref, wu_ref, wd_ref, h_hbm, y_hbm,
                xb, xa, yb, ya, gsem, ssem, *, tm, npair, total):
    step = pl.program_id(0) * npair + pl.program_id(1)
    t0 = 2 * step
    t1 = t0 + 1

    def gather_row(base, r, buf, s):
        return pltpu.make_async_copy(h_hbm.at[pl.ds(idx_ref[base + r], 1), :], buf.at[pl.ds(r, 1), :], gsem.at[s])

    def scatter_row(base, r, buf, s):
        return pltpu.make_async_copy(buf.at[pl.ds(r, 1), :], y_hbm.at[pl.ds(q_ref[base + r], 1), :], ssem.at[s])

    def wait_gather(buf, s):
        pltpu.make_async_copy(h_hbm.at[pl.ds(0, tm), :], buf, gsem.at[s]).wait()

    def wait_scatter(buf, s):
        pltpu.make_async_copy(buf, y_hbm.at[pl.ds(0, tm), :], ssem.at[s]).wait()

    n_chunk = 4
    fc = EXPERT_FF // n_chunk
    per = tm // n_chunk

    def compute(x_buf, y_buf, r0, gather_base, gather_buf, scatter_base, scatter_buf, s):
        x = x_buf[...].astype(BF16)
        gate = gate_ref[r0:r0 + tm, :]
        for c in range(n_chunk):
            for r in range(c * per, (c + 1) * per):
                gather_row(gather_base, r, gather_buf, s).start()
            for r in range(c * per, (c + 1) * per):
                scatter_row(scatter_base, r, scatter_buf, s).start()
            g = jnp.dot(x, wg_ref[0, :, c * fc:(c + 1) * fc], preferred_element_type=F32)
            u = jnp.dot(x, wu_ref[0, :, c * fc:(c + 1) * fc], preferred_element_type=F32)
            hid = (g * _sigmoid(g) * u * gate).astype(BF16)
            part = jnp.dot(hid, wd_ref[0, c * fc:(c + 1) * fc, :], preferred_element_type=F32)
            if c == 0:
                y_buf[...] = part
            else:
                y_buf[...] += part

    @pl.when(step == 0)
    def _():
        yb[...] = jnp.zeros(yb.shape, F32)

        def body(r, carry):
            gather_row(0, r, xa, 0).start()
            return carry

        lax.fori_loop(0, tm, body, 0)

    wait_gather(xa, 0)

    @pl.when(step >= 1)
    def _():
        wait_scatter(ya, 0)

    base1 = t1 * tm
    pbase = jnp.maximum(t0 - 1, 0) * tm
    compute(xa, ya, 0, base1, xb, pbase, yb, 1)

    wait_gather(xb, 1)
    wait_scatter(yb, 1)
    nbase = jnp.minimum(t1 + 1, total - 1) * tm
    base0 = t0 * tm
    compute(xb, yb, tm, nbase, xa, base0, ya, 0)

    @pl.when(t1 == total - 1)
    def _():
        wait_gather(xa, 0)
        wait_scatter(ya, 0)

        def body(r, carry):
            scatter_row(base1, r, yb, 1).start()
            return carry

        lax.fori_loop(0, tm, body, 0)
        wait_scatter(yb, 1)


def _ffn(h2, idx, qpos, gate, wg, wu, wd):
    n, d = h2.shape
    ne = wg.shape[0]
    cap = idx.shape[0] // ne
    tm = min(512, cap // 2)
    npair = cap // (2 * tm)
    total = 2 * ne * npair
    grid_spec = pltpu.PrefetchScalarGridSpec(
        num_scalar_prefetch=2,
        grid=(ne, npair),
        in_specs=[pl.BlockSpec((2 * tm, 1), lambda e, i, *_: (e * npair + i, 0)),
                  pl.BlockSpec((1, d, EXPERT_FF), lambda e, i, *_: (e, 0, 0)),
                  pl.BlockSpec((1, d, EXPERT_FF), lambda e, i, *_: (e, 0, 0)),
                  pl.BlockSpec((1, EXPERT_FF, d), lambda e, i, *_: (e, 0, 0)),
                  pl.BlockSpec(memory_space=pl.ANY)],
        out_specs=pl.BlockSpec(memory_space=pl.ANY),
        scratch_shapes=[pltpu.VMEM((tm, d), F32), pltpu.VMEM((tm, d), F32),
                        pltpu.VMEM((tm, d), F32), pltpu.VMEM((tm, d), F32),
                        pltpu.SemaphoreType.DMA((2,)),
                        pltpu.SemaphoreType.DMA((2,))],
    )
    return pl.pallas_call(
        functools.partial(_ffn_kernel, tm=tm, npair=npair, total=total),
        grid_spec=grid_spec,
        out_shape=jax.ShapeDtypeStruct((ne * cap, d), F32),
        compiler_params=_cparams(("arbitrary", "arbitrary")),
        name="expert_ffn",
    )(idx, qpos, gate, wg, wu, wd, h2)


def _combine_kernel(cs_ref, x1_ref, mod_ref, st_ref, en_ref, g_ref, b_ref, y_hbm, o_ref,
                    ybuf, acc, sem, *, win, total, nsteps):
    i = pl.program_id(0)
    slot = i % 2
    lo0 = cs_ref[i]
    hi = cs_ref[i + 1]
    adv = win - 8

    def win_start(lo):
        return pl.multiple_of((jnp.minimum(lo, total - win) // 8) * 8, 8)

    def window(lo, sl):
        return pltpu.make_async_copy(y_hbm.at[pl.ds(win_start(lo), win), :], ybuf.at[sl], sem.at[sl])

    @pl.when(i == 0)
    def _():
        window(lo0, 0).start()

    @pl.when(i + 1 < nsteps)
    def _():
        window(hi, 1 - slot).start()

    ntrip = jnp.maximum((hi - lo0 + adv - 1) // adv, 1)
    acc[...] = jnp.zeros_like(acc)
    st = st_ref[...]
    en = en_ref[...]

    def body(k, carry):
        lo = lo0 + k * adv

        @pl.when(k > 0)
        def _():
            window(lo, slot).start()

        window(lo, slot).wait()
        qj = win_start(lo) + lax.broadcasted_iota(I32, (1, win), 1)
        seg = (qj >= st) & (qj < en) & (qj >= lo) & (qj < lo + adv)
        acc[...] += jnp.dot(jnp.where(seg, 1.0, 0.0).astype(BF16), ybuf[slot].astype(BF16),
                            preferred_element_type=F32)
        return carry

    lax.fori_loop(0, ntrip, body, 0)
    m = mod_ref[0]
    o_ref[...] = _layer_norm(ALPHA * x1_ref[...] + m[5:6] * acc[...], g_ref[...], b_ref[...])


def _combine(x1, mods, mod_row, start, pairs, g, b):
    n, d = x1.shape
    total = pairs.shape[0]
    tc = 256
    win = 640
    st = start.reshape(n, 1)
    en = jnp.concatenate([start[1:], jnp.full((1,), total, I32)]).reshape(n, 1)
    cs = jnp.concatenate([start[::tc], jnp.full((1,), total, I32)])
    grid_spec = pltpu.PrefetchScalarGridSpec(
        num_scalar_prefetch=1,
        grid=(n // tc,),
        in_specs=[pl.BlockSpec((tc, d), lambda i, *_: (i, 0)),
                  pl.BlockSpec((1, N_MOD, d), lambda i, *_: (mod_row(i * tc), 0, 0)),
                  pl.BlockSpec((tc, 1), lambda i, *_: (i, 0)),
                  pl.BlockSpec((tc, 1), lambda i, *_: (i, 0)),
                  pl.BlockSpec((1, d), lambda i, *_: (0, 0)),
                  pl.BlockSpec((1, d), lambda i, *_: (0, 0)),
                  pl.BlockSpec(memory_space=pl.ANY)],
        out_specs=pl.BlockSpec((tc, d), lambda i, *_: (i, 0)),
        scratch_shapes=[pltpu.VMEM((2, win, d), F32),
                        pltpu.VMEM((tc, d), F32),
                        pltpu.SemaphoreType.DMA((2,))],
    )
    return pl.pallas_call(
        functools.partial(_combine_kernel, win=win, total=total, nsteps=n // tc),
        grid_spec=grid_spec,
        out_shape=jax.ShapeDtypeStruct((n, d), F32),
        compiler_params=_cparams(("arbitrary",)),
        name="combine_ln2",
    )(cs, x1, mods, st, en, g, b, pairs)


def _swap_pairs(w):
    f = ROPE_FREQS
    return jnp.concatenate([w[..., f:2 * f], w[..., 0:f], w[..., 3 * f:4 * f], w[..., 2 * f:3 * f]], axis=-1)


def _rope_tables(seq):
    n_rows = seq // GRID_W
    row = np.repeat(np.arange(n_rows, dtype=np.float64), GRID_W)
    col = np.tile(np.arange(GRID_W, dtype=np.float64), n_rows)
    inv = ROPE_THETA ** (-np.arange(ROPE_FREQS, dtype=np.float64) / ROPE_FREQS)
    ar = row[:, None] * inv
    ac = col[:, None] * inv
    z = np.zeros((seq, LANES - QK_ROPE))
    cos128 = np.concatenate([np.cos(ar), np.cos(ar), np.cos(ac), np.cos(ac), z], axis=1)
    sin128 = np.concatenate([-np.sin(ar), np.sin(ar), -np.sin(ac), np.sin(ac), z], axis=1)
    return jnp.asarray(cos128, F32), jnp.asarray(sin128, F32)


def _dft_tables(seq):
    m = 2 * seq
    blk = 64
    s = np.arange(seq, dtype=np.int64)
    ang1 = ((np.arange(0, seq, blk, dtype=np.int64)[:, None] * s[None, :]) % m) * (2.0 * math.pi / m)
    ang0 = ((np.arange(blk, dtype=np.int64)[:, None] * s[None, :]) % m) * (2.0 * math.pi / m)
    c1, s1 = jnp.asarray(np.cos(ang1), F32)[:, None, :], jnp.asarray(np.sin(ang1), F32)[:, None, :]
    c0, s0 = jnp.asarray(np.cos(ang0), F32)[None, :, :], jnp.asarray(np.sin(ang0), F32)[None, :, :]
    cmat = (c1 * c0 - s1 * s0).reshape(seq, seq).astype(BF16)
    smat = (s1 * c0 + c1 * s0).reshape(seq, seq).astype(BF16)
    return cmat, smat


def _prep_weights(p):
    w = {}
    w_in = p['w_in']
    s0, s1, s2, s3, s4 = (Q_LORA, Q_LORA + KV_LORA, Q_LORA + KV_LORA + QK_ROPE,
                          Q_LORA + KV_LORA + QK_ROPE + 3 * HY_WIDTH,
                          Q_LORA + KV_LORA + QK_ROPE + 3 * HY_WIDTH + D_MODEL)
    w_in = w_in.astype(BF16)
    kr = w_in[:, s1:s2]
    z64 = jnp.zeros((D_MODEL, LANES - QK_ROPE), BF16)
    w['w_in'] = jnp.concatenate([w_in[:, :s1], kr, z64, _swap_pairs(kr), z64,
                                 w_in[:, s3:], w_in[:, s2:s3]], axis=1)
    scale = (QK_NOPE + QK_ROPE) ** -0.5 * math.log2(math.e)
    wq = (p['w_uq'] * scale).reshape(Q_LORA, MLA_HEADS, QK_NOPE + QK_ROPE)
    zq = jnp.zeros((Q_LORA, MLA_HEADS, HEAD_PAD - QK_NOPE - QK_ROPE), F32)
    w['wa'] = jnp.concatenate([wq, zq], axis=-1).reshape(Q_LORA, MLA_HEADS * HEAD_PAD).astype(BF16)
    w['wb'] = jnp.concatenate([_swap_pairs(wq[..., QK_NOPE:]), zq], axis=-1).reshape(
        Q_LORA, MLA_HEADS * LANES).astype(BF16)
    wkv = p['w_ukv'].reshape(KV_LORA, MLA_HEADS, QK_NOPE + V_HEAD)
    w['wk'] = wkv[..., :QK_NOPE].reshape(KV_LORA, MLA_HEADS * QK_NOPE).astype(BF16)
    w['wv'] = wkv[..., QK_NOPE:].reshape(KV_LORA, MLA_HEADS * V_HEAD).astype(BF16)
    w['woa'] = p['w_o_mla'].astype(BF16)
    w['woh'] = p['w_o_hy'].astype(BF16)
    w['wout'] = p['w_out'].astype(BF16)
    wr_hi = p['w_router'].astype(BF16)
    wr_lo = (p['w_router'] - wr_hi.astype(F32)).astype(BF16)
    w['wr_pack'] = jnp.pad(jnp.concatenate([wr_hi, wr_lo], axis=1), ((0, 0), (0, LANES - 2 * N_EXPERTS)))
    w['wg'] = p['w_gate'].astype(BF16)
    w['wu'] = p['w_up'].astype(BF16)
    w['wd'] = p['w_down'].astype(BF16)
    w['hy_w'] = p['hy_short_w'].reshape(3, 3, HY_WIDTH).transpose(1, 0, 2)
    w['hy_b'] = p['hy_short_b'].reshape(3, 1, HY_WIDTH)
    return w


def _trunk(x3, mods, mod_row, tm_in, p, w, cache_kv, cache_kr, rope):
    b, seq, d = x3.shape
    n = b * seq
    x = x3.reshape(n, d)
    small, big = _inproj(x, mods, mod_row, w['w_in'], tm_in)
    cos128, sin128 = _rope_tables(seq) if rope else (None, None)
    q, ckv, kr = _qprep(small, p['q_norm_g'].reshape(1, -1), p['kv_norm_g'].reshape(1, -1),
                        w['wa'], w['wb'], cos128, sin128, seq, rope)
    ckv3 = ckv.reshape(b, seq, KV_LORA)
    kr3 = kr.reshape(b, seq, LANES)
    if cache_kv is None:
        kv_all, kr_all = ckv3, kr3
    else:
        kv_all = jnp.concatenate([cache_kv, ckv3], axis=1)
        kr_all = jnp.concatenate([jnp.pad(cache_kr, ((0, 0), (0, 0), (0, LANES - QK_ROPE))), kr3], axis=1)
    kh, vh = _kvup(kv_all, kr_all, w['wk'], w['wv'])
    attn = _attention(q.reshape(b, seq, MLA_HEADS * HEAD_PAD), kh, vh).reshape(n, MLA_HEADS * V_HEAD)

    nb = HY_TIME_BLOCKS if seq >= 512 else 1
    x0c, z, zn = _hypre(big.reshape(b, seq, BIG_W), w['hy_w'], w['hy_b'], nb)
    hf_td, hb_td, ext = _filter_td(seq, p, nb)
    cmat, smat = _dft_tables(seq // nb)
    kc, ks, kn = _kspec(cmat, smat, hf_td, hb_td, ext, nb)
    pc, ps = _hy_fwd(cmat, smat, z, kc, ks, nb)
    hy = _hy_inv(cmat, smat, pc, ps, z, x0c, zn, kn, p['hy_skip'].reshape(1, -1), nb).reshape(n, HY_WIDTH)

    merged = _merge(attn, hy, big, w['woa'], w['woh'])
    x1, h2, lg = _ln1(x, merged, mods, mod_row, w['wout'], p['ln1_g'].reshape(1, -1),
                      p['ln1_b'].reshape(1, -1), w['wr_pack'])
    cap = EC_CAPACITY * n // N_EXPERTS
    lg3 = lg[:, :N_EXPERTS].T.reshape(N_EXPERTS, n // LANES, LANES)
    idx, qpos, gate, start = _route(lg3, cap)
    pairs = _ffn(h2, idx.reshape(-1), qpos.reshape(-1), gate.reshape(-1, 1), w['wg'], w['wu'], w['wd'])
    y = _combine(x1, mods, mod_row, start.reshape(-1), pairs,
                 p['ln2_g'].reshape(1, -1), p['ln2_b'].reshape(1, -1))
    return y.reshape(b, seq, d), ckv3, kr3[..., :QK_ROPE]


def kernel(x_prompt, x_sample, cache_kv_c, cache_k_rope, c, c_ctx, w_ada, b_ada, w_in, q_norm_g, kv_norm_g, w_uq, w_ukv, w_o_mla, hy_short_w, hy_short_b, hy_filt_w1, hy_filt_b1, hy_filt_freq1, hy_filt_w2, hy_filt_b2, hy_filt_freq2, hy_filt_w3, hy_filt_b3, hy_skip, w_o_hy, w_out, ln1_g, ln1_b, ln2_g, ln2_b, w_router, w_gate, w_up, w_down):
    params = dict(w_in=w_in, q_norm_g=q_norm_g, kv_norm_g=kv_norm_g, w_uq=w_uq, w_ukv=w_ukv, w_o_mla=w_o_mla,
                  hy_short_w=hy_short_w, hy_short_b=hy_short_b, hy_filt_w1=hy_filt_w1, hy_filt_b1=hy_filt_b1,
                  hy_filt_freq1=hy_filt_freq1, hy_filt_w2=hy_filt_w2, hy_filt_b2=hy_filt_b2,
                  hy_filt_freq2=hy_filt_freq2, hy_filt_w3=hy_filt_w3, hy_filt_b3=hy_filt_b3, hy_skip=hy_skip,
                  w_o_hy=w_o_hy, w_out=w_out, ln1_g=ln1_g, ln1_b=ln1_b, ln2_g=ln2_g, ln2_b=ln2_b,
                  w_router=w_router, w_gate=w_gate, w_up=w_up, w_down=w_down)
    depth = w_in.shape[0]
    dec_b, dec_seq, d = x_sample.shape
    n_rows = 16
    cond = jnp.zeros((n_rows, d), F32).at[:dec_b].set(c).at[dec_b].set(c_ctx)
    y_prompt, y_sample = x_prompt, x_sample
    kv_list, kr_list = [], []
    for l in range(depth):
        p = {k: v[l] for k, v in params.items()}
        w = _prep_weights(p)
        mods = _ada_mod(cond, w_ada[l], b_ada[l]).reshape(n_rows, N_MOD, d)
        n_prompt = x_prompt.shape[0] * x_prompt.shape[1]
        y_prompt, c_kv, k_rope = _trunk(y_prompt, mods, lambda r: dec_b, min(1024, n_prompt), p, w,
                                        None, None, False)
        kv_list.append(c_kv)
        kr_list.append(k_rope)
        y_sample, _, _ = _trunk(y_sample, mods, lambda r: r // dec_seq, min(1024, dec_seq), p, w,
                                cache_kv_c[:, l], cache_k_rope[:, l], True)
    return (y_prompt, y_sample, jnp.stack(kv_list, axis=1), jnp.stack(kr_list, axis=1))
```

```python
import functools
import math

import numpy as np
import jax
import jax.numpy as jnp
from jax import lax
from jax.experimental import pallas as pl
from jax.experimental.pallas import tpu as pltpu

F32 = jnp.float32
BF16 = jnp.bfloat16
I32 = jnp.int32
HIGHEST = lax.Precision.HIGHEST

D_MODEL = 2048
GRID_W = 64
MLA_HEADS = 8
QK_NOPE = 128
QK_ROPE = 64
V_HEAD = 128
Q_LORA = 512
KV_LORA = 256
ROPE_THETA = 10000.0
ROPE_FREQS = QK_ROPE // 4
HY_WIDTH = 1024
HY_EMB = 33
HY_BANDS = (HY_EMB - 1) // 2
HY_HIDDEN = 64
HY_FAST_DECAY = 0.3
HY_SLOW_DECAY = 1.5
HY_TARGET = 0.01
N_EXPERTS = 16
EXPERT_FF = 1024
EC_CAPACITY = 2
EPS = 1e-6
DEPTH = 1
ALPHA = (2 * DEPTH) ** 0.25
N_MOD = 6

LANES = 128
HEAD_PAD = 256
SMALL_W = 1024
BIG_W = 2 * D_MODEL + 3 * HY_WIDTH
VMEM_LIMIT = 56 * 1024 * 1024
HY_TIME_BLOCKS = 4


def _cparams(sem, vmem=VMEM_LIMIT):
    return pltpu.CompilerParams(dimension_semantics=sem, vmem_limit_bytes=vmem)


def _sigmoid(x):
    return 1.0 / (1.0 + jnp.exp(-x))


def _const_spec(shape):
    nd = len(shape)
    return pl.BlockSpec(shape, lambda *_: (0,) * nd, pipeline_mode=pl.Buffered(1))


def _ada_kernel(c_ref, w_ref, b_ref, o_ref):
    c = c_ref[...]
    s = (c * _sigmoid(c)).astype(BF16)
    o_ref[...] = jnp.dot(s, w_ref[...].astype(BF16), preferred_element_type=F32) + b_ref[...]


def _ada_mod(cond, w_ada, b_ada):
    r, d = cond.shape
    n = w_ada.shape[1]
    tn = 512
    return pl.pallas_call(
        _ada_kernel,
        grid=(n // tn,),
        in_specs=[pl.BlockSpec((r, d), lambda j: (0, 0)),
                  pl.BlockSpec((d, tn), lambda j: (0, j)),
                  pl.BlockSpec((1, tn), lambda j: (0, j))],
        out_specs=pl.BlockSpec((r, tn), lambda j: (0, j)),
        out_shape=jax.ShapeDtypeStruct((r, n), F32),
        compiler_params=_cparams(("arbitrary",)),
        name="ada_mod",
    )(cond, w_ada, b_ada.reshape(1, n))


def _inproj_kernel(x_ref, mod_ref, w_ref, small_ref, big_ref, h_scr, *, n_small):
    j = pl.program_id(1)

    @pl.when(j == 0)
    def _():
        m = mod_ref[0]
        h_scr[...] = (x_ref[...] * (1.0 + m[1:2]) + m[0:1]).astype(BF16)

    acc = jnp.dot(h_scr[...], w_ref[...], preferred_element_type=F32)

    @pl.when(j < n_small)
    def _():
        small_ref[...] = acc

    @pl.when(j >= n_small)
    def _():
        big_ref[...] = acc.astype(BF16)


def _inproj(x, mods, mod_row, w_r, tm):
    n, d = x.shape
    tn = 1024
    n_small = SMALL_W // tn
    n_cols = w_r.shape[1] // tn
    return pl.pallas_call(
        functools.partial(_inproj_kernel, n_small=n_small),
        grid=(n // tm, n_cols),
        in_specs=[pl.BlockSpec((tm, d), lambda i, j: (i, 0)),
                  pl.BlockSpec((1, N_MOD, d), lambda i, j: (mod_row(i * tm), 0, 0)),
                  pl.BlockSpec((d, tn), lambda i, j: (0, j))],
        out_specs=[pl.BlockSpec((tm, tn), lambda i, j: (i, jnp.minimum(j, n_small - 1))),
                   pl.BlockSpec((tm, tn), lambda i, j: (i, jnp.maximum(j - n_small, 0)))],
        out_shape=[jax.ShapeDtypeStruct((n, SMALL_W), F32),
                   jax.ShapeDtypeStruct((n, BIG_W), BF16)],
        scratch_shapes=[pltpu.VMEM((tm, d), BF16)],
        compiler_params=_cparams(("arbitrary", "arbitrary")),
        name="inproj",
    )(x, mods, w_r)


def _qprep_kernel(*refs, rope):
    if rope:
        small_ref, qg_ref, kvg_ref, wa_ref, wb_ref, c_ref, s_ref, q_ref, ckv_ref, kr_ref = refs
    else:
        small_ref, qg_ref, kvg_ref, wa_ref, q_ref, ckv_ref, kr_ref = refs
    qc = small_ref[:, 0:Q_LORA]
    kvc = small_ref[:, Q_LORA:Q_LORA + KV_LORA]
    kr = small_ref[:, Q_LORA + KV_LORA:Q_LORA + KV_LORA + LANES]
    qn = (qc * lax.rsqrt(jnp.mean(qc * qc, axis=-1, keepdims=True) + EPS) * qg_ref[...]).astype(BF16)
    ckv_ref[...] = kvc * lax.rsqrt(jnp.mean(kvc * kvc, axis=-1, keepdims=True) + EPS) * kvg_ref[...]
    qa = jnp.dot(qn, wa_ref[...], preferred_element_type=F32)
    if rope:
        krs = small_ref[:, Q_LORA + KV_LORA + LANES:SMALL_W]
        qb = jnp.dot(qn, wb_ref[...], preferred_element_type=F32)
        cs = c_ref[...]
        sn = s_ref[...]
        kr_ref[...] = kr * cs + krs * sn
        for h in range(MLA_HEADS):
            lo = h * HEAD_PAD
            q_ref[:, lo:lo + LANES] = qa[:, lo:lo + LANES].astype(BF16)
            q_ref[:, lo + LANES:lo + HEAD_PAD] = (
                qa[:, lo + LANES:lo + HEAD_PAD] * cs + qb[:, h * LANES:(h + 1) * LANES] * sn).astype(BF16)
    else:
        kr_ref[...] = kr
        q_ref[...] = qa.astype(BF16)


def _qprep(small, qg, kvg, wa, wb, cos128, sin128, seq, rope):
    n = small.shape[0]
    tm = min(512, seq)
    nblk = seq // tm
    in_specs = [pl.BlockSpec((tm, SMALL_W), lambda i: (i, 0)),
                pl.BlockSpec((1, Q_LORA), lambda i: (0, 0)),
                pl.BlockSpec((1, KV_LORA), lambda i: (0, 0)),
                pl.BlockSpec(wa.shape, lambda i: (0, 0))]
    args = [small, qg, kvg, wa]
    if rope:
        in_specs += [pl.BlockSpec(wb.shape, lambda i: (0, 0)),
                     pl.BlockSpec((tm, LANES), lambda i: (i % nblk, 0)),
                     pl.BlockSpec((tm, LANES), lambda i: (i % nblk, 0))]
        args += [wb, cos128, sin128]
    return pl.pallas_call(
        functools.partial(_qprep_kernel, rope=rope),
        grid=(n // tm,),
        in_specs=in_specs,
        out_specs=[pl.BlockSpec((tm, MLA_HEADS * HEAD_PAD), lambda i: (i, 0)),
                   pl.BlockSpec((tm, KV_LORA), lambda i: (i, 0)),
                   pl.BlockSpec((tm, LANES), lambda i: (i, 0))],
        out_shape=[jax.ShapeDtypeStruct((n, MLA_HEADS * HEAD_PAD), BF16),
                   jax.ShapeDtypeStruct((n, KV_LORA), F32),
                   jax.ShapeDtypeStruct((n, LANES), F32)],
        compiler_params=_cparams(("arbitrary",)),
        name="qprep",
    )(*args)


def _kvup_kernel(kv_ref, kr_ref, wk_ref, wv_ref, k_ref, v_ref):
    kv = kv_ref[0].astype(BF16)
    kn = jnp.dot(kv, wk_ref[...], preferred_element_type=F32)
    vv = jnp.dot(kv, wv_ref[...], preferred_element_type=F32)
    krp = kr_ref[0].astype(BF16)
    lane = lax.broadcasted_iota(I32, (kv.shape[0], LANES), 1)
    ones_col = jnp.where(lane == 0, 1.0, 0.0).astype(BF16)
    for h in range(MLA_HEADS):
        k_ref[0, h, :, 0:LANES] = kn[:, h * QK_NOPE:(h + 1) * QK_NOPE].astype(BF16)
        k_ref[0, h, :, LANES:HEAD_PAD] = krp
        v_ref[0, h, :, 0:V_HEAD] = vv[:, h * V_HEAD:(h + 1) * V_HEAD].astype(BF16)
        v_ref[0, h, :, V_HEAD:V_HEAD + LANES] = ones_col


def _kvup(kv_all, kr_all, wk, wv):
    b, lk, _ = kv_all.shape
    tm = 512 if lk % 512 == 0 else 256
    return pl.pallas_call(
        _kvup_kernel,
        grid=(b, lk // tm),
        in_specs=[pl.BlockSpec((1, tm, KV_LORA), lambda bi, i: (bi, i, 0)),
                  pl.BlockSpec((1, tm, LANES), lambda bi, i: (bi, i, 0)),
                  pl.BlockSpec(wk.shape, lambda bi, i: (0, 0)),
                  pl.BlockSpec(wv.shape, lambda bi, i: (0, 0))],
        out_specs=[pl.BlockSpec((1, MLA_HEADS, tm, HEAD_PAD), lambda bi, i: (bi, 0, i, 0)),
                   pl.BlockSpec((1, MLA_HEADS, tm, V_HEAD + LANES), lambda bi, i: (bi, 0, i, 0))],
        out_shape=[jax.ShapeDtypeStruct((b, MLA_HEADS, lk, HEAD_PAD), BF16),
                   jax.ShapeDtypeStruct((b, MLA_HEADS, lk, V_HEAD + LANES), BF16)],
        compiler_params=_cparams(("arbitrary", "arbitrary")),
        name="kvup",
    )(kv_all, kr_all, wk, wv)


def _attn_kernel(q_ref, k_ref, v_ref, o_ref, *, rows):
    for h in range(k_ref.shape[1]):
        k = k_ref[0, h]
        v = v_ref[0, h]
        for r0 in range(0, q_ref.shape[1], rows):
            q = q_ref[0, r0:r0 + rows, h * HEAD_PAD:(h + 1) * HEAD_PAD]
            s = lax.dot_general(q, k, (((1,), (1,)), ((), ())), preferred_element_type=F32)
            p = jnp.exp2(s - jnp.max(s, axis=-1, keepdims=True)).astype(BF16)
            o = jnp.dot(p, v, preferred_element_type=F32)
            o_ref[0, r0:r0 + rows, h * V_HEAD:(h + 1) * V_HEAD] = (
                o[:, 0:V_HEAD] / o[:, V_HEAD:V_HEAD + 1]).astype(BF16)


def _attention(q, k, v):
    b, l, _ = q.shape
    lk = k.shape[2]
    rows = 256
    chains = 8
    tq = min(l, chains * rows)
    hb = max(1, chains * rows // tq)
    return pl.pallas_call(
        functools.partial(_attn_kernel, rows=rows),
        grid=(b, MLA_HEADS // hb, l // tq),
        in_specs=[pl.BlockSpec((1, tq, hb * HEAD_PAD), lambda bi, h, i: (bi, i, h)),
                  pl.BlockSpec((1, hb, lk, HEAD_PAD), lambda bi, h, i: (bi, h, 0, 0)),
                  pl.BlockSpec((1, hb, lk, V_HEAD + LANES), lambda bi, h, i: (bi, h, 0, 0))],
        out_specs=pl.BlockSpec((1, tq, hb * V_HEAD), lambda bi, h, i: (bi, i, h)),
        out_shape=jax.ShapeDtypeStruct((b, l, MLA_HEADS * V_HEAD), BF16),
        compiler_params=_cparams(("arbitrary", "arbitrary", "arbitrary")),
        name="attention",
    )(q, k, v)


def _hypre_kernel(x0_ref, x1_ref, v_ref, w_ref, b_ref, x0o_ref, z_ref, zn_ref, *, nb):
    seq = x0_ref.shape[1]
    tb = seq // nb
    row = lax.broadcasted_iota(I32, (seq, 1), 0)

    def conv(u_ref, g):
        u = u_ref[0].astype(F32)
        up = jnp.where(row == 0, 0.0, pltpu.roll(u, 1, axis=0))
        un = jnp.where(row == seq - 1, 0.0, pltpu.roll(u, seq - 1, axis=0))
        w = w_ref[g]
        return up * w[0:1] + u * w[1:2] + un * w[2:3] + b_ref[g]

    x0o_ref[0] = conv(x0_ref, 0).astype(BF16)
    z = conv(x1_ref, 1) * conv(v_ref, 2)
    z_ref[0] = z.astype(BF16)
    zs = z * (1 - 2 * (row & 1)).astype(F32)
    for j in range(nb):
        zn_ref[0, j:j + 1, :] = jnp.sum(zs[j * tb:(j + 1) * tb], axis=0, keepdims=True)


def _hypre(big3, w3, b3, nb):
    b, l, _ = big3.shape
    cb = 256
    ncb = HY_WIDTH // cb
    off = 2 * D_MODEL // cb
    return pl.pallas_call(
        functools.partial(_hypre_kernel, nb=nb),
        grid=(b, ncb),
        in_specs=[pl.BlockSpec((1, l, cb), lambda bi, j: (bi, 0, off + j)),
                  pl.BlockSpec((1, l, cb), lambda bi, j: (bi, 0, off + j + ncb)),
                  pl.BlockSpec((1, l, cb), lambda bi, j: (bi, 0, off + j + 2 * ncb)),
                  pl.BlockSpec((3, 3, cb), lambda bi, j: (0, 0, j)),
                  pl.BlockSpec((3, 1, cb), lambda bi, j: (0, 0, j))],
        out_specs=[pl.BlockSpec((1, l, cb), lambda bi, j: (bi, 0, j)),
                   pl.BlockSpec((1, l, cb), lambda bi, j: (bi, 0, j)),
                   pl.BlockSpec((1, nb, cb), lambda bi, j: (bi, 0, j))],
        out_shape=[jax.ShapeDtypeStruct((b, l, HY_WIDTH), BF16),
                   jax.ShapeDtypeStruct((b, l, HY_WIDTH), BF16),
                   jax.ShapeDtypeStruct((b, nb, HY_WIDTH), F32)],
        compiler_params=_cparams(("arbitrary", "arbitrary")),
        name="hyena_pre",
    )(big3, big3, big3, w3, b3)


def _filter_kernel(feat_ref, w1_ref, b1_ref, f1_ref, w2_ref, b2_ref, f2_ref,
                   w3f_ref, w3b_ref, b3f_ref, b3b_ref, dl_ref, hf_ref, hb_ref, ext_ref, hd_scr, *, nb):
    seq = feat_ref.shape[0]
    tb = seq // nb

    @pl.when(pl.program_id(0) == 0)
    def _():
        h1 = jnp.sin(f1_ref[...] * (jnp.dot(feat_ref[...], w1_ref[...], precision=HIGHEST,
                                            preferred_element_type=F32) + b1_ref[...]))
        hd_scr[...] = jnp.sin(f2_ref[...] * (jnp.dot(h1, w2_ref[...], precision=HIGHEST,
                                                     preferred_element_type=F32) + b2_ref[...]))

    hd = hd_scr[...]
    row = lax.broadcasted_iota(I32, (seq, 1), 0)
    t = row.astype(F32) / float(max(seq - 1, 1))
    window = jnp.exp(-t * dl_ref[...])
    hf = (jnp.dot(hd, w3f_ref[...], precision=HIGHEST, preferred_element_type=F32) + b3f_ref[...]) * window
    hb = (jnp.dot(hd, w3b_ref[...], precision=HIGHEST, preferred_element_type=F32) + b3b_ref[...]) * window
    hb = jnp.where(row == 0, 0.0, hb)
    den = jnp.sum(jnp.abs(hf) + jnp.abs(hb), axis=0, keepdims=True) + EPS
    hf = hf / den
    hb = hb / den
    hf_ref[...] = hf.astype(BF16)
    hb_ref[...] = hb.astype(BF16)
    sign = (1 - 2 * (row & 1)).astype(F32)
    for j in range(nb):
        sl = slice(j * tb, (j + 1) * tb)
        ext_ref[0, j:j + 1, :] = hf[j * tb:j * tb + 1].astype(BF16).astype(F32)
        ext_ref[1, j:j + 1, :] = hb[j * tb:j * tb + 1].astype(BF16).astype(F32)
        ext_ref[2, j:j + 1, :] = jnp.sum(hf[sl] * sign[sl], axis=0, keepdims=True)
        ext_ref[3, j:j + 1, :] = jnp.sum(hb[sl] * sign[sl], axis=0, keepdims=True)


def _filter_td(seq, p, nb):
    pos = np.arange(seq, dtype=np.float64)
    t = pos / max(seq - 1, 1)
    bands = np.linspace(1e-4, HY_BANDS - 1, HY_BANDS)
    w = 2.0 * math.pi * pos / seq
    feats = np.concatenate([t[:, None], np.cos(w[:, None] * bands), -np.sin(w[:, None] * bands)], axis=-1)
    feats = jnp.asarray(np.pad(feats, ((0, 0), (0, LANES - HY_EMB))), F32)
    w1 = jnp.pad(p['hy_filt_w1'], ((0, LANES - HY_EMB), (0, 0)))
    deltas = jnp.asarray(np.abs(np.linspace(math.log(HY_TARGET) / HY_SLOW_DECAY, math.log(HY_TARGET) / HY_FAST_DECAY,
                                            HY_WIDTH)).reshape(1, HY_WIDTH), F32)
    cb = 256
    ncb = HY_WIDTH // cb
    row = lambda v: v.reshape(1, -1)
    c2 = lambda j: (0, 0)
    return pl.pallas_call(
        functools.partial(_filter_kernel, nb=nb),
        grid=(ncb,),
        in_specs=[pl.BlockSpec((seq, LANES), c2),
                  pl.BlockSpec((LANES, HY_HIDDEN), c2), pl.BlockSpec((1, HY_HIDDEN), c2),
                  pl.BlockSpec((1, HY_HIDDEN), c2),
                  pl.BlockSpec((HY_HIDDEN, HY_HIDDEN), c2), pl.BlockSpec((1, HY_HIDDEN), c2),
                  pl.BlockSpec((1, HY_HIDDEN), c2),
                  pl.BlockSpec((HY_HIDDEN, cb), lambda j: (0, j)),
                  pl.BlockSpec((HY_HIDDEN, cb), lambda j: (0, j + ncb)),
                  pl.BlockSpec((1, cb), lambda j: (0, j)),
                  pl.BlockSpec((1, cb), lambda j: (0, j + ncb)),
                  pl.BlockSpec((1, cb), lambda j: (0, j))],
        out_specs=[pl.BlockSpec((seq, cb), lambda j: (0, j)),
                   pl.BlockSpec((seq, cb), lambda j: (0, j)),
                   pl.BlockSpec((4, nb, cb), lambda j: (0, 0, j))],
        out_shape=[jax.ShapeDtypeStruct((seq, HY_WIDTH), BF16),
                   jax.ShapeDtypeStruct((seq, HY_WIDTH), BF16),
                   jax.ShapeDtypeStruct((4, nb, HY_WIDTH), F32)],
        scratch_shapes=[pltpu.VMEM((seq, HY_HIDDEN), F32)],
        compiler_params=_cparams(("arbitrary",)),
        name="hyena_filter",
    )(feats, w1, row(p['hy_filt_b1']), row(p['hy_filt_freq1']),
      p['hy_filt_w2'], row(p['hy_filt_b2']), row(p['hy_filt_freq2']),
      p['hy_filt_w3'], p['hy_filt_w3'], row(p['hy_filt_b3']), row(p['hy_filt_b3']), deltas)


def _kspec_kernel(c_ref, s_ref, hf_ref, hb_ref, ext_ref, kc_ref, ks_ref, kn_ref, *, nb):
    tb = c_ref.shape[0]
    f = lax.broadcasted_iota(I32, (tb, 1), 0)
    wgt = jnp.where(f == 0, 1.0, 2.0) / float(2 * tb)
    sig = (1 - 2 * (f & 1)).astype(F32)
    cm = c_ref[...]
    sm = s_ref[...]
    blk = lambda ref, j: ref[j * tb:(j + 1) * tb, :]
    chf = [jnp.dot(cm, blk(hf_ref, j), preferred_element_type=F32) for j in range(nb)]
    shf = [jnp.dot(sm, blk(hf_ref, j), preferred_element_type=F32) for j in range(nb)]
    chb = [jnp.dot(cm, blk(hb_ref, j), preferred_element_type=F32) for j in range(nb)]
    shb = [jnp.dot(sm, blk(hb_ref, j), preferred_element_type=F32) for j in range(nb)]
    first_f = lambda j: ext_ref[0, j:j + 1, :]
    first_b = lambda j: ext_ref[1, j:j + 1, :]
    alt_f = lambda j: ext_ref[2, j:j + 1, :]
    alt_b = lambda j: ext_ref[3, j:j + 1, :]
    for d in range(-(nb - 1), nb):
        k = d + nb - 1
        if d == 0:
            kc, ks, kn = chf[0] + chb[0], shf[0] - shb[0], alt_f(0) + alt_b(0)
        elif d > 0:
            kc = chf[d] + sig * (chf[d - 1] - first_f(d - 1))
            ks = shf[d] + sig * shf[d - 1]
            kn = alt_f(d) + alt_f(d - 1) - first_f(d - 1)
        else:
            e = -d
            kc = chb[e] + sig * (chb[e - 1] - first_b(e - 1))
            ks = -(shb[e] + sig * shb[e - 1])
            kn = alt_b(e) + alt_b(e - 1) - first_b(e - 1)
        kc_ref[k] = kc * wgt
        ks_ref[k] = ks * wgt
        kn_ref[k:k + 1, :] = kn * (1.0 / float(2 * tb))


def _kspec(cmat, smat, hf, hb, ext, nb):
    tb = cmat.shape[0]
    seq = hf.shape[0]
    nd = 2 * nb - 1
    cb = 256 if nb > 1 else 512
    return pl.pallas_call(
        functools.partial(_kspec_kernel, nb=nb),
        grid=(HY_WIDTH // cb,),
        in_specs=[pl.BlockSpec((tb, tb), lambda j: (0, 0)),
                  pl.BlockSpec((tb, tb), lambda j: (0, 0)),
                  pl.BlockSpec((seq, cb), lambda j: (0, j)),
                  pl.BlockSpec((seq, cb), lambda j: (0, j)),
                  pl.BlockSpec((4, nb, cb), lambda j: (0, 0, j))],
        out_specs=[pl.BlockSpec((nd, tb, cb), lambda j: (0, 0, j)),
                   pl.BlockSpec((nd, tb, cb), lambda j: (0, 0, j)),
                   pl.BlockSpec((nd, cb), lambda j: (0, j))],
        out_shape=[jax.ShapeDtypeStruct((nd, tb, HY_WIDTH), F32),
                   jax.ShapeDtypeStruct((nd, tb, HY_WIDTH), F32),
                   jax.ShapeDtypeStruct((nd, HY_WIDTH), F32)],
        compiler_params=_cparams(("arbitrary",)),
        name="hyena_kspec",
    )(cmat, smat, hf, hb, ext)


def _fwd_kernel(c_ref, s_ref, z_ref, kc_ref, ks_ref, pc_ref, ps_ref, *, nb):
    tb = c_ref.shape[0]
    cm = c_ref[...]
    sm = s_ref[...]
    zc = [jnp.dot(cm, z_ref[0, j * tb:(j + 1) * tb, :], preferred_element_type=F32) for j in range(nb)]
    zs = [jnp.dot(sm, z_ref[0, j * tb:(j + 1) * tb, :], preferred_element_type=F32) for j in range(nb)]
    for i in range(nb):
        pc = None
        ps = None
        for j in range(nb):
            kc = kc_ref[i - j + nb - 1]
            ks = ks_ref[i - j + nb - 1]
            tc = zc[j] * kc - zs[j] * ks
            ts = zc[j] * ks + zs[j] * kc
            pc = tc if pc is None else pc + tc
            ps = ts if ps is None else ps + ts
        pc_ref[0, i * tb:(i + 1) * tb, :] = pc.astype(BF16)
        ps_ref[0, i * tb:(i + 1) * tb, :] = ps.astype(BF16)


def _hy_fwd(cmat, smat, z, kc, ks, nb):
    b, seq, _ = z.shape
    tb = seq // nb
    nd = 2 * nb - 1
    cb = 256 if nb > 1 else HY_WIDTH
    return pl.pallas_call(
        functools.partial(_fwd_kernel, nb=nb),
        grid=(HY_WIDTH // cb, b),
        in_specs=[pl.BlockSpec((tb, tb), lambda j, bi: (0, 0)),
                  pl.BlockSpec((tb, tb), lambda j, bi: (0, 0)),
                  pl.BlockSpec((1, seq, cb), lambda j, bi: (bi, 0, j)),
                  pl.BlockSpec((nd, tb, cb), lambda j, bi: (0, 0, j)),
                  pl.BlockSpec((nd, tb, cb), lambda j, bi: (0, 0, j))],
        out_specs=[pl.BlockSpec((1, seq, cb), lambda j, bi: (bi, 0, j)),
                   pl.BlockSpec((1, seq, cb), lambda j, bi: (bi, 0, j))],
        out_shape=[jax.ShapeDtypeStruct((b, seq, HY_WIDTH), BF16),
                   jax.ShapeDtypeStruct((b, seq, HY_WIDTH), BF16)],
        compiler_params=_cparams(("arbitrary", "arbitrary")),
        name="hyena_fwd",
    )(cmat, smat, z, kc, ks)


def _inv_kernel(c_ref, s_ref, pc_ref, ps_ref, z_ref, x0_ref, zn_ref, kn_ref, skip_ref, o_ref, *, nb):
    tb = c_ref.shape[0]
    cm = c_ref[...]
    sm = s_ref[...]
    t = lax.broadcasted_iota(I32, (tb, 1), 0)
    sign = (1 - 2 * (t & 1)).astype(F32)
    for i in range(nb):
        rs = slice(i * tb, (i + 1) * tb)
        y = jnp.dot(cm, pc_ref[0, rs, :], preferred_element_type=F32)
        y = y + jnp.dot(sm, ps_ref[0, rs, :], preferred_element_type=F32)
        nyq = None
        for j in range(nb):
            k = i - j + nb - 1
            term = zn_ref[0, j:j + 1, :] * kn_ref[k:k + 1, :]
            nyq = term if nyq is None else nyq + term
        y = y + sign * nyq + z_ref[0, rs, :].astype(F32) * skip_ref[...]
        o_ref[0, rs, :] = (x0_ref[0, rs, :].astype(F32) * y).astype(BF16)


def _hy_inv(cmat, smat, pc, ps, z, x0, zn, kn, skip, nb):
    b, seq, _ = z.shape
    tb = seq // nb
    nd = 2 * nb - 1
    cb = 256 if nb > 1 else HY_WIDTH
    blk = pl.BlockSpec((1, seq, cb), lambda j, bi: (bi, 0, j))
    return pl.pallas_call(
        functools.partial(_inv_kernel, nb=nb),
        grid=(HY_WIDTH // cb, b),
        in_specs=[pl.BlockSpec((tb, tb), lambda j, bi: (0, 0)),
                  pl.BlockSpec((tb, tb), lambda j, bi: (0, 0)),
                  blk, blk, blk, blk,
                  pl.BlockSpec((1, nb, cb), lambda j, bi: (bi, 0, j)),
                  pl.BlockSpec((nd, cb), lambda j, bi: (0, j)),
                  pl.BlockSpec((1, cb), lambda j, bi: (0, j))],
        out_specs=blk,
        out_shape=jax.ShapeDtypeStruct((b, seq, HY_WIDTH), BF16),
        compiler_params=_cparams(("arbitrary", "arbitrary")),
        name="hyena_inv",
    )(cmat, smat, pc, ps, z, x0, zn, kn, skip)


def _layer_norm(v, g, b):
    mu = jnp.mean(v, axis=-1, keepdims=True)
    vc = v - mu
    var = jnp.mean(vc * vc, axis=-1, keepdims=True)
    return vc * lax.rsqrt(var + EPS) * g + b


def _merge_kernel(at_ref, hy_ref, ga_ref, gb_ref, woa_ref, woh_ref, o_ref):
    a = jnp.dot(at_ref[...], woa_ref[...], preferred_element_type=F32)
    hh = jnp.dot(hy_ref[...], woh_ref[...], preferred_element_type=F32)
    merged = _sigmoid(ga_ref[...].astype(F32)) * a + _sigmoid(gb_ref[...].astype(F32)) * hh
    o_ref[...] = merged.astype(BF16)


def _merge(attn, hy, big, woa, woh):
    n = attn.shape[0]
    d = woa.shape[1]
    tm = 512
    return pl.pallas_call(
        _merge_kernel,
        grid=(n // tm,),
        in_specs=[pl.BlockSpec((tm, attn.shape[1]), lambda i: (i, 0)),
                  pl.BlockSpec((tm, hy.shape[1]), lambda i: (i, 0)),
                  pl.BlockSpec((tm, d), lambda i: (i, 0)),
                  pl.BlockSpec((tm, d), lambda i: (i, 1)),
                  _const_spec(woa.shape), _const_spec(woh.shape)],
        out_specs=pl.BlockSpec((tm, d), lambda i: (i, 0)),
        out_shape=jax.ShapeDtypeStruct((n, d), BF16),
        compiler_params=_cparams(("arbitrary",)),
        name="merge",
    )(attn, hy, big, big, woa, woh)


def _ln1_kernel(x_ref, mg_ref, mod_ref, wout_ref, g_ref, b_ref, wr_ref, x1_ref, h2_ref, lg_ref, *, n_sub):
    m = mod_ref[0]
    rows = x_ref.shape[0] // n_sub
    for t in range(n_sub):
        rs = slice(t * rows, (t + 1) * rows)
        mix = jnp.dot(mg_ref[rs, :], wout_ref[...], preferred_element_type=F32)
        x1 = _layer_norm(ALPHA * x_ref[rs, :] + m[2:3] * mix, g_ref[...], b_ref[...])
        x1_ref[rs, :] = x1
        h2 = x1 * (1.0 + m[4:5]) + m[3:4]
        h2_ref[rs, :] = h2
        hi = h2.astype(BF16)
        lo = (h2 - hi.astype(F32)).astype(BF16)
        lg = (jnp.dot(hi, wr_ref[...], preferred_element_type=F32)
              + jnp.dot(lo, wr_ref[...], preferred_element_type=F32))
        lg_ref[rs, :] = lg + pltpu.roll(lg, LANES - N_EXPERTS, axis=1)


def _ln1(x, merged, mods, mod_row, wout, g, b, wr_pack):
    n, d = x.shape
    tm = 512
    return pl.pallas_call(
        functools.partial(_ln1_kernel, n_sub=2),
        grid=(n // tm,),
        in_specs=[pl.BlockSpec((tm, d), lambda i: (i, 0)),
                  pl.BlockSpec((tm, d), lambda i: (i, 0)),
                  pl.BlockSpec((1, N_MOD, d), lambda i: (mod_row(i * tm), 0, 0)),
                  _const_spec(wout.shape), _const_spec((1, d)), _const_spec((1, d)),
                  _const_spec(wr_pack.shape)],
        out_specs=[pl.BlockSpec((tm, d), lambda i: (i, 0)),
                   pl.BlockSpec((tm, d), lambda i: (i, 0)),
                   pl.BlockSpec((tm, LANES), lambda i: (i, 0))],
        out_shape=[jax.ShapeDtypeStruct((n, d), F32),
                   jax.ShapeDtypeStruct((n, d), F32),
                   jax.ShapeDtypeStruct((n, LANES), F32)],
        compiler_params=_cparams(("arbitrary",)),
        name="ln1_router",
    )(x, merged, mods, wout, g, b, wr_pack)


def _split3(t):
    t1 = t.astype(BF16)
    r = t - t1.astype(F32)
    t2 = r.astype(BF16)
    t3 = (r - t2.astype(F32)).astype(BF16)
    return t1, t2, t3


def _dot_exact(sel01, table):
    t1, t2, t3 = _split3(table)
    out = jnp.dot(sel01, t1, preferred_element_type=F32)
    out = out + jnp.dot(sel01, t2, preferred_element_type=F32)
    return out + jnp.dot(sel01, t3, preferred_element_type=F32)


def _route_kernel(lg_ref, idx_ref, q_ref, gate_ref, start_ref, aff_scr, sel_scr, pos_scr, *, cap):
    ne, nc, _ = lg_ref.shape
    lg = lg_ref[...]
    mx = jnp.max(lg, axis=0, keepdims=True)
    ex = jnp.exp(lg - mx)
    aff = ex / jnp.sum(ex, axis=0, keepdims=True)
    aff_scr[...] = aff

    def tbody(k, thr):
        cand = thr | lax.shift_left(jnp.int32(1), 30 - k)
        ge = jnp.where(aff >= lax.bitcast_convert_type(cand, F32), 1.0, 0.0)
        cnt = jnp.sum(jnp.sum(ge, axis=1, keepdims=True), axis=2, keepdims=True)
        return jnp.where(cnt >= float(cap), cand, thr)

    thr = lax.fori_loop(0, 31, tbody, jnp.zeros((ne, 1, 1), I32))
    lo = lax.bitcast_convert_type(thr, F32)
    above = aff >= lax.bitcast_convert_type(thr + 1, F32)
    sel_scr[...] = jnp.where(above, 1.0, 0.0)
    pos_scr[...] = jnp.where((aff >= lo) & jnp.logical_not(above), aff, -1.0)

    r_i = lax.broadcasted_iota(I32, (LANES, LANES), 0)
    c_i = lax.broadcasted_iota(I32, (LANES, LANES), 1)
    upper_incl = jnp.where(r_i <= c_i, 1.0, 0.0).astype(BF16)
    ones_sq = jnp.ones((LANES, LANES), BF16)
    rr_i = lax.broadcasted_iota(I32, (nc, nc), 0)
    cc_i = lax.broadcasted_iota(I32, (nc, nc), 1)
    lower_strict = jnp.where(cc_i < rr_i, 1.0, 0.0).astype(BF16)
    upper_rows = jnp.where(rr_i <= cc_i, 1.0, 0.0).astype(BF16)

    def prefix(x01):
        xb = x01.astype(BF16)
        p1 = jnp.dot(xb, upper_incl, preferred_element_type=F32)
        totb = jnp.dot(xb, ones_sq, preferred_element_type=F32)
        offs = jnp.dot(lower_strict, totb.astype(BF16), preferred_element_type=F32)
        return p1, offs

    tok = (lax.broadcasted_iota(I32, (nc, LANES), 0) * LANES
           + lax.broadcasted_iota(I32, (nc, LANES), 1)).astype(F32)

    def reduce2(fn, v):
        return fn(fn(v, axis=0, keepdims=True), axis=1, keepdims=True)

    def pass1(e, start):
        above_e = sel_scr[e]
        need = cap - jnp.sum(above_e).astype(I32)

        def pick(_, carry):
            vals, picked = carry
            first = reduce2(jnp.min, jnp.where(vals == reduce2(jnp.max, vals), tok, float(nc * LANES)))
            hit = tok == first
            return jnp.where(hit, -1.0, vals), picked + jnp.where(hit, 1.0, 0.0)

        _, picked = lax.fori_loop(0, need, pick, (pos_scr[e], jnp.zeros((nc, LANES), F32)))
        sel = above_e + picked
        sel_scr[e] = sel
        s1, soffs = prefix(sel)
        return start + (s1 + soffs - sel)

    start = lax.fori_loop(0, ne, pass1, jnp.zeros((nc, LANES), F32))
    start_ref[...] = start.astype(I32)

    p_col = lax.broadcasted_iota(I32, (cap, 1), 0).astype(F32)
    lane_row = lax.broadcasted_iota(I32, (1, LANES), 1).astype(F32)
    chunk_row = lax.broadcasted_iota(I32, (1, nc), 1).astype(F32)
    ones8 = jnp.ones((8, LANES), BF16)

    def row_sums(m):
        ones = jnp.ones((8, m.shape[1]), BF16)
        nt = lambda part: lax.dot_general(ones, part, (((1,), (1,)), ((), ())), preferred_element_type=F32)
        m1, m2, m3 = _split3(m)
        return (nt(m1) + nt(m2) + nt(m3))[0:1]

    def pass2(e, before):
        sel = sel_scr[e]
        selb = sel.astype(BF16)
        p1, offs = prefix(sel)
        tot_row = lax.dot_general(ones8, selb, (((1,), (1,)), ((), ())), preferred_element_type=F32)
        cum_row = jnp.dot(tot_row.astype(BF16), upper_rows, preferred_element_type=F32)
        cum1 = cum_row[0:1]
        prev1 = cum1 - tot_row[0:1]
        in_chunk = (prev1 <= p_col) & (p_col < cum1)
        ohc = jnp.where(in_chunk, 1.0, 0.0).astype(BF16)
        keyg = jnp.dot(ohc, (p1 * sel).astype(BF16), preferred_element_type=F32)
        offg = _dot_exact(ohc, offs)
        ohl = keyg == (p_col - offg + 1.0)
        l_p = row_sums(jnp.where(ohl, lane_row, 0.0))
        c_p = row_sums(jnp.where(in_chunk, chunk_row, 0.0))
        idx_ref[e] = (c_p * float(LANES) + l_p).astype(I32)
        qtab = _dot_exact(ohc, start) + jnp.dot(ohc, before.astype(BF16), preferred_element_type=F32)
        q_ref[e] = row_sums(jnp.where(ohl, qtab, 0.0)).astype(I32)
        afg = _dot_exact(ohc, aff_scr[e])
        gate_ref[e] = row_sums(jnp.where(ohl, afg, 0.0))
        return before + sel

    lax.fori_loop(0, ne, pass2, jnp.zeros((nc, LANES), F32))


def _route(lg3, cap):
    ne, nc, _ = lg3.shape
    full = lambda s: pl.BlockSpec(s, lambda: (0,) * len(s))
    return pl.pallas_call(
        functools.partial(_route_kernel, cap=cap),
        in_specs=[full(lg3.shape)],
        out_specs=[full((ne, 1, cap)), full((ne, 1, cap)), full((ne, 1, cap)), full((nc, LANES))],
        out_shape=[jax.ShapeDtypeStruct((ne, 1, cap), I32),
                   jax.ShapeDtypeStruct((ne, 1, cap), I32),
                   jax.ShapeDtypeStruct((ne, 1, cap), F32),
                   jax.ShapeDtypeStruct((nc, LANES), I32)],
        scratch_shapes=[pltpu.VMEM((ne, nc, LANES), F32),
                        pltpu.VMEM((ne, nc, LANES), F32),
                        pltpu.VMEM((ne, nc, LANES), F32)],
        compiler_params=pltpu.CompilerParams(vmem_limit_bytes=VMEM_LIMIT),
        name="route",
    )(lg3)


def _ffn_kernel(idx_ref, q_ref, gate_ref, wg_ref, wu_ref, wd_ref, h_hbm, y_hbm,
                xb, xa, yb, ya, gsem, ssem, *, tm, npair, total):
    step = pl.program_id(0) * npair + pl.program_id(1)
    t0 = 2 * step
    t1 = t0 + 1

    def gather_row(base, r, buf, s):
        return pltpu.make_async_copy(h_hbm.at[pl.ds(idx_ref[base + r], 1), :], buf.at[pl.ds(r, 1), :], gsem.at[s])

    def scatter_row(base, r, buf, s):
        return pltpu.make_async_copy(buf.at[pl.ds(r, 1), :], y_hbm.at[pl.ds(q_ref[base + r], 1), :], ssem.at[s])

    def wait_gather(buf, s):
        pltpu.make_async_copy(h_hbm.at[pl.ds(0, tm), :], buf, gsem.at[s]).wait()

    def wait_scatter(buf, s):
        pltpu.make_async_copy(buf, y_hbm.at[pl.ds(0, tm), :], ssem.at[s]).wait()

    n_chunk = 4
    fc = EXPERT_FF // n_chunk
    per = tm // n_chunk

    def compute(x_buf, y_buf, r0, gather_base, gather_buf, scatter_base, scatter_buf, s):
        x = x_buf[...].astype(BF16)
        gate = gate_ref[r0:r0 + tm, :]
        for c in range(n_chunk):
            for r in range(c * per, (c + 1) * per):
                gather_row(gather_base, r, gather_buf, s).start()
            for r in range(c * per, (c + 1) * per):
                scatter_row(scatter_base, r, scatter_buf, s).start()
            g = jnp.dot(x, wg_ref[0, :, c * fc:(c + 1) * fc], preferred_element_type=F32)
            u = jnp.dot(x, wu_ref[0, :, c * fc:(c + 1) * fc], preferred_element_type=F32)
            hid = (g * _sigmoid(g) * u * gate).astype(BF16)
            part = jnp.dot(hid, wd_ref[0, c * fc:(c + 1) * fc, :], preferred_element_type=F32)
            if c == 0:
                y_buf[...] = part
            else:
                y_buf[...] += part

    @pl.when(step == 0)
    def _():
        yb[...] = jnp.zeros(yb.shape, F32)

        def body(r, carry):
            gather_row(0, r, xa, 0).start()
            return carry

        lax.fori_loop(0, tm, body, 0)

    wait_gather(xa, 0)

    @pl.when(step >= 1)
    def _():
        wait_scatter(ya, 0)

    base1 = t1 * tm
    pbase = jnp.maximum(t0 - 1, 0) * tm
    compute(xa, ya, 0, base1, xb, pbase, yb, 1)

    wait_gather(xb, 1)
    wait_scatter(yb, 1)
    nbase = jnp.minimum(t1 + 1, total - 1) * tm
    base0 = t0 * tm
    compute(xb, yb, tm, nbase, xa, base0, ya, 0)

    @pl.when(t1 == total - 1)
    def _():
        wait_gather(xa, 0)
        wait_scatter(ya, 0)

        def body(r, carry):
            scatter_row(base1, r, yb, 1).start()
            return carry

        lax.fori_loop(0, tm, body, 0)
        wait_scatter(yb, 1)


def _ffn(h2, idx, qpos, gate, wg, wu, wd):
    n, d = h2.shape
    ne = wg.shape[0]
    cap = idx.shape[0] // ne
    tm = min(512, cap // 2)
    npair = cap // (2 * tm)
    total = 2 * ne * npair
    grid_spec = pltpu.PrefetchScalarGridSpec(
        num_scalar_prefetch=2,
        grid=(ne, npair),
        in_specs=[pl.BlockSpec((2 * tm, 1), lambda e, i, *_: (e * npair + i, 0)),
                  pl.BlockSpec((1, d, EXPERT_FF), lambda e, i, *_: (e, 0, 0)),
                  pl.BlockSpec((1, d, EXPERT_FF), lambda e, i, *_: (e, 0, 0)),
                  pl.BlockSpec((1, EXPERT_FF, d), lambda e, i, *_: (e, 0, 0)),
                  pl.BlockSpec(memory_space=pl.ANY)],
        out_specs=pl.BlockSpec(memory_space=pl.ANY),
        scratch_shapes=[pltpu.VMEM((tm, d), F32), pltpu.VMEM((tm, d), F32),
                        pltpu.VMEM((tm, d), F32), pltpu.VMEM((tm, d), F32),
                        pltpu.SemaphoreType.DMA((2,)),
                        pltpu.SemaphoreType.DMA((2,))],
    )
    return pl.pallas_call(
        functools.partial(_ffn_kernel, tm=tm, npair=npair, total=total),
        grid_spec=grid_spec,
        out_shape=jax.ShapeDtypeStruct((ne * cap, d), F32),
        compiler_params=_cparams(("arbitrary", "arbitrary")),
        name="expert_ffn",
    )(idx, qpos, gate, wg, wu, wd, h2)


def _combine_kernel(cs_ref, x1_ref, mod_ref, st_ref, en_ref, g_ref, b_ref, y_hbm, o_ref,
                    ybuf, acc, sem, *, win, total, nsteps):
    i = pl.program_id(0)
    slot = i % 2
    lo0 = cs_ref[i]
    hi = cs_ref[i + 1]
    adv = win - 8

    def win_start(lo):
        return pl.multiple_of((jnp.minimum(lo, total - win) // 8) * 8, 8)

    def window(lo, sl):
        return pltpu.make_async_copy(y_hbm.at[pl.ds(win_start(lo), win), :], ybuf.at[sl], sem.at[sl])

    @pl.when(i == 0)
    def _():
        window(lo0, 0).start()

    @pl.when(i + 1 < nsteps)
    def _():
        window(hi, 1 - slot).start()

    ntrip = jnp.maximum((hi - lo0 + adv - 1) // adv, 1)
    acc[...] = jnp.zeros_like(acc)
    st = st_ref[...]
    en = en_ref[...]

    def body(k, carry):
        lo = lo0 + k * adv

        @pl.when(k > 0)
        def _():
            window(lo, slot).start()

        window(lo, slot).wait()
        qj = win_start(lo) + lax.broadcasted_iota(I32, (1, win), 1)
        first = jnp.maximum(st, lo)
        last = jnp.minimum(en, lo + adv)
        seg = (qj >= first) & (qj < last)
        acc[...] += jnp.dot(jnp.where(seg, 1.0, 0.0).astype(BF16), ybuf[slot].astype(BF16),
                            preferred_element_type=F32)
        return carry

    lax.fori_loop(0, ntrip, body, 0)
    m = mod_ref[0]
    o_ref[...] = _layer_norm(ALPHA * x1_ref[...] + m[5:6] * acc[...], g_ref[...], b_ref[...])


def _combine(x1, mods, mod_row, start, pairs, g, b):
    n, d = x1.shape
    total = pairs.shape[0]
    tc = 256
    win = 640
    st = start.reshape(n, 1)
    en = jnp.concatenate([start[1:], jnp.full((1,), total, I32)]).reshape(n, 1)
    cs = jnp.concatenate([start[::tc], jnp.full((1,), total, I32)])
    grid_spec = pltpu.PrefetchScalarGridSpec(
        num_scalar_prefetch=1,
        grid=(n // tc,),
        in_specs=[pl.BlockSpec((tc, d), lambda i, *_: (i, 0)),
                  pl.BlockSpec((1, N_MOD, d), lambda i, *_: (mod_row(i * tc), 0, 0)),
                  pl.BlockSpec((tc, 1), lambda i, *_: (i, 0)),
                  pl.BlockSpec((tc, 1), lambda i, *_: (i, 0)),
                  pl.BlockSpec((1, d), lambda i, *_: (0, 0)),
                  pl.BlockSpec((1, d), lambda i, *_: (0, 0)),
                  pl.BlockSpec(memory_space=pl.ANY)],
        out_specs=pl.BlockSpec((tc, d), lambda i, *_: (i, 0)),
        scratch_shapes=[pltpu.VMEM((2, win, d), F32),
                        pltpu.VMEM((tc, d), F32),
                        pltpu.SemaphoreType.DMA((2,))],
    )
    return pl.pallas_call(
        functools.partial(_combine_kernel, win=win, total=total, nsteps=n // tc),
        grid_spec=grid_spec,
        out_shape=jax.ShapeDtypeStruct((n, d), F32),
        compiler_params=_cparams(("arbitrary",)),
        name="combine_ln2",
    )(cs, x1, mods, st, en, g, b, pairs)


def _swap_pairs(w):
    f = ROPE_FREQS
    return jnp.concatenate([w[..., f:2 * f], w[..., 0:f], w[..., 3 * f:4 * f], w[..., 2 * f:3 * f]], axis=-1)


def _rope_tables(seq):
    n_rows = seq // GRID_W
    row = np.repeat(np.arange(n_rows, dtype=np.float64), GRID_W)
    col = np.tile(np.arange(GRID_W, dtype=np.float64), n_rows)
    inv = ROPE_THETA ** (-np.arange(ROPE_FREQS, dtype=np.float64) / ROPE_FREQS)
    ar = row[:, None] * inv
    ac = col[:, None] * inv
    z = np.zeros((seq, LANES - QK_ROPE))
    cos128 = np.concatenate([np.cos(ar), np.cos(ar), np.cos(ac), np.cos(ac), z], axis=1)
    sin128 = np.concatenate([-np.sin(ar), np.sin(ar), -np.sin(ac), np.sin(ac), z], axis=1)
    return jnp.asarray(cos128, F32), jnp.asarray(sin128, F32)


def _dft_tables(seq):
    m = 2 * seq
    blk = 64
    s = np.arange(seq, dtype=np.int64)
    ang1 = ((np.arange(0, seq, blk, dtype=np.int64)[:, None] * s[None, :]) % m) * (2.0 * math.pi / m)
    ang0 = ((np.arange(blk, dtype=np.int64)[:, None] * s[None, :]) % m) * (2.0 * math.pi / m)
    c1, s1 = jnp.asarray(np.cos(ang1), F32)[:, None, :], jnp.asarray(np.sin(ang1), F32)[:, None, :]
    c0, s0 = jnp.asarray(np.cos(ang0), F32)[None, :, :], jnp.asarray(np.sin(ang0), F32)[None, :, :]
    cmat = (c1 * c0 - s1 * s0).reshape(seq, seq).astype(BF16)
    smat = (s1 * c0 + c1 * s0).reshape(seq, seq).astype(BF16)
    return cmat, smat


def _relayout_kernel(w_ref, kr_ref, o_ref, *, s1, s2, s3):
    chunk = 512
    width = w_ref.shape[1]
    o_ref[:, 0:s1] = w_ref[:, 0:s1].astype(BF16)
    o_ref[:, s1:SMALL_W] = kr_ref[...]
    dst = SMALL_W
    for lo, hi in ((s3, width), (s2, s3)):
        for c in range(lo, hi, chunk):
            n = min(chunk, hi - c)
            o_ref[:, dst:dst + n] = w_ref[:, c:c + n].astype(BF16)
            dst += n


def _relayout_w_in(w_in, kr_blk, s1, s2, s3):
    d, width = w_in.shape
    tr = 256
    return pl.pallas_call(
        functools.partial(_relayout_kernel, s1=s1, s2=s2, s3=s3),
        grid=(d // tr,),
        in_specs=[pl.BlockSpec((tr, width), lambda i: (i, 0)),
                  pl.BlockSpec((tr, SMALL_W - s1), lambda i: (i, 0))],
        out_specs=pl.BlockSpec((tr, SMALL_W + BIG_W), lambda i: (i, 0)),
        out_shape=jax.ShapeDtypeStruct((d, SMALL_W + BIG_W), BF16),
        compiler_params=_cparams(("arbitrary",)),
        name="w_in_relayout",
    )(w_in, kr_blk)


def _prep_weights(p):
    w = {}
    w_in = p['w_in']
    s0, s1, s2, s3, s4 = (Q_LORA, Q_LORA + KV_LORA, Q_LORA + KV_LORA + QK_ROPE,
                          Q_LORA + KV_LORA + QK_ROPE + 3 * HY_WIDTH,
                          Q_LORA + KV_LORA + QK_ROPE + 3 * HY_WIDTH + D_MODEL)
    kr = w_in[:, s1:s2]
    z64 = jnp.zeros((D_MODEL, LANES - QK_ROPE), F32)
    kr_blk = jnp.concatenate([kr, z64, _swap_pairs(kr), z64], axis=1).astype(BF16)
    w['w_in'] = _relayout_w_in(w_in, kr_blk, s1, s2, s3)
    scale = (QK_NOPE + QK_ROPE) ** -0.5 * math.log2(math.e)
    wq = (p['w_uq'] * scale).reshape(Q_LORA, MLA_HEADS, QK_NOPE + QK_ROPE)
    zq = jnp.zeros((Q_LORA, MLA_HEADS, HEAD_PAD - QK_NOPE - QK_ROPE), F32)
    w['wa'] = jnp.concatenate([wq, zq], axis=-1).reshape(Q_LORA, MLA_HEADS * HEAD_PAD).astype(BF16)
    w['wb'] = jnp.concatenate([_swap_pairs(wq[..., QK_NOPE:]), zq], axis=-1).reshape(
        Q_LORA, MLA_HEADS * LANES).astype(BF16)
    wkv = p['w_ukv'].reshape(KV_LORA, MLA_HEADS, QK_NOPE + V_HEAD)
    w['wk'] = wkv[..., :QK_NOPE].reshape(KV_LORA, MLA_HEADS * QK_NOPE).astype(BF16)
    w['wv'] = wkv[..., QK_NOPE:].reshape(KV_LORA, MLA_HEADS * V_HEAD).astype(BF16)
    w['woa'] = p['w_o_mla'].astype(BF16)
    w['woh'] = p['w_o_hy'].astype(BF16)
    w['wout'] = p['w_out'].astype(BF16)
    wr_hi = p['w_router'].astype(BF16)
    wr_lo = (p['w_router'] - wr_hi.astype(F32)).astype(BF16)
    w['wr_pack'] = jnp.pad(jnp.concatenate([wr_hi, wr_lo], axis=1), ((0, 0), (0, LANES - 2 * N_EXPERTS)))
    w['wg'] = p['w_gate'].astype(BF16)
    w['wu'] = p['w_up'].astype(BF16)
    w['wd'] = p['w_down'].astype(BF16)
    w['hy_w'] = p['hy_short_w'].reshape(3, 3, HY_WIDTH).transpose(1, 0, 2)
    w['hy_b'] = p['hy_short_b'].reshape(3, 1, HY_WIDTH)
    return w


def _trunk(x3, mods, mod_row, tm_in, p, w, cache_kv, cache_kr, rope):
    b, seq, d = x3.shape
    n = b * seq
    x = x3.reshape(n, d)
    small, big = _inproj(x, mods, mod_row, w['w_in'], tm_in)
    cos128, sin128 = _rope_tables(seq) if rope else (None, None)
    q, ckv, kr = _qprep(small, p['q_norm_g'].reshape(1, -1), p['kv_norm_g'].reshape(1, -1),
                        w['wa'], w['wb'], cos128, sin128, seq, rope)
    ckv3 = ckv.reshape(b, seq, KV_LORA)
    kr3 = kr.reshape(b, seq, LANES)
    if cache_kv is None:
        kv_all, kr_all = ckv3, kr3
    else:
        kv_all = jnp.concatenate([cache_kv, ckv3], axis=1)
        kr_all = jnp.concatenate([jnp.pad(cache_kr, ((0, 0), (0, 0), (0, LANES - QK_ROPE))), kr3], axis=1)
    kh, vh = _kvup(kv_all, kr_all, w['wk'], w['wv'])
    attn = _attention(q.reshape(b, seq, MLA_HEADS * HEAD_PAD), kh, vh).reshape(n, MLA_HEADS * V_HEAD)

    nb = HY_TIME_BLOCKS if seq >= 512 else 1
    x0c, z, zn = _hypre(big.reshape(b, seq, BIG_W), w['hy_w'], w['hy_b'], nb)
    hf_td, hb_td, ext = _filter_td(seq, p, nb)
    cmat, smat = _dft_tables(seq // nb)
    kc, ks, kn = _kspec(cmat, smat, hf_td, hb_td, ext, nb)
    pc, ps = _hy_fwd(cmat, smat, z, kc, ks, nb)
    hy = _hy_inv(cmat, smat, pc, ps, z, x0c, zn, kn, p['hy_skip'].reshape(1, -1), nb).reshape(n, HY_WIDTH)

    merged = _merge(attn, hy, big, w['woa'], w['woh'])
    x1, h2, lg = _ln1(x, merged, mods, mod_row, w['wout'], p['ln1_g'].reshape(1, -1),
                      p['ln1_b'].reshape(1, -1), w['wr_pack'])
    cap = EC_CAPACITY * n // N_EXPERTS
    lg3 = lg[:, :N_EXPERTS].T.reshape(N_EXPERTS, n // LANES, LANES)
    idx, qpos, gate, start = _route(lg3, cap)
    pairs = _ffn(h2, idx.reshape(-1), qpos.reshape(-1), gate.reshape(-1, 1), w['wg'], w['wu'], w['wd'])
    y = _combine(x1, mods, mod_row, start.reshape(-1), pairs,
                 p['ln2_g'].reshape(1, -1), p['ln2_b'].reshape(1, -1))
    return y.reshape(b, seq, d), ckv3, kr3[..., :QK_ROPE]


def kernel(x_prompt, x_sample, cache_kv_c, cache_k_rope, c, c_ctx, w_ada, b_ada, w_in, q_norm_g, kv_norm_g, w_uq, w_ukv, w_o_mla, hy_short_w, hy_short_b, hy_filt_w1, hy_filt_b1, hy_filt_freq1, hy_filt_w2, hy_filt_b2, hy_filt_freq2, hy_filt_w3, hy_filt_b3, hy_skip, w_o_hy, w_out, ln1_g, ln1_b, ln2_g, ln2_b, w_router, w_gate, w_up, w_down):
    params = dict(w_in=w_in, q_norm_g=q_norm_g, kv_norm_g=kv_norm_g, w_uq=w_uq, w_ukv=w_ukv, w_o_mla=w_o_mla,
                  hy_short_w=hy_short_w, hy_short_b=hy_short_b, hy_filt_w1=hy_filt_w1, hy_filt_b1=hy_filt_b1,
                  hy_filt_freq1=hy_filt_freq1, hy_filt_w2=hy_filt_w2, hy_filt_b2=hy_filt_b2,
                  hy_filt_freq2=hy_filt_freq2, hy_filt_w3=hy_filt_w3, hy_filt_b3=hy_filt_b3, hy_skip=hy_skip,
                  w_o_hy=w_o_hy, w_out=w_out, ln1_g=ln1_g, ln1_b=ln1_b, ln2_g=ln2_g, ln2_b=ln2_b,
                  w_router=w_router, w_gate=w_gate, w_up=w_up, w_down=w_down)
    depth = w_in.shape[0]
    dec_b, dec_seq, d = x_sample.shape
    n_rows = 16
    cond = jnp.zeros((n_rows, d), F32).at[:dec_b].set(c).at[dec_b].set(c_ctx)
    y_prompt, y_sample = x_prompt, x_sample
    kv_list, kr_list = [], []
    for l in range(depth):
        p = {k: v[l] for k, v in params.items()}
        w = _prep_weights(p)
        mods = _ada_mod(cond, w_ada[l], b_ada[l]).reshape(n_rows, N_MOD, d)
        n_prompt = x_prompt.shape[0] * x_prompt.shape[1]
        y_prompt, c_kv, k_rope = _trunk(y_prompt, mods, lambda r: dec_b, min(1024, n_prompt), p, w,
                                        None, None, False)
        kv_list.append(c_kv)
        kr_list.append(k_rope)
        y_sample, _, _ = _trunk(y_sample, mods, lambda r: r // dec_seq, min(1024, dec_seq), p, w,
                                cache_kv_c[:, l], cache_k_rope[:, l], True)
    return (y_prompt, y_sample, jnp.stack(kv_list, axis=1), jnp.stack(kr_list, axis=1))
```

```python
import functools
import math

import numpy as np
import jax
import jax.numpy as jnp
from jax import lax
from jax.experimental import pallas as pl
from jax.experimental.pallas import tpu as pltpu

F32 = jnp.float32
BF16 = jnp.bfloat16
I32 = jnp.int32
HIGHEST = lax.Precision.HIGHEST

D_MODEL = 2048
GRID_W = 64
MLA_HEADS = 8
QK_NOPE = 128
QK_ROPE = 64
V_HEAD = 128
Q_LORA = 512
KV_LORA = 256
ROPE_THETA = 10000.0
ROPE_FREQS = QK_ROPE // 4
HY_WIDTH = 1024
HY_EMB = 33
HY_BANDS = (HY_EMB - 1) // 2
HY_HIDDEN = 64
HY_FAST_DECAY = 0.3
HY_SLOW_DECAY = 1.5
HY_TARGET = 0.01
N_EXPERTS = 16
EXPERT_FF = 1024
EC_CAPACITY = 2
EPS = 1e-6
DEPTH = 1
ALPHA = (2 * DEPTH) ** 0.25
N_MOD = 6

LANES = 128
HEAD_PAD = 256
SMALL_W = 1024
BIG_W = 2 * D_MODEL + 3 * HY_WIDTH
VMEM_LIMIT = 56 * 1024 * 1024
HY_TIME_BLOCKS = 4


def _cparams(sem, vmem=VMEM_LIMIT):
    return pltpu.CompilerParams(dimension_semantics=sem, vmem_limit_bytes=vmem)


def _sigmoid(x):
    return 1.0 / (1.0 + jnp.exp(-x))


def _const_spec(shape):
    nd = len(shape)
    return pl.BlockSpec(shape, lambda *_: (0,) * nd, pipeline_mode=pl.Buffered(1))


def _ada_kernel(c_ref, w_ref, b_ref, o_ref):
    c = c_ref[...]
    s = (c * _sigmoid(c)).astype(BF16)
    o_ref[...] = jnp.dot(s, w_ref[...].astype(BF16), preferred_element_type=F32) + b_ref[...]


def _ada_mod(cond, w_ada, b_ada):
    r, d = cond.shape
    n = w_ada.shape[1]
    tn = 512
    return pl.pallas_call(
        _ada_kernel,
        grid=(n // tn,),
        in_specs=[pl.BlockSpec((r, d), lambda j: (0, 0)),
                  pl.BlockSpec((d, tn), lambda j: (0, j)),
                  pl.BlockSpec((1, tn), lambda j: (0, j))],
        out_specs=pl.BlockSpec((r, tn), lambda j: (0, j)),
        out_shape=jax.ShapeDtypeStruct((r, n), F32),
        compiler_params=_cparams(("arbitrary",)),
        name="ada_mod",
    )(cond, w_ada, b_ada.reshape(1, n))


def _inproj_kernel(x_ref, mod_ref, w_ref, *rest, n_small, n_cast):
    cast_in = rest[:n_cast]
    small_ref, big_ref = rest[n_cast:n_cast + 2]
    cast_out = rest[n_cast + 2:2 * n_cast + 2]
    h_scr = rest[-1]
    for src, dst in zip(cast_in, cast_out):
        dst[...] = src[...].astype(BF16)
    j = pl.program_id(1)

    @pl.when(j == 0)
    def _():
        m = mod_ref[0]
        h_scr[...] = (x_ref[...] * (1.0 + m[1:2]) + m[0:1]).astype(BF16)

    acc = jnp.dot(h_scr[...], w_ref[...], preferred_element_type=F32)

    @pl.when(j < n_small)
    def _():
        small_ref[...] = acc

    @pl.when(j >= n_small)
    def _():
        big_ref[...] = acc.astype(BF16)


def _inproj(x, mods, mod_row, w_r, tm, cast=()):
    n, d = x.shape
    tn = 1024
    n_small = SMALL_W // tn
    n_cols = w_r.shape[1] // tn
    steps = (n // tm) * n_cols
    side = lambda a: pl.BlockSpec((a.shape[0] // steps, a.shape[1]), lambda i, j: (i * n_cols + j, 0))
    outs = pl.pallas_call(
        functools.partial(_inproj_kernel, n_small=n_small, n_cast=len(cast)),
        grid=(n // tm, n_cols),
        in_specs=[pl.BlockSpec((tm, d), lambda i, j: (i, 0)),
                  pl.BlockSpec((1, N_MOD, d), lambda i, j: (mod_row(i * tm), 0, 0)),
                  pl.BlockSpec((d, tn), lambda i, j: (0, j))] + [side(a) for a in cast],
        out_specs=[pl.BlockSpec((tm, tn), lambda i, j: (i, jnp.minimum(j, n_small - 1))),
                   pl.BlockSpec((tm, tn), lambda i, j: (i, jnp.maximum(j - n_small, 0)))]
        + [side(a) for a in cast],
        out_shape=[jax.ShapeDtypeStruct((n, SMALL_W), F32),
                   jax.ShapeDtypeStruct((n, BIG_W), BF16)]
        + [jax.ShapeDtypeStruct(a.shape, BF16) for a in cast],
        scratch_shapes=[pltpu.VMEM((tm, d), BF16)],
        compiler_params=_cparams(("arbitrary", "arbitrary")),
        name="inproj",
    )(x, mods, w_r, *cast)
    return outs[0], outs[1], tuple(outs[2:])


def _qprep_kernel(*refs, rope):
    if rope:
        small_ref, qg_ref, kvg_ref, wa_ref, wb_ref, c_ref, s_ref, q_ref, ckv_ref, kr_ref = refs
    else:
        small_ref, qg_ref, kvg_ref, wa_ref, q_ref, ckv_ref, kr_ref = refs
    qc = small_ref[:, 0:Q_LORA]
    kvc = small_ref[:, Q_LORA:Q_LORA + KV_LORA]
    kr = small_ref[:, Q_LORA + KV_LORA:Q_LORA + KV_LORA + LANES]
    qn = (qc * lax.rsqrt(jnp.mean(qc * qc, axis=-1, keepdims=True) + EPS) * qg_ref[...]).astype(BF16)
    ckv_ref[...] = kvc * lax.rsqrt(jnp.mean(kvc * kvc, axis=-1, keepdims=True) + EPS) * kvg_ref[...]
    qa = jnp.dot(qn, wa_ref[...], preferred_element_type=F32)
    if rope:
        krs = small_ref[:, Q_LORA + KV_LORA + LANES:SMALL_W]
        qb = jnp.dot(qn, wb_ref[...], preferred_element_type=F32)
        cs = c_ref[...]
        sn = s_ref[...]
        kr_ref[...] = kr * cs + krs * sn
        for h in range(MLA_HEADS):
            lo = h * HEAD_PAD
            q_ref[:, lo:lo + LANES] = qa[:, lo:lo + LANES].astype(BF16)
            q_ref[:, lo + LANES:lo + HEAD_PAD] = (
                qa[:, lo + LANES:lo + HEAD_PAD] * cs + qb[:, h * LANES:(h + 1) * LANES] * sn).astype(BF16)
    else:
        kr_ref[...] = kr
        q_ref[...] = qa.astype(BF16)


def _qprep(small, qg, kvg, wa, wb, cos128, sin128, seq, rope):
    n = small.shape[0]
    tm = min(512, seq)
    nblk = seq // tm
    in_specs = [pl.BlockSpec((tm, SMALL_W), lambda i: (i, 0)),
                pl.BlockSpec((1, Q_LORA), lambda i: (0, 0)),
                pl.BlockSpec((1, KV_LORA), lambda i: (0, 0)),
                pl.BlockSpec(wa.shape, lambda i: (0, 0))]
    args = [small, qg, kvg, wa]
    if rope:
        in_specs += [pl.BlockSpec(wb.shape, lambda i: (0, 0)),
                     pl.BlockSpec((tm, LANES), lambda i: (i % nblk, 0)),
                     pl.BlockSpec((tm, LANES), lambda i: (i % nblk, 0))]
        args += [wb, cos128, sin128]
    return pl.pallas_call(
        functools.partial(_qprep_kernel, rope=rope),
        grid=(n // tm,),
        in_specs=in_specs,
        out_specs=[pl.BlockSpec((tm, MLA_HEADS * HEAD_PAD), lambda i: (i, 0)),
                   pl.BlockSpec((tm, KV_LORA), lambda i: (i, 0)),
                   pl.BlockSpec((tm, LANES), lambda i: (i, 0))],
        out_shape=[jax.ShapeDtypeStruct((n, MLA_HEADS * HEAD_PAD), BF16),
                   jax.ShapeDtypeStruct((n, KV_LORA), F32),
                   jax.ShapeDtypeStruct((n, LANES), F32)],
        compiler_params=_cparams(("arbitrary",)),
        name="qprep",
    )(*args)


def _kvup_kernel(kv_ref, kr_ref, wk_ref, wv_ref, k_ref, v_ref):
    kv = kv_ref[0].astype(BF16)
    kn = jnp.dot(kv, wk_ref[...], preferred_element_type=F32)
    vv = jnp.dot(kv, wv_ref[...], preferred_element_type=F32)
    krp = kr_ref[0].astype(BF16)
    lane = lax.broadcasted_iota(I32, (kv.shape[0], LANES), 1)
    ones_col = jnp.where(lane == 0, 1.0, 0.0).astype(BF16)
    for h in range(MLA_HEADS):
        k_ref[0, h, :, 0:LANES] = kn[:, h * QK_NOPE:(h + 1) * QK_NOPE].astype(BF16)
        k_ref[0, h, :, LANES:HEAD_PAD] = krp
        v_ref[0, h, :, 0:V_HEAD] = vv[:, h * V_HEAD:(h + 1) * V_HEAD].astype(BF16)
        v_ref[0, h, :, V_HEAD:V_HEAD + LANES] = ones_col


def _kvup(kv_all, kr_all, wk, wv):
    b, lk, _ = kv_all.shape
    tm = 512 if lk % 512 == 0 else 256
    return pl.pallas_call(
        _kvup_kernel,
        grid=(b, lk // tm),
        in_specs=[pl.BlockSpec((1, tm, KV_LORA), lambda bi, i: (bi, i, 0)),
                  pl.BlockSpec((1, tm, LANES), lambda bi, i: (bi, i, 0)),
                  pl.BlockSpec(wk.shape, lambda bi, i: (0, 0)),
                  pl.BlockSpec(wv.shape, lambda bi, i: (0, 0))],
        out_specs=[pl.BlockSpec((1, MLA_HEADS, tm, HEAD_PAD), lambda bi, i: (bi, 0, i, 0)),
                   pl.BlockSpec((1, MLA_HEADS, tm, V_HEAD + LANES), lambda bi, i: (bi, 0, i, 0))],
        out_shape=[jax.ShapeDtypeStruct((b, MLA_HEADS, lk, HEAD_PAD), BF16),
                   jax.ShapeDtypeStruct((b, MLA_HEADS, lk, V_HEAD + LANES), BF16)],
        compiler_params=_cparams(("arbitrary", "arbitrary")),
        name="kvup",
    )(kv_all, kr_all, wk, wv)


def _attn_kernel(q_ref, k_ref, v_ref, o_ref, *, rows):
    for h in range(k_ref.shape[1]):
        k = k_ref[0, h]
        v = v_ref[0, h]
        for r0 in range(0, q_ref.shape[1], rows):
            q = q_ref[0, r0:r0 + rows, h * HEAD_PAD:(h + 1) * HEAD_PAD]
            s = lax.dot_general(q, k, (((1,), (1,)), ((), ())), preferred_element_type=F32)
            p = jnp.exp2(s - jnp.max(s, axis=-1, keepdims=True)).astype(BF16)
            o = jnp.dot(p, v, preferred_element_type=F32)
            o_ref[0, r0:r0 + rows, h * V_HEAD:(h + 1) * V_HEAD] = (
                o[:, 0:V_HEAD] / o[:, V_HEAD:V_HEAD + 1]).astype(BF16)


def _attention(q, k, v):
    b, l, _ = q.shape
    lk = k.shape[2]
    rows = 256
    chains = 8
    tq = min(l, chains * rows)
    hb = max(1, chains * rows // tq)
    return pl.pallas_call(
        functools.partial(_attn_kernel, rows=rows),
        grid=(b, MLA_HEADS // hb, l // tq),
        in_specs=[pl.BlockSpec((1, tq, hb * HEAD_PAD), lambda bi, h, i: (bi, i, h)),
                  pl.BlockSpec((1, hb, lk, HEAD_PAD), lambda bi, h, i: (bi, h, 0, 0)),
                  pl.BlockSpec((1, hb, lk, V_HEAD + LANES), lambda bi, h, i: (bi, h, 0, 0))],
        out_specs=pl.BlockSpec((1, tq, hb * V_HEAD), lambda bi, h, i: (bi, i, h)),
        out_shape=jax.ShapeDtypeStruct((b, l, MLA_HEADS * V_HEAD), BF16),
        compiler_params=_cparams(("arbitrary", "arbitrary", "arbitrary")),
        name="attention",
    )(q, k, v)


def _hypre_kernel(x0_ref, x1_ref, v_ref, w_ref, b_ref, x0o_ref, z_ref, zn_ref, *, nb):
    seq = x0_ref.shape[1]
    tb = seq // nb
    row = lax.broadcasted_iota(I32, (seq, 1), 0)

    def conv(u_ref, g):
        u = u_ref[0].astype(F32)
        up = jnp.where(row == 0, 0.0, pltpu.roll(u, 1, axis=0))
        un = jnp.where(row == seq - 1, 0.0, pltpu.roll(u, seq - 1, axis=0))
        w = w_ref[g]
        return up * w[0:1] + u * w[1:2] + un * w[2:3] + b_ref[g]

    x0o_ref[0] = conv(x0_ref, 0).astype(BF16)
    z = conv(x1_ref, 1) * conv(v_ref, 2)
    z_ref[0] = z.astype(BF16)
    zs = z * (1 - 2 * (row & 1)).astype(F32)
    for j in range(nb):
        zn_ref[0, j:j + 1, :] = jnp.sum(zs[j * tb:(j + 1) * tb], axis=0, keepdims=True)


def _hypre(big3, w3, b3, nb):
    b, l, _ = big3.shape
    cb = 256
    ncb = HY_WIDTH // cb
    off = 2 * D_MODEL // cb
    return pl.pallas_call(
        functools.partial(_hypre_kernel, nb=nb),
        grid=(b, ncb),
        in_specs=[pl.BlockSpec((1, l, cb), lambda bi, j: (bi, 0, off + j)),
                  pl.BlockSpec((1, l, cb), lambda bi, j: (bi, 0, off + j + ncb)),
                  pl.BlockSpec((1, l, cb), lambda bi, j: (bi, 0, off + j + 2 * ncb)),
                  pl.BlockSpec((3, 3, cb), lambda bi, j: (0, 0, j)),
                  pl.BlockSpec((3, 1, cb), lambda bi, j: (0, 0, j))],
        out_specs=[pl.BlockSpec((1, l, cb), lambda bi, j: (bi, 0, j)),
                   pl.BlockSpec((1, l, cb), lambda bi, j: (bi, 0, j)),
                   pl.BlockSpec((1, nb, cb), lambda bi, j: (bi, 0, j))],
        out_shape=[jax.ShapeDtypeStruct((b, l, HY_WIDTH), BF16),
                   jax.ShapeDtypeStruct((b, l, HY_WIDTH), BF16),
                   jax.ShapeDtypeStruct((b, nb, HY_WIDTH), F32)],
        compiler_params=_cparams(("arbitrary", "arbitrary")),
        name="hyena_pre",
    )(big3, big3, big3, w3, b3)


def _filter_kernel(feat_ref, w1_ref, b1_ref, f1_ref, w2_ref, b2_ref, f2_ref,
                   w3f_ref, w3b_ref, b3f_ref, b3b_ref, dl_ref, hf_ref, hb_ref, ext_ref, hd_scr, *, nb):
    seq = feat_ref.shape[0]
    tb = seq // nb

    @pl.when(pl.program_id(0) == 0)
    def _():
        h1 = jnp.sin(f1_ref[...] * (jnp.dot(feat_ref[...], w1_ref[...], precision=HIGHEST,
                                            preferred_element_type=F32) + b1_ref[...]))
        hd_scr[...] = jnp.sin(f2_ref[...] * (jnp.dot(h1, w2_ref[...], precision=HIGHEST,
                                                     preferred_element_type=F32) + b2_ref[...]))

    hd = hd_scr[...]
    row = lax.broadcasted_iota(I32, (seq, 1), 0)
    t = row.astype(F32) / float(max(seq - 1, 1))
    window = jnp.exp(-t * dl_ref[...])
    hf = (jnp.dot(hd, w3f_ref[...], precision=HIGHEST, preferred_element_type=F32) + b3f_ref[...]) * window
    hb = (jnp.dot(hd, w3b_ref[...], precision=HIGHEST, preferred_element_type=F32) + b3b_ref[...]) * window
    hb = jnp.where(row == 0, 0.0, hb)
    den = jnp.sum(jnp.abs(hf) + jnp.abs(hb), axis=0, keepdims=True) + EPS
    hf = hf / den
    hb = hb / den
    hf_ref[...] = hf.astype(BF16)
    hb_ref[...] = hb.astype(BF16)
    sign = (1 - 2 * (row & 1)).astype(F32)
    for j in range(nb):
        sl = slice(j * tb, (j + 1) * tb)
        ext_ref[0, j:j + 1, :] = hf[j * tb:j * tb + 1].astype(BF16).astype(F32)
        ext_ref[1, j:j + 1, :] = hb[j * tb:j * tb + 1].astype(BF16).astype(F32)
        ext_ref[2, j:j + 1, :] = jnp.sum(hf[sl] * sign[sl], axis=0, keepdims=True)
        ext_ref[3, j:j + 1, :] = jnp.sum(hb[sl] * sign[sl], axis=0, keepdims=True)


def _filter_td(seq, p, nb):
    pos = np.arange(seq, dtype=np.float64)
    t = pos / max(seq - 1, 1)
    bands = np.linspace(1e-4, HY_BANDS - 1, HY_BANDS)
    w = 2.0 * math.pi * pos / seq
    feats = np.concatenate([t[:, None], np.cos(w[:, None] * bands), -np.sin(w[:, None] * bands)], axis=-1)
    feats = jnp.asarray(np.pad(feats, ((0, 0), (0, LANES - HY_EMB))), F32)
    w1 = jnp.pad(p['hy_filt_w1'], ((0, LANES - HY_EMB), (0, 0)))
    deltas = jnp.asarray(np.abs(np.linspace(math.log(HY_TARGET) / HY_SLOW_DECAY, math.log(HY_TARGET) / HY_FAST_DECAY,
                                            HY_WIDTH)).reshape(1, HY_WIDTH), F32)
    cb = 256
    ncb = HY_WIDTH // cb
    row = lambda v: v.reshape(1, -1)
    c2 = lambda j: (0, 0)
    return pl.pallas_call(
        functools.partial(_filter_kernel, nb=nb),
        grid=(ncb,),
        in_specs=[pl.BlockSpec((seq, LANES), c2),
                  pl.BlockSpec((LANES, HY_HIDDEN), c2), pl.BlockSpec((1, HY_HIDDEN), c2),
                  pl.BlockSpec((1, HY_HIDDEN), c2),
                  pl.BlockSpec((HY_HIDDEN, HY_HIDDEN), c2), pl.BlockSpec((1, HY_HIDDEN), c2),
                  pl.BlockSpec((1, HY_HIDDEN), c2),
                  pl.BlockSpec((HY_HIDDEN, cb), lambda j: (0, j)),
                  pl.BlockSpec((HY_HIDDEN, cb), lambda j: (0, j + ncb)),
                  pl.BlockSpec((1, cb), lambda j: (0, j)),
                  pl.BlockSpec((1, cb), lambda j: (0, j + ncb)),
                  pl.BlockSpec((1, cb), lambda j: (0, j))],
        out_specs=[pl.BlockSpec((seq, cb), lambda j: (0, j)),
                   pl.BlockSpec((seq, cb), lambda j: (0, j)),
                   pl.BlockSpec((4, nb, cb), lambda j: (0, 0, j))],
        out_shape=[jax.ShapeDtypeStruct((seq, HY_WIDTH), BF16),
                   jax.ShapeDtypeStruct((seq, HY_WIDTH), BF16),
                   jax.ShapeDtypeStruct((4, nb, HY_WIDTH), F32)],
        scratch_shapes=[pltpu.VMEM((seq, HY_HIDDEN), F32)],
        compiler_params=_cparams(("arbitrary",)),
        name="hyena_filter",
    )(feats, w1, row(p['hy_filt_b1']), row(p['hy_filt_freq1']),
      p['hy_filt_w2'], row(p['hy_filt_b2']), row(p['hy_filt_freq2']),
      p['hy_filt_w3'], p['hy_filt_w3'], row(p['hy_filt_b3']), row(p['hy_filt_b3']), deltas)


def _kspec_kernel(c_ref, s_ref, hf_ref, hb_ref, ext_ref, kc_ref, ks_ref, kn_ref, *, nb):
    tb = c_ref.shape[0]
    f = lax.broadcasted_iota(I32, (tb, 1), 0)
    wgt = jnp.where(f == 0, 1.0, 2.0) / float(2 * tb)
    sig = (1 - 2 * (f & 1)).astype(F32)
    cm = c_ref[...]
    sm = s_ref[...]
    blk = lambda ref, j: ref[j * tb:(j + 1) * tb, :]
    chf = [jnp.dot(cm, blk(hf_ref, j), preferred_element_type=F32) for j in range(nb)]
    shf = [jnp.dot(sm, blk(hf_ref, j), preferred_element_type=F32) for j in range(nb)]
    chb = [jnp.dot(cm, blk(hb_ref, j), preferred_element_type=F32) for j in range(nb)]
    shb = [jnp.dot(sm, blk(hb_ref, j), preferred_element_type=F32) for j in range(nb)]
    first_f = lambda j: ext_ref[0, j:j + 1, :]
    first_b = lambda j: ext_ref[1, j:j + 1, :]
    alt_f = lambda j: ext_ref[2, j:j + 1, :]
    alt_b = lambda j: ext_ref[3, j:j + 1, :]
    for d in range(-(nb - 1), nb):
        k = d + nb - 1
        if d == 0:
            kc, ks, kn = chf[0] + chb[0], shf[0] - shb[0], alt_f(0) + alt_b(0)
        elif d > 0:
            kc = chf[d] + sig * (chf[d - 1] - first_f(d - 1))
            ks = shf[d] + sig * shf[d - 1]
            kn = alt_f(d) + alt_f(d - 1) - first_f(d - 1)
        else:
            e = -d
            kc = chb[e] + sig * (chb[e - 1] - first_b(e - 1))
            ks = -(shb[e] + sig * shb[e - 1])
            kn = alt_b(e) + alt_b(e - 1) - first_b(e - 1)
        kc_ref[k] = kc * wgt
        ks_ref[k] = ks * wgt
        kn_ref[k:k + 1, :] = kn * (1.0 / float(2 * tb))


def _kspec(cmat, smat, hf, hb, ext, nb):
    tb = cmat.shape[0]
    seq = hf.shape[0]
    nd = 2 * nb - 1
    cb = 256 if nb > 1 else 512
    return pl.pallas_call(
        functools.partial(_kspec_kernel, nb=nb),
        grid=(HY_WIDTH // cb,),
        in_specs=[pl.BlockSpec((tb, tb), lambda j: (0, 0)),
                  pl.BlockSpec((tb, tb), lambda j: (0, 0)),
                  pl.BlockSpec((seq, cb), lambda j: (0, j)),
                  pl.BlockSpec((seq, cb), lambda j: (0, j)),
                  pl.BlockSpec((4, nb, cb), lambda j: (0, 0, j))],
        out_specs=[pl.BlockSpec((nd, tb, cb), lambda j: (0, 0, j)),
                   pl.BlockSpec((nd, tb, cb), lambda j: (0, 0, j)),
                   pl.BlockSpec((nd, cb), lambda j: (0, j))],
        out_shape=[jax.ShapeDtypeStruct((nd, tb, HY_WIDTH), F32),
                   jax.ShapeDtypeStruct((nd, tb, HY_WIDTH), F32),
                   jax.ShapeDtypeStruct((nd, HY_WIDTH), F32)],
        compiler_params=_cparams(("arbitrary",)),
        name="hyena_kspec",
    )(cmat, smat, hf, hb, ext)


def _fwd_kernel(c_ref, s_ref, z_ref, kc_ref, ks_ref, pc_ref, ps_ref, *, nb):
    tb = c_ref.shape[0]
    cm = c_ref[...]
    sm = s_ref[...]
    zc = [jnp.dot(cm, z_ref[0, j * tb:(j + 1) * tb, :], preferred_element_type=F32) for j in range(nb)]
    zs = [jnp.dot(sm, z_ref[0, j * tb:(j + 1) * tb, :], preferred_element_type=F32) for j in range(nb)]
    for i in range(nb):
        pc = None
        ps = None
        for j in range(nb):
            kc = kc_ref[i - j + nb - 1]
            ks = ks_ref[i - j + nb - 1]
            tc = zc[j] * kc - zs[j] * ks
            ts = zc[j] * ks + zs[j] * kc
            pc = tc if pc is None else pc + tc
            ps = ts if ps is None else ps + ts
        pc_ref[0, i * tb:(i + 1) * tb, :] = pc.astype(BF16)
        ps_ref[0, i * tb:(i + 1) * tb, :] = ps.astype(BF16)


def _hy_fwd(cmat, smat, z, kc, ks, nb):
    b, seq, _ = z.shape
    tb = seq // nb
    nd = 2 * nb - 1
    cb = 256 if nb > 1 else HY_WIDTH
    return pl.pallas_call(
        functools.partial(_fwd_kernel, nb=nb),
        grid=(HY_WIDTH // cb, b),
        in_specs=[pl.BlockSpec((tb, tb), lambda j, bi: (0, 0)),
                  pl.BlockSpec((tb, tb), lambda j, bi: (0, 0)),
                  pl.BlockSpec((1, seq, cb), lambda j, bi: (bi, 0, j)),
                  pl.BlockSpec((nd, tb, cb), lambda j, bi: (0, 0, j)),
                  pl.BlockSpec((nd, tb, cb), lambda j, bi: (0, 0, j))],
        out_specs=[pl.BlockSpec((1, seq, cb), lambda j, bi: (bi, 0, j)),
                   pl.BlockSpec((1, seq, cb), lambda j, bi: (bi, 0, j))],
        out_shape=[jax.ShapeDtypeStruct((b, seq, HY_WIDTH), BF16),
                   jax.ShapeDtypeStruct((b, seq, HY_WIDTH), BF16)],
        compiler_params=_cparams(("arbitrary", "arbitrary")),
        name="hyena_fwd",
    )(cmat, smat, z, kc, ks)


def _inv_kernel(c_ref, s_ref, pc_ref, ps_ref, z_ref, x0_ref, zn_ref, kn_ref, skip_ref, o_ref, *, nb):
    tb = c_ref.shape[0]
    cm = c_ref[...]
    sm = s_ref[...]
    t = lax.broadcasted_iota(I32, (tb, 1), 0)
    sign = (1 - 2 * (t & 1)).astype(F32)
    for i in range(nb):
        rs = slice(i * tb, (i + 1) * tb)
        y = jnp.dot(cm, pc_ref[0, rs, :], preferred_element_type=F32)
        y = y + jnp.dot(sm, ps_ref[0, rs, :], preferred_element_type=F32)
        nyq = None
        for j in range(nb):
            k = i - j + nb - 1
            term = zn_ref[0, j:j + 1, :] * kn_ref[k:k + 1, :]
            nyq = term if nyq is None else nyq + term
        y = y + sign * nyq + z_ref[0, rs, :].astype(F32) * skip_ref[...]
        o_ref[0, rs, :] = (x0_ref[0, rs, :].astype(F32) * y).astype(BF16)


def _hy_inv(cmat, smat, pc, ps, z, x0, zn, kn, skip, nb):
    b, seq, _ = z.shape
    tb = seq // nb
    nd = 2 * nb - 1
    cb = 256 if nb > 1 else HY_WIDTH
    blk = pl.BlockSpec((1, seq, cb), lambda j, bi: (bi, 0, j))
    return pl.pallas_call(
        functools.partial(_inv_kernel, nb=nb),
        grid=(HY_WIDTH // cb, b),
        in_specs=[pl.BlockSpec((tb, tb), lambda j, bi: (0, 0)),
                  pl.BlockSpec((tb, tb), lambda j, bi: (0, 0)),
                  blk, blk, blk, blk,
                  pl.BlockSpec((1, nb, cb), lambda j, bi: (bi, 0, j)),
                  pl.BlockSpec((nd, cb), lambda j, bi: (0, j)),
                  pl.BlockSpec((1, cb), lambda j, bi: (0, j))],
        out_specs=blk,
        out_shape=jax.ShapeDtypeStruct((b, seq, HY_WIDTH), BF16),
        compiler_params=_cparams(("arbitrary", "arbitrary")),
        name="hyena_inv",
    )(cmat, smat, pc, ps, z, x0, zn, kn, skip)


def _layer_norm(v, g, b):
    mu = jnp.mean(v, axis=-1, keepdims=True)
    vc = v - mu
    var = jnp.mean(vc * vc, axis=-1, keepdims=True)
    return vc * lax.rsqrt(var + EPS) * g + b


def _merge_kernel(at_ref, hy_ref, ga_ref, gb_ref, woa_ref, woh_ref, o_ref):
    a = jnp.dot(at_ref[...], woa_ref[...], preferred_element_type=F32)
    hh = jnp.dot(hy_ref[...], woh_ref[...], preferred_element_type=F32)
    merged = _sigmoid(ga_ref[...].astype(F32)) * a + _sigmoid(gb_ref[...].astype(F32)) * hh
    o_ref[...] = merged.astype(BF16)


def _merge(attn, hy, big, woa, woh):
    n = attn.shape[0]
    d = woa.shape[1]
    tm = 512
    return pl.pallas_call(
        _merge_kernel,
        grid=(n // tm,),
        in_specs=[pl.BlockSpec((tm, attn.shape[1]), lambda i: (i, 0)),
                  pl.BlockSpec((tm, hy.shape[1]), lambda i: (i, 0)),
                  pl.BlockSpec((tm, d), lambda i: (i, 0)),
                  pl.BlockSpec((tm, d), lambda i: (i, 1)),
                  _const_spec(woa.shape), _const_spec(woh.shape)],
        out_specs=pl.BlockSpec((tm, d), lambda i: (i, 0)),
        out_shape=jax.ShapeDtypeStruct((n, d), BF16),
        compiler_params=_cparams(("arbitrary",)),
        name="merge",
    )(attn, hy, big, big, woa, woh)


def _ln1_kernel(x_ref, mg_ref, mod_ref, wout_ref, g_ref, b_ref, wr_ref, x1_ref, h2_ref, lg_ref, *, n_sub):
    m = mod_ref[0]
    rows = x_ref.shape[0] // n_sub
    for t in range(n_sub):
        rs = slice(t * rows, (t + 1) * rows)
        mix = jnp.dot(mg_ref[rs, :], wout_ref[...], preferred_element_type=F32)
        x1 = _layer_norm(ALPHA * x_ref[rs, :] + m[2:3] * mix, g_ref[...], b_ref[...])
        x1_ref[rs, :] = x1
        h2 = x1 * (1.0 + m[4:5]) + m[3:4]
        h2_ref[rs, :] = h2
        hi = h2.astype(BF16)
        lo = (h2 - hi.astype(F32)).astype(BF16)
        lg = (jnp.dot(hi, wr_ref[...], preferred_element_type=F32)
              + jnp.dot(lo, wr_ref[...], preferred_element_type=F32))
        lg_ref[rs, :] = lg + pltpu.roll(lg, LANES - N_EXPERTS, axis=1)


def _ln1(x, merged, mods, mod_row, wout, g, b, wr_pack):
    n, d = x.shape
    tm = 512
    return pl.pallas_call(
        functools.partial(_ln1_kernel, n_sub=2),
        grid=(n // tm,),
        in_specs=[pl.BlockSpec((tm, d), lambda i: (i, 0)),
                  pl.BlockSpec((tm, d), lambda i: (i, 0)),
                  pl.BlockSpec((1, N_MOD, d), lambda i: (mod_row(i * tm), 0, 0)),
                  _const_spec(wout.shape), _const_spec((1, d)), _const_spec((1, d)),
                  _const_spec(wr_pack.shape)],
        out_specs=[pl.BlockSpec((tm, d), lambda i: (i, 0)),
                   pl.BlockSpec((tm, d), lambda i: (i, 0)),
                   pl.BlockSpec((tm, LANES), lambda i: (i, 0))],
        out_shape=[jax.ShapeDtypeStruct((n, d), F32),
                   jax.ShapeDtypeStruct((n, d), F32),
                   jax.ShapeDtypeStruct((n, LANES), F32)],
        compiler_params=_cparams(("arbitrary",)),
        name="ln1_router",
    )(x, merged, mods, wout, g, b, wr_pack)


def _split3(t):
    t1 = t.astype(BF16)
    r = t - t1.astype(F32)
    t2 = r.astype(BF16)
    t3 = (r - t2.astype(F32)).astype(BF16)
    return t1, t2, t3


def _dot_exact(sel01, table):
    t1, t2, t3 = _split3(table)
    out = jnp.dot(sel01, t1, preferred_element_type=F32)
    out = out + jnp.dot(sel01, t2, preferred_element_type=F32)
    return out + jnp.dot(sel01, t3, preferred_element_type=F32)


def _route_kernel(lg_ref, idx_ref, q_ref, gate_ref, start_ref, aff_scr, sel_scr, pos_scr, *, cap):
    ne, nc, _ = lg_ref.shape
    lg = lg_ref[...]
    mx = jnp.max(lg, axis=0, keepdims=True)
    ex = jnp.exp(lg - mx)
    aff = ex / jnp.sum(ex, axis=0, keepdims=True)
    aff_scr[...] = aff

    def tbody(k, thr):
        cand = thr | lax.shift_left(jnp.int32(1), 30 - k)
        ge = jnp.where(aff >= lax.bitcast_convert_type(cand, F32), 1.0, 0.0)
        cnt = jnp.sum(jnp.sum(ge, axis=1, keepdims=True), axis=2, keepdims=True)
        return jnp.where(cnt >= float(cap), cand, thr)

    thr = lax.fori_loop(0, 31, tbody, jnp.zeros((ne, 1, 1), I32))
    lo = lax.bitcast_convert_type(thr, F32)
    above = aff >= lax.bitcast_convert_type(thr + 1, F32)
    sel_scr[...] = jnp.where(above, 1.0, 0.0)
    pos_scr[...] = jnp.where((aff >= lo) & jnp.logical_not(above), aff, -1.0)

    r_i = lax.broadcasted_iota(I32, (LANES, LANES), 0)
    c_i = lax.broadcasted_iota(I32, (LANES, LANES), 1)
    upper_incl = jnp.where(r_i <= c_i, 1.0, 0.0).astype(BF16)
    ones_sq = jnp.ones((LANES, LANES), BF16)
    rr_i = lax.broadcasted_iota(I32, (nc, nc), 0)
    cc_i = lax.broadcasted_iota(I32, (nc, nc), 1)
    lower_strict = jnp.where(cc_i < rr_i, 1.0, 0.0).astype(BF16)
    upper_rows = jnp.where(rr_i <= cc_i, 1.0, 0.0).astype(BF16)

    def prefix(x01):
        xb = x01.astype(BF16)
        p1 = jnp.dot(xb, upper_incl, preferred_element_type=F32)
        totb = jnp.dot(xb, ones_sq, preferred_element_type=F32)
        offs = jnp.dot(lower_strict, totb.astype(BF16), preferred_element_type=F32)
        return p1, offs

    tok = (lax.broadcasted_iota(I32, (nc, LANES), 0) * LANES
           + lax.broadcasted_iota(I32, (nc, LANES), 1)).astype(F32)

    def reduce2(fn, v):
        return fn(fn(v, axis=0, keepdims=True), axis=1, keepdims=True)

    def pass1(e, start):
        above_e = sel_scr[e]
        need = cap - jnp.sum(above_e).astype(I32)

        def pick(_, carry):
            vals, picked = carry
            first = reduce2(jnp.min, jnp.where(vals == reduce2(jnp.max, vals), tok, float(nc * LANES)))
            hit = tok == first
            return jnp.where(hit, -1.0, vals), picked + jnp.where(hit, 1.0, 0.0)

        _, picked = lax.fori_loop(0, need, pick, (pos_scr[e], jnp.zeros((nc, LANES), F32)))
        sel = above_e + picked
        sel_scr[e] = sel
        s1, soffs = prefix(sel)
        return start + (s1 + soffs - sel)

    start = lax.fori_loop(0, ne, pass1, jnp.zeros((nc, LANES), F32))
    start_ref[...] = start.astype(I32)

    p_col = lax.broadcasted_iota(I32, (cap, 1), 0).astype(F32)
    lane_row = lax.broadcasted_iota(I32, (1, LANES), 1).astype(F32)
    chunk_row = lax.broadcasted_iota(I32, (1, nc), 1).astype(F32)
    ones8 = jnp.ones((8, LANES), BF16)

    def row_sums(m):
        ones = jnp.ones((8, m.shape[1]), BF16)
        nt = lambda part: lax.dot_general(ones, part, (((1,), (1,)), ((), ())), preferred_element_type=F32)
        m1, m2, m3 = _split3(m)
        return (nt(m1) + nt(m2) + nt(m3))[0:1]

    def pass2(e, before):
        sel = sel_scr[e]
        selb = sel.astype(BF16)
        p1, offs = prefix(sel)
        tot_row = lax.dot_general(ones8, selb, (((1,), (1,)), ((), ())), preferred_element_type=F32)
        cum_row = jnp.dot(tot_row.astype(BF16), upper_rows, preferred_element_type=F32)
        cum1 = cum_row[0:1]
        prev1 = cum1 - tot_row[0:1]
        in_chunk = (prev1 <= p_col) & (p_col < cum1)
        ohc = jnp.where(in_chunk, 1.0, 0.0).astype(BF16)
        keyg = jnp.dot(ohc, (p1 * sel).astype(BF16), preferred_element_type=F32)
        offg = _dot_exact(ohc, offs)
        ohl = keyg == (p_col - offg + 1.0)
        l_p = row_sums(jnp.where(ohl, lane_row, 0.0))
        c_p = row_sums(jnp.where(in_chunk, chunk_row, 0.0))
        idx_ref[e] = (c_p * float(LANES) + l_p).astype(I32)
        qtab = _dot_exact(ohc, start) + jnp.dot(ohc, before.astype(BF16), preferred_element_type=F32)
        q_ref[e] = row_sums(jnp.where(ohl, qtab, 0.0)).astype(I32)
        afg = _dot_exact(ohc, aff_scr[e])
        gate_ref[e] = row_sums(jnp.where(ohl, afg, 0.0))
        return before + sel

    lax.fori_loop(0, ne, pass2, jnp.zeros((nc, LANES), F32))


def _route(lg3, cap):
    ne, nc, _ = lg3.shape
    full = lambda s: pl.BlockSpec(s, lambda: (0,) * len(s))
    return pl.pallas_call(
        functools.partial(_route_kernel, cap=cap),
        in_specs=[full(lg3.shape)],
        out_specs=[full((ne, 1, cap)), full((ne, 1, cap)), full((ne, 1, cap)), full((nc, LANES))],
        out_shape=[jax.ShapeDtypeStruct((ne, 1, cap), I32),
                   jax.ShapeDtypeStruct((ne, 1, cap), I32),
                   jax.ShapeDtypeStruct((ne, 1, cap), F32),
                   jax.ShapeDtypeStruct((nc, LANES), I32)],
        scratch_shapes=[pltpu.VMEM((ne, nc, LANES), F32),
                        pltpu.VMEM((ne, nc, LANES), F32),
                        pltpu.VMEM((ne, nc, LANES), F32)],
        compiler_params=pltpu.CompilerParams(vmem_limit_bytes=VMEM_LIMIT),
        name="route",
    )(lg3)


def _ffn_kernel(idx_ref, q_ref, gate_ref, wg_ref, wu_ref, wd_ref, h_hbm, y_hbm,
                xb, xa, yb, ya, gsem, ssem, *, tm, npair, total):
    step = pl.program_id(0) * npair + pl.program_id(1)
    t0 = 2 * step
    t1 = t0 + 1

    def gather_row(base, r, buf, s):
        return pltpu.make_async_copy(h_hbm.at[pl.ds(idx_ref[base + r], 1), :], buf.at[pl.ds(r, 1), :], gsem.at[s])

    def scatter_row(base, r, buf, s):
        return pltpu.make_async_copy(buf.at[pl.ds(r, 1), :], y_hbm.at[pl.ds(q_ref[base + r], 1), :], ssem.at[s])

    def wait_gather(buf, s):
        pltpu.make_async_copy(h_hbm.at[pl.ds(0, tm), :], buf, gsem.at[s]).wait()

    def wait_scatter(buf, s):
        pltpu.make_async_copy(buf, y_hbm.at[pl.ds(0, tm), :], ssem.at[s]).wait()

    n_chunk = 4
    fc = EXPERT_FF // n_chunk
    per = tm // n_chunk

    def compute(x_buf, y_buf, r0, gather_base, gather_buf, scatter_base, scatter_buf, s):
        x = x_buf[...].astype(BF16)
        gate = gate_ref[r0:r0 + tm, :]
        for c in range(n_chunk):
            for r in range(c * per, (c + 1) * per):
                gather_row(gather_base, r, gather_buf, s).start()
            for r in range(c * per, (c + 1) * per):
                scatter_row(scatter_base, r, scatter_buf, s).start()
            g = jnp.dot(x, wg_ref[0, :, c * fc:(c + 1) * fc], preferred_element_type=F32)
            u = jnp.dot(x, wu_ref[0, :, c * fc:(c + 1) * fc], preferred_element_type=F32)
            hid = (g * _sigmoid(g) * u * gate).astype(BF16)
            part = jnp.dot(hid, wd_ref[0, c * fc:(c + 1) * fc, :], preferred_element_type=F32)
            if c == 0:
                y_buf[...] = part
            else:
                y_buf[...] += part

    @pl.when(step == 0)
    def _():
        yb[...] = jnp.zeros(yb.shape, F32)

        def body(r, carry):
            gather_row(0, r, xa, 0).start()
            return carry

        lax.fori_loop(0, tm, body, 0)

    wait_gather(xa, 0)

    @pl.when(step >= 1)
    def _():
        wait_scatter(ya, 0)

    base1 = t1 * tm
    pbase = jnp.maximum(t0 - 1, 0) * tm
    compute(xa, ya, 0, base1, xb, pbase, yb, 1)

    wait_gather(xb, 1)
    wait_scatter(yb, 1)
    nbase = jnp.minimum(t1 + 1, total - 1) * tm
    base0 = t0 * tm
    compute(xb, yb, tm, nbase, xa, base0, ya, 0)

    @pl.when(t1 == total - 1)
    def _():
        wait_gather(xa, 0)
        wait_scatter(ya, 0)

        def body(r, carry):
            scatter_row(base1, r, yb, 1).start()
            return carry

        lax.fori_loop(0, tm, body, 0)
        wait_scatter(yb, 1)


def _ffn(h2, idx, qpos, gate, wg, wu, wd):
    n, d = h2.shape
    ne = wg.shape[0]
    cap = idx.shape[0] // ne
    tm = min(512, cap // 2)
    npair = cap // (2 * tm)
    total = 2 * ne * npair
    grid_spec = pltpu.PrefetchScalarGridSpec(
        num_scalar_prefetch=2,
        grid=(ne, npair),
        in_specs=[pl.BlockSpec((2 * tm, 1), lambda e, i, *_: (e * npair + i, 0)),
                  pl.BlockSpec((1, d, EXPERT_FF), lambda e, i, *_: (e, 0, 0)),
                  pl.BlockSpec((1, d, EXPERT_FF), lambda e, i, *_: (e, 0, 0)),
                  pl.BlockSpec((1, EXPERT_FF, d), lambda e, i, *_: (e, 0, 0)),
                  pl.BlockSpec(memory_space=pl.ANY)],
        out_specs=pl.BlockSpec(memory_space=pl.ANY),
        scratch_shapes=[pltpu.VMEM((tm, d), F32), pltpu.VMEM((tm, d), F32),
                        pltpu.VMEM((tm, d), F32), pltpu.VMEM((tm, d), F32),
                        pltpu.SemaphoreType.DMA((2,)),
                        pltpu.SemaphoreType.DMA((2,))],
    )
    return pl.pallas_call(
        functools.partial(_ffn_kernel, tm=tm, npair=npair, total=total),
        grid_spec=grid_spec,
        out_shape=jax.ShapeDtypeStruct((ne * cap, d), F32),
        compiler_params=_cparams(("arbitrary", "arbitrary")),
        name="expert_ffn",
    )(idx, qpos, gate, wg, wu, wd, h2)


def _combine_kernel(cs_ref, x1_ref, mod_ref, st_ref, en_ref, g_ref, b_ref, y_hbm, o_ref,
                    ybuf, acc, sem, *, win, total, nsteps):
    i = pl.program_id(0)
    slot = i % 2
    lo0 = cs_ref[i]
    hi = cs_ref[i + 1]
    adv = win - 8

    def win_start(lo):
        return pl.multiple_of((jnp.minimum(lo, total - win) // 8) * 8, 8)

    def window(lo, sl):
        return pltpu.make_async_copy(y_hbm.at[pl.ds(win_start(lo), win), :], ybuf.at[sl], sem.at[sl])

    @pl.when(i == 0)
    def _():
        window(lo0, 0).start()

    @pl.when(i + 1 < nsteps)
    def _():
        window(hi, 1 - slot).start()

    ntrip = jnp.maximum((hi - lo0 + adv - 1) // adv, 1)
    acc[...] = jnp.zeros_like(acc)
    st = st_ref[...]
    en = en_ref[...]

    def body(k, carry):
        lo = lo0 + k * adv

        @pl.when(k > 0)
        def _():
            window(lo, slot).start()

        window(lo, slot).wait()
        qj = win_start(lo) + lax.broadcasted_iota(I32, (1, win), 1)
        first = jnp.maximum(st, lo)
        last = jnp.minimum(en, lo + adv)
        seg = (qj >= first) & (qj < last)
        acc[...] += jnp.dot(jnp.where(seg, 1.0, 0.0).astype(BF16), ybuf[slot].astype(BF16),
                            preferred_element_type=F32)
        return carry

    lax.fori_loop(0, ntrip, body, 0)
    m = mod_ref[0]
    o_ref[...] = _layer_norm(ALPHA * x1_ref[...] + m[5:6] * acc[...], g_ref[...], b_ref[...])


def _combine(x1, mods, mod_row, start, pairs, g, b):
    n, d = x1.shape
    total = pairs.shape[0]
    tc = 256
    win = 640
    st = start.reshape(n, 1)
    en = jnp.concatenate([start[1:], jnp.full((1,), total, I32)]).reshape(n, 1)
    cs = jnp.concatenate([start[::tc], jnp.full((1,), total, I32)])
    grid_spec = pltpu.PrefetchScalarGridSpec(
        num_scalar_prefetch=1,
        grid=(n // tc,),
        in_specs=[pl.BlockSpec((tc, d), lambda i, *_: (i, 0)),
                  pl.BlockSpec((1, N_MOD, d), lambda i, *_: (mod_row(i * tc), 0, 0)),
                  pl.BlockSpec((tc, 1), lambda i, *_: (i, 0)),
                  pl.BlockSpec((tc, 1), lambda i, *_: (i, 0)),
                  pl.BlockSpec((1, d), lambda i, *_: (0, 0)),
                  pl.BlockSpec((1, d), lambda i, *_: (0, 0)),
                  pl.BlockSpec(memory_space=pl.ANY)],
        out_specs=pl.BlockSpec((tc, d), lambda i, *_: (i, 0)),
        scratch_shapes=[pltpu.VMEM((2, win, d), F32),
                        pltpu.VMEM((tc, d), F32),
                        pltpu.SemaphoreType.DMA((2,))],
    )
    return pl.pallas_call(
        functools.partial(_combine_kernel, win=win, total=total, nsteps=n // tc),
        grid_spec=grid_spec,
        out_shape=jax.ShapeDtypeStruct((n, d), F32),
        compiler_params=_cparams(("arbitrary",)),
        name="combine_ln2",
    )(cs, x1, mods, st, en, g, b, pairs)


def _swap_pairs(w):
    f = ROPE_FREQS
    return jnp.concatenate([w[..., f:2 * f], w[..., 0:f], w[..., 3 * f:4 * f], w[..., 2 * f:3 * f]], axis=-1)


def _rope_tables(seq):
    n_rows = seq // GRID_W
    row = np.repeat(np.arange(n_rows, dtype=np.float64), GRID_W)
    col = np.tile(np.arange(GRID_W, dtype=np.float64), n_rows)
    inv = ROPE_THETA ** (-np.arange(ROPE_FREQS, dtype=np.float64) / ROPE_FREQS)
    ar = row[:, None] * inv
    ac = col[:, None] * inv
    z = np.zeros((seq, LANES - QK_ROPE))
    cos128 = np.concatenate([np.cos(ar), np.cos(ar), np.cos(ac), np.cos(ac), z], axis=1)
    sin128 = np.concatenate([-np.sin(ar), np.sin(ar), -np.sin(ac), np.sin(ac), z], axis=1)
    return jnp.asarray(cos128, F32), jnp.asarray(sin128, F32)


def _dft_tables(seq):
    m = 2 * seq
    blk = 64
    s = np.arange(seq, dtype=np.int64)
    ang1 = ((np.arange(0, seq, blk, dtype=np.int64)[:, None] * s[None, :]) % m) * (2.0 * math.pi / m)
    ang0 = ((np.arange(blk, dtype=np.int64)[:, None] * s[None, :]) % m) * (2.0 * math.pi / m)
    c1, s1 = jnp.asarray(np.cos(ang1), F32)[:, None, :], jnp.asarray(np.sin(ang1), F32)[:, None, :]
    c0, s0 = jnp.asarray(np.cos(ang0), F32)[None, :, :], jnp.asarray(np.sin(ang0), F32)[None, :, :]
    cmat = (c1 * c0 - s1 * s0).reshape(seq, seq).astype(BF16)
    smat = (s1 * c0 + c1 * s0).reshape(seq, seq).astype(BF16)
    return cmat, smat


def _relayout_kernel(w_ref, o_ref, *, s1, s2, s3):
    chunk = 512
    width = w_ref.shape[1]
    f = ROPE_FREQS
    o_ref[:, 0:s1] = w_ref[:, 0:s1].astype(BF16)
    x = w_ref[:, s1:s1 + LANES]
    lane = lax.broadcasted_iota(I32, x.shape, 1)
    keep = lane < (s2 - s1)
    partner = jnp.where((lane & f) == 0, pltpu.roll(x, LANES - f, axis=1), pltpu.roll(x, f, axis=1))
    o_ref[:, s1:s1 + LANES] = jnp.where(keep, x, 0.0).astype(BF16)
    o_ref[:, s1 + LANES:SMALL_W] = jnp.where(keep, partner, 0.0).astype(BF16)
    dst = SMALL_W
    for lo, hi in ((s3, width), (s2, s3)):
        for c in range(lo, hi, chunk):
            n = min(chunk, hi - c)
            o_ref[:, dst:dst + n] = w_ref[:, c:c + n].astype(BF16)
            dst += n


def _relayout_w_in(w_in, s1, s2, s3):
    d, width = w_in.shape
    tr = 256
    return pl.pallas_call(
        functools.partial(_relayout_kernel, s1=s1, s2=s2, s3=s3),
        grid=(d // tr,),
        in_specs=[pl.BlockSpec((tr, width), lambda i: (i, 0))],
        out_specs=pl.BlockSpec((tr, SMALL_W + BIG_W), lambda i: (i, 0)),
        out_shape=jax.ShapeDtypeStruct((d, SMALL_W + BIG_W), BF16),
        compiler_params=_cparams(("arbitrary",)),
        name="w_in_relayout",
    )(w_in)


def _prep_weights(p):
    w = {}
    w_in = p['w_in']
    s0, s1, s2, s3, s4 = (Q_LORA, Q_LORA + KV_LORA, Q_LORA + KV_LORA + QK_ROPE,
                          Q_LORA + KV_LORA + QK_ROPE + 3 * HY_WIDTH,
                          Q_LORA + KV_LORA + QK_ROPE + 3 * HY_WIDTH + D_MODEL)
    w['w_in'] = _relayout_w_in(w_in, s1, s2, s3)
    scale = (QK_NOPE + QK_ROPE) ** -0.5 * math.log2(math.e)
    wq = (p['w_uq'] * scale).reshape(Q_LORA, MLA_HEADS, QK_NOPE + QK_ROPE)
    zq = jnp.zeros((Q_LORA, MLA_HEADS, HEAD_PAD - QK_NOPE - QK_ROPE), F32)
    w['wa'] = jnp.concatenate([wq, zq], axis=-1).reshape(Q_LORA, MLA_HEADS * HEAD_PAD).astype(BF16)
    w['wb'] = jnp.concatenate([_swap_pairs(wq[..., QK_NOPE:]), zq], axis=-1).reshape(
        Q_LORA, MLA_HEADS * LANES).astype(BF16)
    wkv = p['w_ukv'].reshape(KV_LORA, MLA_HEADS, QK_NOPE + V_HEAD)
    w['wk'] = wkv[..., :QK_NOPE].reshape(KV_LORA, MLA_HEADS * QK_NOPE).astype(BF16)
    w['wv'] = wkv[..., QK_NOPE:].reshape(KV_LORA, MLA_HEADS * V_HEAD).astype(BF16)
    w['woa'] = p['w_o_mla'].astype(BF16)
    w['woh'] = p['w_o_hy'].astype(BF16)
    w['wout'] = p['w_out'].astype(BF16)
    wr_hi = p['w_router'].astype(BF16)
    wr_lo = (p['w_router'] - wr_hi.astype(F32)).astype(BF16)
    w['wr_pack'] = jnp.pad(jnp.concatenate([wr_hi, wr_lo], axis=1), ((0, 0), (0, LANES - 2 * N_EXPERTS)))
    w['hy_w'] = p['hy_short_w'].reshape(3, 3, HY_WIDTH).transpose(1, 0, 2)
    w['hy_b'] = p['hy_short_b'].reshape(3, 1, HY_WIDTH)
    return w


def _trunk(x3, mods, mod_row, tm_in, p, w, cache_kv, cache_kr, rope, ffn_w):
    b, seq, d = x3.shape
    n = b * seq
    x = x3.reshape(n, d)
    if ffn_w is None:
        shapes = [p[k].shape for k in ('w_gate', 'w_up', 'w_down')]
        flat = [p[k].reshape(-1, p[k].shape[-1]) for k in ('w_gate', 'w_up', 'w_down')]
        small, big, cast = _inproj(x, mods, mod_row, w['w_in'], tm_in, flat)
        ffn_w = tuple(a.reshape(s) for a, s in zip(cast, shapes))
    else:
        small, big, _ = _inproj(x, mods, mod_row, w['w_in'], tm_in)
    cos128, sin128 = _rope_tables(seq) if rope else (None, None)
    q, ckv, kr = _qprep(small, p['q_norm_g'].reshape(1, -1), p['kv_norm_g'].reshape(1, -1),
                        w['wa'], w['wb'], cos128, sin128, seq, rope)
    ckv3 = ckv.reshape(b, seq, KV_LORA)
    kr3 = kr.reshape(b, seq, LANES)
    if cache_kv is None:
        kv_all, kr_all = ckv3, kr3
    else:
        kv_all = jnp.concatenate([cache_kv, ckv3], axis=1)
        kr_all = jnp.concatenate([jnp.pad(cache_kr, ((0, 0), (0, 0), (0, LANES - QK_ROPE))), kr3], axis=1)
    kh, vh = _kvup(kv_all, kr_all, w['wk'], w['wv'])
    attn = _attention(q.reshape(b, seq, MLA_HEADS * HEAD_PAD), kh, vh).reshape(n, MLA_HEADS * V_HEAD)

    nb = HY_TIME_BLOCKS if seq >= 512 else 1
    x0c, z, zn = _hypre(big.reshape(b, seq, BIG_W), w['hy_w'], w['hy_b'], nb)
    hf_td, hb_td, ext = _filter_td(seq, p, nb)
    cmat, smat = _dft_tables(seq // nb)
    kc, ks, kn = _kspec(cmat, smat, hf_td, hb_td, ext, nb)
    pc, ps = _hy_fwd(cmat, smat, z, kc, ks, nb)
    hy = _hy_inv(cmat, smat, pc, ps, z, x0c, zn, kn, p['hy_skip'].reshape(1, -1), nb).reshape(n, HY_WIDTH)

    merged = _merge(attn, hy, big, w['woa'], w['woh'])
    x1, h2, lg = _ln1(x, merged, mods, mod_row, w['wout'], p['ln1_g'].reshape(1, -1),
                      p['ln1_b'].reshape(1, -1), w['wr_pack'])
    cap = EC_CAPACITY * n // N_EXPERTS
    lg3 = lg[:, :N_EXPERTS].T.reshape(N_EXPERTS, n // LANES, LANES)
    idx, qpos, gate, start = _route(lg3, cap)
    pairs = _ffn(h2, idx.reshape(-1), qpos.reshape(-1), gate.reshape(-1, 1), *ffn_w)
    y = _combine(x1, mods, mod_row, start.reshape(-1), pairs,
                 p['ln2_g'].reshape(1, -1), p['ln2_b'].reshape(1, -1))
    return y.reshape(b, seq, d), ckv3, kr3[..., :QK_ROPE], ffn_w


def kernel(x_prompt, x_sample, cache_kv_c, cache_k_rope, c, c_ctx, w_ada, b_ada, w_in, q_norm_g, kv_norm_g, w_uq, w_ukv, w_o_mla, hy_short_w, hy_short_b, hy_filt_w1, hy_filt_b1, hy_filt_freq1, hy_filt_w2, hy_filt_b2, hy_filt_freq2, hy_filt_w3, hy_filt_b3, hy_skip, w_o_hy, w_out, ln1_g, ln1_b, ln2_g, ln2_b, w_router, w_gate, w_up, w_down):
    params = dict(w_in=w_in, q_norm_g=q_norm_g, kv_norm_g=kv_norm_g, w_uq=w_uq, w_ukv=w_ukv, w_o_mla=w_o_mla,
                  hy_short_w=hy_short_w, hy_short_b=hy_short_b, hy_filt_w1=hy_filt_w1, hy_filt_b1=hy_filt_b1,
                  hy_filt_freq1=hy_filt_freq1, hy_filt_w2=hy_filt_w2, hy_filt_b2=hy_filt_b2,
                  hy_filt_freq2=hy_filt_freq2, hy_filt_w3=hy_filt_w3, hy_filt_b3=hy_filt_b3, hy_skip=hy_skip,
                  w_o_hy=w_o_hy, w_out=w_out, ln1_g=ln1_g, ln1_b=ln1_b, ln2_g=ln2_g, ln2_b=ln2_b,
                  w_router=w_router, w_gate=w_gate, w_up=w_up, w_down=w_down)
    depth = w_in.shape[0]
    dec_b, dec_seq, d = x_sample.shape
    n_rows = 16
    cond = jnp.zeros((n_rows, d), F32).at[:dec_b].set(c).at[dec_b].set(c_ctx)
    y_prompt, y_sample = x_prompt, x_sample
    kv_list, kr_list = [], []
    for l in range(depth):
        p = {k: v[l] for k, v in params.items()}
        w = _prep_weights(p)
        mods = _ada_mod(cond, w_ada[l], b_ada[l]).reshape(n_rows, N_MOD, d)
        n_prompt = x_prompt.shape[0] * x_prompt.shape[1]
        y_sample, _, _, ffn_w = _trunk(y_sample, mods, lambda r: r // dec_seq, min(1024, dec_seq), p, w,
                                       cache_kv_c[:, l], cache_k_rope[:, l], True, None)
        y_prompt, c_kv, k_rope, _ = _trunk(y_prompt, mods, lambda r: dec_b, min(1024, n_prompt), p, w,
                                           None, None, False, ffn_w)
        kv_list.append(c_kv)
        kr_list.append(k_rope)
    return (y_prompt, y_sample, jnp.stack(kv_list, axis=1), jnp.stack(kr_list, axis=1))
```

```python
import functools
import math

import numpy as np
import jax
import jax.numpy as jnp
from jax import lax
from jax.experimental import pallas as pl
from jax.experimental.pallas import tpu as pltpu

F32 = jnp.float32
BF16 = jnp.bfloat16
I32 = jnp.int32
HIGHEST = lax.Precision.HIGHEST

D_MODEL = 2048
GRID_W = 64
MLA_HEADS = 8
QK_NOPE = 128
QK_ROPE = 64
V_HEAD = 128
Q_LORA = 512
KV_LORA = 256
ROPE_THETA = 10000.0
ROPE_FREQS = QK_ROPE // 4
HY_WIDTH = 1024
HY_EMB = 33
HY_BANDS = (HY_EMB - 1) // 2
HY_HIDDEN = 64
HY_FAST_DECAY = 0.3
HY_SLOW_DECAY = 1.5
HY_TARGET = 0.01
N_EXPERTS = 16
EXPERT_FF = 1024
EC_CAPACITY = 2
EPS = 1e-6
DEPTH = 1
ALPHA = (2 * DEPTH) ** 0.25
N_MOD = 6

LANES = 128
HEAD_PAD = 256
SMALL_W = 1024
BIG_W = 2 * D_MODEL + 3 * HY_WIDTH
VMEM_LIMIT = 56 * 1024 * 1024
HY_TIME_BLOCKS = 4


def _cparams(sem, vmem=VMEM_LIMIT):
    return pltpu.CompilerParams(dimension_semantics=sem, vmem_limit_bytes=vmem)


def _sigmoid(x):
    return 1.0 / (1.0 + jnp.exp(-x))


def _const_spec(shape):
    nd = len(shape)
    return pl.BlockSpec(shape, lambda *_: (0,) * nd, pipeline_mode=pl.Buffered(1))


def _ada_kernel(c_ref, w_ref, b_ref, o_ref):
    c = c_ref[...]
    s = (c * _sigmoid(c)).astype(BF16)
    o_ref[...] = jnp.dot(s, w_ref[...].astype(BF16), preferred_element_type=F32) + b_ref[...]


def _ada_mod(cond, w_ada, b_ada):
    r, d = cond.shape
    n = w_ada.shape[1]
    tn = 512
    return pl.pallas_call(
        _ada_kernel,
        grid=(n // tn,),
        in_specs=[pl.BlockSpec((r, d), lambda j: (0, 0)),
                  pl.BlockSpec((d, tn), lambda j: (0, j)),
                  pl.BlockSpec((1, tn), lambda j: (0, j))],
        out_specs=pl.BlockSpec((r, tn), lambda j: (0, j)),
        out_shape=jax.ShapeDtypeStruct((r, n), F32),
        compiler_params=_cparams(("arbitrary",)),
        name="ada_mod",
    )(cond, w_ada, b_ada.reshape(1, n))


def _inproj_kernel(x_ref, mod_ref, w_ref, *rest, n_small, n_cast):
    cast_in = rest[:n_cast]
    small_ref, big_ref = rest[n_cast:n_cast + 2]
    cast_out = rest[n_cast + 2:2 * n_cast + 2]
    h_scr = rest[-1]
    for src, dst in zip(cast_in, cast_out):
        dst[...] = src[...].astype(BF16)
    j = pl.program_id(1)

    @pl.when(j == 0)
    def _():
        m = mod_ref[0]
        h_scr[...] = (x_ref[...] * (1.0 + m[1:2]) + m[0:1]).astype(BF16)

    acc = jnp.dot(h_scr[...], w_ref[...], preferred_element_type=F32)

    @pl.when(j < n_small)
    def _():
        small_ref[...] = acc

    @pl.when(j >= n_small)
    def _():
        big_ref[...] = acc.astype(BF16)


def _inproj(x, mods, mod_row, w_r, tm, cast=()):
    n, d = x.shape
    tn = 1024
    n_small = SMALL_W // tn
    n_cols = w_r.shape[1] // tn
    steps = (n // tm) * n_cols
    side = lambda a: pl.BlockSpec((a.shape[0] // steps, a.shape[1]), lambda i, j: (i * n_cols + j, 0))
    outs = pl.pallas_call(
        functools.partial(_inproj_kernel, n_small=n_small, n_cast=len(cast)),
        grid=(n // tm, n_cols),
        in_specs=[pl.BlockSpec((tm, d), lambda i, j: (i, 0)),
                  pl.BlockSpec((1, N_MOD, d), lambda i, j: (mod_row(i * tm), 0, 0)),
                  pl.BlockSpec((d, tn), lambda i, j: (0, j))] + [side(a) for a in cast],
        out_specs=[pl.BlockSpec((tm, tn), lambda i, j: (i, jnp.minimum(j, n_small - 1))),
                   pl.BlockSpec((tm, tn), lambda i, j: (i, jnp.maximum(j - n_small, 0)))]
        + [side(a) for a in cast],
        out_shape=[jax.ShapeDtypeStruct((n, SMALL_W), F32),
                   jax.ShapeDtypeStruct((n, BIG_W), BF16)]
        + [jax.ShapeDtypeStruct(a.shape, BF16) for a in cast],
        scratch_shapes=[pltpu.VMEM((tm, d), BF16)],
        compiler_params=_cparams(("arbitrary", "arbitrary")),
        name="inproj",
    )(x, mods, w_r, *cast)
    return outs[0], outs[1], tuple(outs[2:])


def _qprep_kernel(*refs, rope):
    if rope:
        small_ref, qg_ref, kvg_ref, wa_ref, wb_ref, c_ref, s_ref, q_ref, ckv_ref, kr_ref = refs
    else:
        small_ref, qg_ref, kvg_ref, wa_ref, q_ref, ckv_ref, kr_ref = refs
    qc = small_ref[:, 0:Q_LORA]
    kvc = small_ref[:, Q_LORA:Q_LORA + KV_LORA]
    kr = small_ref[:, Q_LORA + KV_LORA:Q_LORA + KV_LORA + LANES]
    qn = (qc * lax.rsqrt(jnp.mean(qc * qc, axis=-1, keepdims=True) + EPS) * qg_ref[...]).astype(BF16)
    ckv_ref[...] = kvc * lax.rsqrt(jnp.mean(kvc * kvc, axis=-1, keepdims=True) + EPS) * kvg_ref[...]
    qa = jnp.dot(qn, wa_ref[...], preferred_element_type=F32)
    if rope:
        krs = small_ref[:, Q_LORA + KV_LORA + LANES:SMALL_W]
        qb = jnp.dot(qn, wb_ref[...], preferred_element_type=F32)
        cs = c_ref[...]
        sn = s_ref[...]
        kr_ref[...] = kr * cs + krs * sn
        for h in range(MLA_HEADS):
            lo = h * HEAD_PAD
            q_ref[:, lo:lo + LANES] = qa[:, lo:lo + LANES].astype(BF16)
            q_ref[:, lo + LANES:lo + HEAD_PAD] = (
                qa[:, lo + LANES:lo + HEAD_PAD] * cs + qb[:, h * LANES:(h + 1) * LANES] * sn).astype(BF16)
    else:
        kr_ref[...] = kr
        q_ref[...] = qa.astype(BF16)


def _qprep(small, qg, kvg, wa, wb, cos128, sin128, seq, rope):
    n = small.shape[0]
    tm = min(512, seq)
    nblk = seq // tm
    in_specs = [pl.BlockSpec((tm, SMALL_W), lambda i: (i, 0)),
                pl.BlockSpec((1, Q_LORA), lambda i: (0, 0)),
                pl.BlockSpec((1, KV_LORA), lambda i: (0, 0)),
                pl.BlockSpec(wa.shape, lambda i: (0, 0))]
    args = [small, qg, kvg, wa]
    if rope:
        in_specs += [pl.BlockSpec(wb.shape, lambda i: (0, 0)),
                     pl.BlockSpec((tm, LANES), lambda i: (i % nblk, 0)),
                     pl.BlockSpec((tm, LANES), lambda i: (i % nblk, 0))]
        args += [wb, cos128, sin128]
    return pl.pallas_call(
        functools.partial(_qprep_kernel, rope=rope),
        grid=(n // tm,),
        in_specs=in_specs,
        out_specs=[pl.BlockSpec((tm, MLA_HEADS * HEAD_PAD), lambda i: (i, 0)),
                   pl.BlockSpec((tm, KV_LORA), lambda i: (i, 0)),
                   pl.BlockSpec((tm, LANES), lambda i: (i, 0))],
        out_shape=[jax.ShapeDtypeStruct((n, MLA_HEADS * HEAD_PAD), BF16),
                   jax.ShapeDtypeStruct((n, KV_LORA), F32),
                   jax.ShapeDtypeStruct((n, LANES), F32)],
        compiler_params=_cparams(("arbitrary",)),
        name="qprep",
    )(*args)


def _kvup_kernel(kv_ref, kr_ref, wk_ref, wv_ref, k_ref, v_ref):
    kv = kv_ref[0].astype(BF16)
    kn = jnp.dot(kv, wk_ref[...], preferred_element_type=F32)
    vv = jnp.dot(kv, wv_ref[...], preferred_element_type=F32)
    krp = kr_ref[0].astype(BF16)
    lane = lax.broadcasted_iota(I32, (kv.shape[0], LANES), 1)
    ones_col = jnp.where(lane == 0, 1.0, 0.0).astype(BF16)
    for h in range(MLA_HEADS):
        k_ref[0, h, :, 0:LANES] = kn[:, h * QK_NOPE:(h + 1) * QK_NOPE].astype(BF16)
        k_ref[0, h, :, LANES:HEAD_PAD] = krp
        v_ref[0, h, :, 0:V_HEAD] = vv[:, h * V_HEAD:(h + 1) * V_HEAD].astype(BF16)
        v_ref[0, h, :, V_HEAD:V_HEAD + LANES] = ones_col


def _kvup(kv_all, kr_all, wk, wv):
    b, lk, _ = kv_all.shape
    tm = 512 if lk % 512 == 0 else 256
    return pl.pallas_call(
        _kvup_kernel,
        grid=(b, lk // tm),
        in_specs=[pl.BlockSpec((1, tm, KV_LORA), lambda bi, i: (bi, i, 0)),
                  pl.BlockSpec((1, tm, LANES), lambda bi, i: (bi, i, 0)),
                  pl.BlockSpec(wk.shape, lambda bi, i: (0, 0)),
                  pl.BlockSpec(wv.shape, lambda bi, i: (0, 0))],
        out_specs=[pl.BlockSpec((1, MLA_HEADS, tm, HEAD_PAD), lambda bi, i: (bi, 0, i, 0)),
                   pl.BlockSpec((1, MLA_HEADS, tm, V_HEAD + LANES), lambda bi, i: (bi, 0, i, 0))],
        out_shape=[jax.ShapeDtypeStruct((b, MLA_HEADS, lk, HEAD_PAD), BF16),
                   jax.ShapeDtypeStruct((b, MLA_HEADS, lk, V_HEAD + LANES), BF16)],
        compiler_params=_cparams(("arbitrary", "arbitrary")),
        name="kvup",
    )(kv_all, kr_all, wk, wv)


def _attn_kernel(q_ref, k_ref, v_ref, o_ref, *, rows):
    for h in range(k_ref.shape[1]):
        k = k_ref[0, h]
        v = v_ref[0, h]
        for r0 in range(0, q_ref.shape[1], rows):
            q = q_ref[0, r0:r0 + rows, h * HEAD_PAD:(h + 1) * HEAD_PAD]
            s = lax.dot_general(q, k, (((1,), (1,)), ((), ())), preferred_element_type=F32)
            p = jnp.exp2(s - jnp.max(s, axis=-1, keepdims=True)).astype(BF16)
            o = jnp.dot(p, v, preferred_element_type=F32)
            o_ref[0, r0:r0 + rows, h * V_HEAD:(h + 1) * V_HEAD] = (
                o[:, 0:V_HEAD] / o[:, V_HEAD:V_HEAD + 1]).astype(BF16)


def _attention(q, k, v):
    b, l, _ = q.shape
    lk = k.shape[2]
    rows = 256
    chains = 16
    tq = min(l, chains * rows)
    hb = min(MLA_HEADS, max(1, chains * rows // tq))
    return pl.pallas_call(
        functools.partial(_attn_kernel, rows=rows),
        grid=(b, MLA_HEADS // hb, l // tq),
        in_specs=[pl.BlockSpec((1, tq, hb * HEAD_PAD), lambda bi, h, i: (bi, i, h)),
                  pl.BlockSpec((1, hb, lk, HEAD_PAD), lambda bi, h, i: (bi, h, 0, 0)),
                  pl.BlockSpec((1, hb, lk, V_HEAD + LANES), lambda bi, h, i: (bi, h, 0, 0))],
        out_specs=pl.BlockSpec((1, tq, hb * V_HEAD), lambda bi, h, i: (bi, i, h)),
        out_shape=jax.ShapeDtypeStruct((b, l, MLA_HEADS * V_HEAD), BF16),
        compiler_params=_cparams(("arbitrary", "arbitrary", "arbitrary")),
        name="attention",
    )(q, k, v)


def _hypre_kernel(x0_ref, x1_ref, v_ref, w_ref, b_ref, x0o_ref, z_ref, zn_ref, *, nb):
    seq = x0_ref.shape[1]
    tb = seq // nb
    row = lax.broadcasted_iota(I32, (seq, 1), 0)

    def conv(u_ref, g):
        u = u_ref[0].astype(F32)
        up = jnp.where(row == 0, 0.0, pltpu.roll(u, 1, axis=0))
        un = jnp.where(row == seq - 1, 0.0, pltpu.roll(u, seq - 1, axis=0))
        w = w_ref[g]
        return up * w[0:1] + u * w[1:2] + un * w[2:3] + b_ref[g]

    x0o_ref[0] = conv(x0_ref, 0).astype(BF16)
    z = conv(x1_ref, 1) * conv(v_ref, 2)
    z_ref[0] = z.astype(BF16)
    zs = z * (1 - 2 * (row & 1)).astype(F32)
    for j in range(nb):
        zn_ref[0, j:j + 1, :] = jnp.sum(zs[j * tb:(j + 1) * tb], axis=0, keepdims=True)


def _hypre(big3, w3, b3, nb):
    b, l, _ = big3.shape
    cb = 256
    ncb = HY_WIDTH // cb
    off = 2 * D_MODEL // cb
    return pl.pallas_call(
        functools.partial(_hypre_kernel, nb=nb),
        grid=(b, ncb),
        in_specs=[pl.BlockSpec((1, l, cb), lambda bi, j: (bi, 0, off + j)),
                  pl.BlockSpec((1, l, cb), lambda bi, j: (bi, 0, off + j + ncb)),
                  pl.BlockSpec((1, l, cb), lambda bi, j: (bi, 0, off + j + 2 * ncb)),
                  pl.BlockSpec((3, 3, cb), lambda bi, j: (0, 0, j)),
                  pl.BlockSpec((3, 1, cb), lambda bi, j: (0, 0, j))],
        out_specs=[pl.BlockSpec((1, l, cb), lambda bi, j: (bi, 0, j)),
                   pl.BlockSpec((1, l, cb), lambda bi, j: (bi, 0, j)),
                   pl.BlockSpec((1, nb, cb), lambda bi, j: (bi, 0, j))],
        out_shape=[jax.ShapeDtypeStruct((b, l, HY_WIDTH), BF16),
                   jax.ShapeDtypeStruct((b, l, HY_WIDTH), BF16),
                   jax.ShapeDtypeStruct((b, nb, HY_WIDTH), F32)],
        compiler_params=_cparams(("arbitrary", "arbitrary")),
        name="hyena_pre",
    )(big3, big3, big3, w3, b3)


def _filter_kernel(feat_ref, w1_ref, b1_ref, f1_ref, w2_ref, b2_ref, f2_ref,
                   w3f_ref, w3b_ref, b3f_ref, b3b_ref, dl_ref, hf_ref, hb_ref, ext_ref, hd_scr, *, nb):
    seq = feat_ref.shape[0]
    tb = seq // nb

    @pl.when(pl.program_id(0) == 0)
    def _():
        h1 = jnp.sin(f1_ref[...] * (jnp.dot(feat_ref[...], w1_ref[...], precision=HIGHEST,
                                            preferred_element_type=F32) + b1_ref[...]))
        hd_scr[...] = jnp.sin(f2_ref[...] * (jnp.dot(h1, w2_ref[...], precision=HIGHEST,
                                                     preferred_element_type=F32) + b2_ref[...]))

    hd = hd_scr[...]
    row = lax.broadcasted_iota(I32, (seq, 1), 0)
    t = row.astype(F32) / float(max(seq - 1, 1))
    window = jnp.exp(-t * dl_ref[...])
    hf = (jnp.dot(hd, w3f_ref[...], precision=HIGHEST, preferred_element_type=F32) + b3f_ref[...]) * window
    hb = (jnp.dot(hd, w3b_ref[...], precision=HIGHEST, preferred_element_type=F32) + b3b_ref[...]) * window
    hb = jnp.where(row == 0, 0.0, hb)
    den = jnp.sum(jnp.abs(hf) + jnp.abs(hb), axis=0, keepdims=True) + EPS
    hf = hf / den
    hb = hb / den
    hf_ref[...] = hf.astype(BF16)
    hb_ref[...] = hb.astype(BF16)
    sign = (1 - 2 * (row & 1)).astype(F32)
    for j in range(nb):
        sl = slice(j * tb, (j + 1) * tb)
        ext_ref[0, j:j + 1, :] = hf[j * tb:j * tb + 1].astype(BF16).astype(F32)
        ext_ref[1, j:j + 1, :] = hb[j * tb:j * tb + 1].astype(BF16).astype(F32)
        ext_ref[2, j:j + 1, :] = jnp.sum(hf[sl] * sign[sl], axis=0, keepdims=True)
        ext_ref[3, j:j + 1, :] = jnp.sum(hb[sl] * sign[sl], axis=0, keepdims=True)


def _filter_td(seq, p, nb):
    pos = np.arange(seq, dtype=np.float64)
    t = pos / max(seq - 1, 1)
    bands = np.linspace(1e-4, HY_BANDS - 1, HY_BANDS)
    w = 2.0 * math.pi * pos / seq
    feats = np.concatenate([t[:, None], np.cos(w[:, None] * bands), -np.sin(w[:, None] * bands)], axis=-1)
    feats = jnp.asarray(np.pad(feats, ((0, 0), (0, LANES - HY_EMB))), F32)
    w1 = jnp.pad(p['hy_filt_w1'], ((0, LANES - HY_EMB), (0, 0)))
    deltas = jnp.asarray(np.abs(np.linspace(math.log(HY_TARGET) / HY_SLOW_DECAY, math.log(HY_TARGET) / HY_FAST_DECAY,
                                            HY_WIDTH)).reshape(1, HY_WIDTH), F32)
    cb = 256
    ncb = HY_WIDTH // cb
    row = lambda v: v.reshape(1, -1)
    c2 = lambda j: (0, 0)
    return pl.pallas_call(
        functools.partial(_filter_kernel, nb=nb),
        grid=(ncb,),
        in_specs=[pl.BlockSpec((seq, LANES), c2),
                  pl.BlockSpec((LANES, HY_HIDDEN), c2), pl.BlockSpec((1, HY_HIDDEN), c2),
                  pl.BlockSpec((1, HY_HIDDEN), c2),
                  pl.BlockSpec((HY_HIDDEN, HY_HIDDEN), c2), pl.BlockSpec((1, HY_HIDDEN), c2),
                  pl.BlockSpec((1, HY_HIDDEN), c2),
                  pl.BlockSpec((HY_HIDDEN, cb), lambda j: (0, j)),
                  pl.BlockSpec((HY_HIDDEN, cb), lambda j: (0, j + ncb)),
                  pl.BlockSpec((1, cb), lambda j: (0, j)),
                  pl.BlockSpec((1, cb), lambda j: (0, j + ncb)),
                  pl.BlockSpec((1, cb), lambda j: (0, j))],
        out_specs=[pl.BlockSpec((seq, cb), lambda j: (0, j)),
                   pl.BlockSpec((seq, cb), lambda j: (0, j)),
                   pl.BlockSpec((4, nb, cb), lambda j: (0, 0, j))],
        out_shape=[jax.ShapeDtypeStruct((seq, HY_WIDTH), BF16),
                   jax.ShapeDtypeStruct((seq, HY_WIDTH), BF16),
                   jax.ShapeDtypeStruct((4, nb, HY_WIDTH), F32)],
        scratch_shapes=[pltpu.VMEM((seq, HY_HIDDEN), F32)],
        compiler_params=_cparams(("arbitrary",)),
        name="hyena_filter",
    )(feats, w1, row(p['hy_filt_b1']), row(p['hy_filt_freq1']),
      p['hy_filt_w2'], row(p['hy_filt_b2']), row(p['hy_filt_freq2']),
      p['hy_filt_w3'], p['hy_filt_w3'], row(p['hy_filt_b3']), row(p['hy_filt_b3']), deltas)


def _kspec_kernel(c_ref, s_ref, hf_ref, hb_ref, ext_ref, kc_ref, ks_ref, kn_ref, *, nb):
    tb = c_ref.shape[0]
    f = lax.broadcasted_iota(I32, (tb, 1), 0)
    wgt = jnp.where(f == 0, 1.0, 2.0) / float(2 * tb)
    sig = (1 - 2 * (f & 1)).astype(F32)
    cm = c_ref[...]
    sm = s_ref[...]
    blk = lambda ref, j: ref[j * tb:(j + 1) * tb, :]
    chf = [jnp.dot(cm, blk(hf_ref, j), preferred_element_type=F32) for j in range(nb)]
    shf = [jnp.dot(sm, blk(hf_ref, j), preferred_element_type=F32) for j in range(nb)]
    chb = [jnp.dot(cm, blk(hb_ref, j), preferred_element_type=F32) for j in range(nb)]
    shb = [jnp.dot(sm, blk(hb_ref, j), preferred_element_type=F32) for j in range(nb)]
    first_f = lambda j: ext_ref[0, j:j + 1, :]
    first_b = lambda j: ext_ref[1, j:j + 1, :]
    alt_f = lambda j: ext_ref[2, j:j + 1, :]
    alt_b = lambda j: ext_ref[3, j:j + 1, :]
    for d in range(-(nb - 1), nb):
        k = d + nb - 1
        if d == 0:
            kc, ks, kn = chf[0] + chb[0], shf[0] - shb[0], alt_f(0) + alt_b(0)
        elif d > 0:
            kc = chf[d] + sig * (chf[d - 1] - first_f(d - 1))
            ks = shf[d] + sig * shf[d - 1]
            kn = alt_f(d) + alt_f(d - 1) - first_f(d - 1)
        else:
            e = -d
            kc = chb[e] + sig * (chb[e - 1] - first_b(e - 1))
            ks = -(shb[e] + sig * shb[e - 1])
            kn = alt_b(e) + alt_b(e - 1) - first_b(e - 1)
        kc_ref[k] = kc * wgt
        ks_ref[k] = ks * wgt
        kn_ref[k:k + 1, :] = kn * (1.0 / float(2 * tb))


def _kspec(cmat, smat, hf, hb, ext, nb):
    tb = cmat.shape[0]
    seq = hf.shape[0]
    nd = 2 * nb - 1
    cb = 256 if nb > 1 else 512
    return pl.pallas_call(
        functools.partial(_kspec_kernel, nb=nb),
        grid=(HY_WIDTH // cb,),
        in_specs=[pl.BlockSpec((tb, tb), lambda j: (0, 0)),
                  pl.BlockSpec((tb, tb), lambda j: (0, 0)),
                  pl.BlockSpec((seq, cb), lambda j: (0, j)),
                  pl.BlockSpec((seq, cb), lambda j: (0, j)),
                  pl.BlockSpec((4, nb, cb), lambda j: (0, 0, j))],
        out_specs=[pl.BlockSpec((nd, tb, cb), lambda j: (0, 0, j)),
                   pl.BlockSpec((nd, tb, cb), lambda j: (0, 0, j)),
                   pl.BlockSpec((nd, cb), lambda j: (0, j))],
        out_shape=[jax.ShapeDtypeStruct((nd, tb, HY_WIDTH), F32),
                   jax.ShapeDtypeStruct((nd, tb, HY_WIDTH), F32),
                   jax.ShapeDtypeStruct((nd, HY_WIDTH), F32)],
        compiler_params=_cparams(("arbitrary",)),
        name="hyena_kspec",
    )(cmat, smat, hf, hb, ext)


def _fwd_kernel(c_ref, s_ref, z_ref, kc_ref, ks_ref, pc_ref, ps_ref, *, nb):
    tb = c_ref.shape[0]
    cm = c_ref[...]
    sm = s_ref[...]
    zc = [jnp.dot(cm, z_ref[0, j * tb:(j + 1) * tb, :], preferred_element_type=F32) for j in range(nb)]
    zs = [jnp.dot(sm, z_ref[0, j * tb:(j + 1) * tb, :], preferred_element_type=F32) for j in range(nb)]
    for i in range(nb):
        pc = None
        ps = None
        for j in range(nb):
            kc = kc_ref[i - j + nb - 1]
            ks = ks_ref[i - j + nb - 1]
            tc = zc[j] * kc - zs[j] * ks
            ts = zc[j] * ks + zs[j] * kc
            pc = tc if pc is None else pc + tc
            ps = ts if ps is None else ps + ts
        pc_ref[0, i * tb:(i + 1) * tb, :] = pc.astype(BF16)
        ps_ref[0, i * tb:(i + 1) * tb, :] = ps.astype(BF16)


def _hy_fwd(cmat, smat, z, kc, ks, nb):
    b, seq, _ = z.shape
    tb = seq // nb
    nd = 2 * nb - 1
    cb = 256 if nb > 1 else HY_WIDTH
    return pl.pallas_call(
        functools.partial(_fwd_kernel, nb=nb),
        grid=(HY_WIDTH // cb, b),
        in_specs=[pl.BlockSpec((tb, tb), lambda j, bi: (0, 0)),
                  pl.BlockSpec((tb, tb), lambda j, bi: (0, 0)),
                  pl.BlockSpec((1, seq, cb), lambda j, bi: (bi, 0, j)),
                  pl.BlockSpec((nd, tb, cb), lambda j, bi: (0, 0, j)),
                  pl.BlockSpec((nd, tb, cb), lambda j, bi: (0, 0, j))],
        out_specs=[pl.BlockSpec((1, seq, cb), lambda j, bi: (bi, 0, j)),
                   pl.BlockSpec((1, seq, cb), lambda j, bi: (bi, 0, j))],
        out_shape=[jax.ShapeDtypeStruct((b, seq, HY_WIDTH), BF16),
                   jax.ShapeDtypeStruct((b, seq, HY_WIDTH), BF16)],
        compiler_params=_cparams(("arbitrary", "arbitrary")),
        name="hyena_fwd",
    )(cmat, smat, z, kc, ks)


def _inv_kernel(c_ref, s_ref, pc_ref, ps_ref, z_ref, x0_ref, zn_ref, kn_ref, skip_ref, o_ref, *, nb):
    tb = c_ref.shape[0]
    cm = c_ref[...]
    sm = s_ref[...]
    t = lax.broadcasted_iota(I32, (tb, 1), 0)
    sign = (1 - 2 * (t & 1)).astype(F32)
    for i in range(nb):
        rs = slice(i * tb, (i + 1) * tb)
        y = jnp.dot(cm, pc_ref[0, rs, :], preferred_element_type=F32)
        y = y + jnp.dot(sm, ps_ref[0, rs, :], preferred_element_type=F32)
        nyq = None
        for j in range(nb):
            k = i - j + nb - 1
            term = zn_ref[0, j:j + 1, :] * kn_ref[k:k + 1, :]
            nyq = term if nyq is None else nyq + term
        y = y + sign * nyq + z_ref[0, rs, :].astype(F32) * skip_ref[...]
        o_ref[0, rs, :] = (x0_ref[0, rs, :].astype(F32) * y).astype(BF16)


def _hy_inv(cmat, smat, pc, ps, z, x0, zn, kn, skip, nb):
    b, seq, _ = z.shape
    tb = seq // nb
    nd = 2 * nb - 1
    cb = 256 if nb > 1 else HY_WIDTH
    blk = pl.BlockSpec((1, seq, cb), lambda j, bi: (bi, 0, j))
    return pl.pallas_call(
        functools.partial(_inv_kernel, nb=nb),
        grid=(HY_WIDTH // cb, b),
        in_specs=[pl.BlockSpec((tb, tb), lambda j, bi: (0, 0)),
                  pl.BlockSpec((tb, tb), lambda j, bi: (0, 0)),
                  blk, blk, blk, blk,
                  pl.BlockSpec((1, nb, cb), lambda j, bi: (bi, 0, j)),
                  pl.BlockSpec((nd, cb), lambda j, bi: (0, j)),
                  pl.BlockSpec((1, cb), lambda j, bi: (0, j))],
        out_specs=blk,
        out_shape=jax.ShapeDtypeStruct((b, seq, HY_WIDTH), BF16),
        compiler_params=_cparams(("arbitrary", "arbitrary")),
        name="hyena_inv",
    )(cmat, smat, pc, ps, z, x0, zn, kn, skip)


def _layer_norm(v, g, b):
    mu = jnp.mean(v, axis=-1, keepdims=True)
    vc = v - mu
    var = jnp.mean(vc * vc, axis=-1, keepdims=True)
    return vc * lax.rsqrt(var + EPS) * g + b


def _merge_kernel(at_ref, hy_ref, ga_ref, gb_ref, woa_ref, woh_ref, o_ref):
    a = jnp.dot(at_ref[...], woa_ref[...], preferred_element_type=F32)
    hh = jnp.dot(hy_ref[...], woh_ref[...], preferred_element_type=F32)
    merged = _sigmoid(ga_ref[...].astype(F32)) * a + _sigmoid(gb_ref[...].astype(F32)) * hh
    o_ref[...] = merged.astype(BF16)


def _merge(attn, hy, big, woa, woh):
    n = attn.shape[0]
    d = woa.shape[1]
    tm = 512
    return pl.pallas_call(
        _merge_kernel,
        grid=(n // tm,),
        in_specs=[pl.BlockSpec((tm, attn.shape[1]), lambda i: (i, 0)),
                  pl.BlockSpec((tm, hy.shape[1]), lambda i: (i, 0)),
                  pl.BlockSpec((tm, d), lambda i: (i, 0)),
                  pl.BlockSpec((tm, d), lambda i: (i, 1)),
                  _const_spec(woa.shape), _const_spec(woh.shape)],
        out_specs=pl.BlockSpec((tm, d), lambda i: (i, 0)),
        out_shape=jax.ShapeDtypeStruct((n, d), BF16),
        compiler_params=_cparams(("arbitrary",)),
        name="merge",
    )(attn, hy, big, big, woa, woh)


def _ln1_kernel(x_ref, mg_ref, mod_ref, wout_ref, g_ref, b_ref, wr_ref, x1_ref, h2_ref, lg_ref, *, n_sub):
    m = mod_ref[0]
    rows = x_ref.shape[0] // n_sub
    for t in range(n_sub):
        rs = slice(t * rows, (t + 1) * rows)
        mix = jnp.dot(mg_ref[rs, :], wout_ref[...], preferred_element_type=F32)
        x1 = _layer_norm(ALPHA * x_ref[rs, :] + m[2:3] * mix, g_ref[...], b_ref[...])
        x1_ref[rs, :] = x1
        h2 = x1 * (1.0 + m[4:5]) + m[3:4]
        h2_ref[rs, :] = h2
        hi = h2.astype(BF16)
        lo = (h2 - hi.astype(F32)).astype(BF16)
        lg = (jnp.dot(hi, wr_ref[...], preferred_element_type=F32)
              + jnp.dot(lo, wr_ref[...], preferred_element_type=F32))
        lg_ref[rs, :] = lg + pltpu.roll(lg, LANES - N_EXPERTS, axis=1)


def _ln1(x, merged, mods, mod_row, wout, g, b, wr_pack):
    n, d = x.shape
    tm = 512
    return pl.pallas_call(
        functools.partial(_ln1_kernel, n_sub=2),
        grid=(n // tm,),
        in_specs=[pl.BlockSpec((tm, d), lambda i: (i, 0)),
                  pl.BlockSpec((tm, d), lambda i: (i, 0)),
                  pl.BlockSpec((1, N_MOD, d), lambda i: (mod_row(i * tm), 0, 0)),
                  _const_spec(wout.shape), _const_spec((1, d)), _const_spec((1, d)),
                  _const_spec(wr_pack.shape)],
        out_specs=[pl.BlockSpec((tm, d), lambda i: (i, 0)),
                   pl.BlockSpec((tm, d), lambda i: (i, 0)),
                   pl.BlockSpec((tm, LANES), lambda i: (i, 0))],
        out_shape=[jax.ShapeDtypeStruct((n, d), F32),
                   jax.ShapeDtypeStruct((n, d), F32),
                   jax.ShapeDtypeStruct((n, LANES), F32)],
        compiler_params=_cparams(("arbitrary",)),
        name="ln1_router",
    )(x, merged, mods, wout, g, b, wr_pack)


def _split3(t):
    t1 = t.astype(BF16)
    r = t - t1.astype(F32)
    t2 = r.astype(BF16)
    t3 = (r - t2.astype(F32)).astype(BF16)
    return t1, t2, t3


def _dot_exact(sel01, table):
    t1, t2, t3 = _split3(table)
    out = jnp.dot(sel01, t1, preferred_element_type=F32)
    out = out + jnp.dot(sel01, t2, preferred_element_type=F32)
    return out + jnp.dot(sel01, t3, preferred_element_type=F32)


def _route_kernel(lg_ref, idx_ref, q_ref, gate_ref, start_ref, aff_scr, sel_scr, pos_scr, *, cap):
    ne, nc, _ = lg_ref.shape
    lg = lg_ref[...]
    mx = jnp.max(lg, axis=0, keepdims=True)
    ex = jnp.exp(lg - mx)
    aff = ex / jnp.sum(ex, axis=0, keepdims=True)
    aff_scr[...] = aff

    def tbody(k, thr):
        cand = thr | lax.shift_left(jnp.int32(1), 30 - k)
        ge = jnp.where(aff >= lax.bitcast_convert_type(cand, F32), 1.0, 0.0)
        cnt = jnp.sum(jnp.sum(ge, axis=1, keepdims=True), axis=2, keepdims=True)
        return jnp.where(cnt >= float(cap), cand, thr)

    thr = lax.fori_loop(0, 31, tbody, jnp.zeros((ne, 1, 1), I32))
    lo = lax.bitcast_convert_type(thr, F32)
    above = aff >= lax.bitcast_convert_type(thr + 1, F32)
    sel_scr[...] = jnp.where(above, 1.0, 0.0)
    pos_scr[...] = jnp.where((aff >= lo) & jnp.logical_not(above), aff, -1.0)

    r_i = lax.broadcasted_iota(I32, (LANES, LANES), 0)
    c_i = lax.broadcasted_iota(I32, (LANES, LANES), 1)
    upper_incl = jnp.where(r_i <= c_i, 1.0, 0.0).astype(BF16)
    ones_sq = jnp.ones((LANES, LANES), BF16)
    rr_i = lax.broadcasted_iota(I32, (nc, nc), 0)
    cc_i = lax.broadcasted_iota(I32, (nc, nc), 1)
    lower_strict = jnp.where(cc_i < rr_i, 1.0, 0.0).astype(BF16)
    upper_rows = jnp.where(rr_i <= cc_i, 1.0, 0.0).astype(BF16)

    def prefix(x01):
        xb = x01.astype(BF16)
        p1 = jnp.dot(xb, upper_incl, preferred_element_type=F32)
        totb = jnp.dot(xb, ones_sq, preferred_element_type=F32)
        offs = jnp.dot(lower_strict, totb.astype(BF16), preferred_element_type=F32)
        return p1, offs

    tok = (lax.broadcasted_iota(I32, (nc, LANES), 0) * LANES
           + lax.broadcasted_iota(I32, (nc, LANES), 1)).astype(F32)

    def reduce2(fn, v):
        return fn(fn(v, axis=0, keepdims=True), axis=1, keepdims=True)

    def pass1(e, start):
        above_e = sel_scr[e]
        need = cap - jnp.sum(above_e).astype(I32)

        def pick(_, carry):
            vals, picked = carry
            first = reduce2(jnp.min, jnp.where(vals == reduce2(jnp.max, vals), tok, float(nc * LANES)))
            hit = tok == first
            return jnp.where(hit, -1.0, vals), picked + jnp.where(hit, 1.0, 0.0)

        _, picked = lax.fori_loop(0, need, pick, (pos_scr[e], jnp.zeros((nc, LANES), F32)))
        sel = above_e + picked
        sel_scr[e] = sel
        s1, soffs = prefix(sel)
        return start + (s1 + soffs - sel)

    start = lax.fori_loop(0, ne, pass1, jnp.zeros((nc, LANES), F32))
    start_ref[...] = start.astype(I32)

    p_col = lax.broadcasted_iota(I32, (cap, 1), 0).astype(F32)
    lane_row = lax.broadcasted_iota(I32, (1, LANES), 1).astype(F32)
    chunk_row = lax.broadcasted_iota(I32, (1, nc), 1).astype(F32)
    ones8 = jnp.ones((8, LANES), BF16)

    def row_sums(m):
        ones = jnp.ones((8, m.shape[1]), BF16)
        nt = lambda part: lax.dot_general(ones, part, (((1,), (1,)), ((), ())), preferred_element_type=F32)
        m1, m2, m3 = _split3(m)
        return (nt(m1) + nt(m2) + nt(m3))[0:1]

    def pass2(e, before):
        sel = sel_scr[e]
        selb = sel.astype(BF16)
        p1, offs = prefix(sel)
        tot_row = lax.dot_general(ones8, selb, (((1,), (1,)), ((), ())), preferred_element_type=F32)
        cum_row = jnp.dot(tot_row.astype(BF16), upper_rows, preferred_element_type=F32)
        cum1 = cum_row[0:1]
        prev1 = cum1 - tot_row[0:1]
        in_chunk = (prev1 <= p_col) & (p_col < cum1)
        ohc = jnp.where(in_chunk, 1.0, 0.0).astype(BF16)
        keyg = jnp.dot(ohc, (p1 * sel).astype(BF16), preferred_element_type=F32)
        offg = _dot_exact(ohc, offs)
        ohl = keyg == (p_col - offg + 1.0)
        l_p = row_sums(jnp.where(ohl, lane_row, 0.0))
        c_p = row_sums(jnp.where(in_chunk, chunk_row, 0.0))
        idx_ref[e] = (c_p * float(LANES) + l_p).astype(I32)
        qtab = _dot_exact(ohc, start) + jnp.dot(ohc, before.astype(BF16), preferred_element_type=F32)
        q_ref[e] = row_sums(jnp.where(ohl, qtab, 0.0)).astype(I32)
        afg = _dot_exact(ohc, aff_scr[e])
        gate_ref[e] = row_sums(jnp.where(ohl, afg, 0.0))
        return before + sel

    lax.fori_loop(0, ne, pass2, jnp.zeros((nc, LANES), F32))


def _route(lg3, cap):
    ne, nc, _ = lg3.shape
    full = lambda s: pl.BlockSpec(s, lambda: (0,) * len(s))
    return pl.pallas_call(
        functools.partial(_route_kernel, cap=cap),
        in_specs=[full(lg3.shape)],
        out_specs=[full((ne, 1, cap)), full((ne, 1, cap)), full((ne, 1, cap)), full((nc, LANES))],
        out_shape=[jax.ShapeDtypeStruct((ne, 1, cap), I32),
                   jax.ShapeDtypeStruct((ne, 1, cap), I32),
                   jax.ShapeDtypeStruct((ne, 1, cap), F32),
                   jax.ShapeDtypeStruct((nc, LANES), I32)],
        scratch_shapes=[pltpu.VMEM((ne, nc, LANES), F32),
                        pltpu.VMEM((ne, nc, LANES), F32),
                        pltpu.VMEM((ne, nc, LANES), F32)],
        compiler_params=pltpu.CompilerParams(vmem_limit_bytes=VMEM_LIMIT),
        name="route",
    )(lg3)


def _ffn_kernel(idx_ref, q_ref, gate_ref, wg_ref, wu_ref, wd_ref, h_hbm, y_hbm,
                xb, xa, yb, ya, gsem, ssem, *, tm, npair, total):
    step = pl.program_id(0) * npair + pl.program_id(1)
    t0 = 2 * step
    t1 = t0 + 1

    def gather_row(base, r, buf, s):
        return pltpu.make_async_copy(h_hbm.at[pl.ds(idx_ref[base + r], 1), :], buf.at[pl.ds(r, 1), :], gsem.at[s])

    def scatter_row(base, r, buf, s):
        return pltpu.make_async_copy(buf.at[pl.ds(r, 1), :], y_hbm.at[pl.ds(q_ref[base + r], 1), :], ssem.at[s])

    def wait_gather(buf, s):
        pltpu.make_async_copy(h_hbm.at[pl.ds(0, tm), :], buf, gsem.at[s]).wait()

    def wait_scatter(buf, s):
        pltpu.make_async_copy(buf, y_hbm.at[pl.ds(0, tm), :], ssem.at[s]).wait()

    n_chunk = 4
    fc = EXPERT_FF // n_chunk
    per = tm // n_chunk

    def compute(x_buf, y_buf, r0, gather_base, gather_buf, scatter_base, scatter_buf, s):
        x = x_buf[...].astype(BF16)
        gate = gate_ref[r0:r0 + tm, :]
        for c in range(n_chunk):
            for r in range(c * per, (c + 1) * per):
                gather_row(gather_base, r, gather_buf, s).start()
            for r in range(c * per, (c + 1) * per):
                scatter_row(scatter_base, r, scatter_buf, s).start()
            g = jnp.dot(x, wg_ref[0, :, c * fc:(c + 1) * fc], preferred_element_type=F32)
            u = jnp.dot(x, wu_ref[0, :, c * fc:(c + 1) * fc], preferred_element_type=F32)
            hid = (g * _sigmoid(g) * u * gate).astype(BF16)
            part = jnp.dot(hid, wd_ref[0, c * fc:(c + 1) * fc, :], preferred_element_type=F32)
            if c == 0:
                y_buf[...] = part
            else:
                y_buf[...] += part

    @pl.when(step == 0)
    def _():
        yb[...] = jnp.zeros(yb.shape, F32)

        def body(r, carry):
            gather_row(0, r, xa, 0).start()
            return carry

        lax.fori_loop(0, tm, body, 0)

    wait_gather(xa, 0)

    @pl.when(step >= 1)
    def _():
        wait_scatter(ya, 0)

    base1 = t1 * tm
    pbase = jnp.maximum(t0 - 1, 0) * tm
    compute(xa, ya, 0, base1, xb, pbase, yb, 1)

    wait_gather(xb, 1)
    wait_scatter(yb, 1)
    nbase = jnp.minimum(t1 + 1, total - 1) * tm
    base0 = t0 * tm
    compute(xb, yb, tm, nbase, xa, base0, ya, 0)

    @pl.when(t1 == total - 1)
    def _():
        wait_gather(xa, 0)
        wait_scatter(ya, 0)

        def body(r, carry):
            scatter_row(base1, r, yb, 1).start()
            return carry

        lax.fori_loop(0, tm, body, 0)
        wait_scatter(yb, 1)


def _ffn(h2, idx, qpos, gate, wg, wu, wd):
    n, d = h2.shape
    ne = wg.shape[0]
    cap = idx.shape[0] // ne
    tm = min(512, cap // 2)
    npair = cap // (2 * tm)
    total = 2 * ne * npair
    grid_spec = pltpu.PrefetchScalarGridSpec(
        num_scalar_prefetch=2,
        grid=(ne, npair),
        in_specs=[pl.BlockSpec((2 * tm, 1), lambda e, i, *_: (e * npair + i, 0)),
                  pl.BlockSpec((1, d, EXPERT_FF), lambda e, i, *_: (e, 0, 0)),
                  pl.BlockSpec((1, d, EXPERT_FF), lambda e, i, *_: (e, 0, 0)),
                  pl.BlockSpec((1, EXPERT_FF, d), lambda e, i, *_: (e, 0, 0)),
                  pl.BlockSpec(memory_space=pl.ANY)],
        out_specs=pl.BlockSpec(memory_space=pl.ANY),
        scratch_shapes=[pltpu.VMEM((tm, d), F32), pltpu.VMEM((tm, d), F32),
                        pltpu.VMEM((tm, d), F32), pltpu.VMEM((tm, d), F32),
                        pltpu.SemaphoreType.DMA((2,)),
                        pltpu.SemaphoreType.DMA((2,))],
    )
    return pl.pallas_call(
        functools.partial(_ffn_kernel, tm=tm, npair=npair, total=total),
        grid_spec=grid_spec,
        out_shape=jax.ShapeDtypeStruct((ne * cap, d), F32),
        compiler_params=_cparams(("arbitrary", "arbitrary")),
        name="expert_ffn",
    )(idx, qpos, gate, wg, wu, wd, h2)


def _combine_kernel(cs_ref, x1_ref, mod_ref, st_ref, en_ref, g_ref, b_ref, y_hbm, o_ref,
                    ybuf, acc, sem, *, win, total, nsteps):
    i = pl.program_id(0)
    slot = i % 2
    lo0 = cs_ref[i]
    hi = cs_ref[i + 1]
    adv = win - 8

    def win_start(lo):
        return pl.multiple_of((jnp.minimum(lo, total - win) // 8) * 8, 8)

    def window(lo, sl):
        return pltpu.make_async_copy(y_hbm.at[pl.ds(win_start(lo), win), :], ybuf.at[sl], sem.at[sl])

    @pl.when(i == 0)
    def _():
        window(lo0, 0).start()

    @pl.when(i + 1 < nsteps)
    def _():
        window(hi, 1 - slot).start()

    ntrip = jnp.maximum((hi - lo0 + adv - 1) // adv, 1)
    acc[...] = jnp.zeros_like(acc)
    st = st_ref[...]
    en = en_ref[...]

    def body(k, carry):
        lo = lo0 + k * adv

        @pl.when(k > 0)
        def _():
            window(lo, slot).start()

        window(lo, slot).wait()
        qj = win_start(lo) + lax.broadcasted_iota(I32, (1, win), 1)
        first = jnp.maximum(st, lo)
        last = jnp.minimum(en, lo + adv)
        seg = (qj >= first) & (qj < last)
        acc[...] += jnp.dot(jnp.where(seg, 1.0, 0.0).astype(BF16), ybuf[slot].astype(BF16),
                            preferred_element_type=F32)
        return carry

    lax.fori_loop(0, ntrip, body, 0)
    m = mod_ref[0]
    o_ref[...] = _layer_norm(ALPHA * x1_ref[...] + m[5:6] * acc[...], g_ref[...], b_ref[...])


def _combine(x1, mods, mod_row, start, pairs, g, b):
    n, d = x1.shape
    total = pairs.shape[0]
    tc = 256
    win = 640
    st = start.reshape(n, 1)
    en = jnp.concatenate([start[1:], jnp.full((1,), total, I32)]).reshape(n, 1)
    cs = jnp.concatenate([start[::tc], jnp.full((1,), total, I32)])
    grid_spec = pltpu.PrefetchScalarGridSpec(
        num_scalar_prefetch=1,
        grid=(n // tc,),
        in_specs=[pl.BlockSpec((tc, d), lambda i, *_: (i, 0)),
                  pl.BlockSpec((1, N_MOD, d), lambda i, *_: (mod_row(i * tc), 0, 0)),
                  pl.BlockSpec((tc, 1), lambda i, *_: (i, 0)),
                  pl.BlockSpec((tc, 1), lambda i, *_: (i, 0)),
                  pl.BlockSpec((1, d), lambda i, *_: (0, 0)),
                  pl.BlockSpec((1, d), lambda i, *_: (0, 0)),
                  pl.BlockSpec(memory_space=pl.ANY)],
        out_specs=pl.BlockSpec((tc, d), lambda i, *_: (i, 0)),
        scratch_shapes=[pltpu.VMEM((2, win, d), F32),
                        pltpu.VMEM((tc, d), F32),
                        pltpu.SemaphoreType.DMA((2,))],
    )
    return pl.pallas_call(
        functools.partial(_combine_kernel, win=win, total=total, nsteps=n // tc),
        grid_spec=grid_spec,
        out_shape=jax.ShapeDtypeStruct((n, d), F32),
        compiler_params=_cparams(("arbitrary",)),
        name="combine_ln2",
    )(cs, x1, mods, st, en, g, b, pairs)


def _swap_pairs(w):
    f = ROPE_FREQS
    return jnp.concatenate([w[..., f:2 * f], w[..., 0:f], w[..., 3 * f:4 * f], w[..., 2 * f:3 * f]], axis=-1)


def _rope_tables(seq):
    n_rows = seq // GRID_W
    row = np.repeat(np.arange(n_rows, dtype=np.float64), GRID_W)
    col = np.tile(np.arange(GRID_W, dtype=np.float64), n_rows)
    inv = ROPE_THETA ** (-np.arange(ROPE_FREQS, dtype=np.float64) / ROPE_FREQS)
    ar = row[:, None] * inv
    ac = col[:, None] * inv
    z = np.zeros((seq, LANES - QK_ROPE))
    cos128 = np.concatenate([np.cos(ar), np.cos(ar), np.cos(ac), np.cos(ac), z], axis=1)
    sin128 = np.concatenate([-np.sin(ar), np.sin(ar), -np.sin(ac), np.sin(ac), z], axis=1)
    return jnp.asarray(cos128, F32), jnp.asarray(sin128, F32)


def _dft_tables(seq):
    m = 2 * seq
    blk = 64
    s = np.arange(seq, dtype=np.int64)
    ang1 = ((np.arange(0, seq, blk, dtype=np.int64)[:, None] * s[None, :]) % m) * (2.0 * math.pi / m)
    ang0 = ((np.arange(blk, dtype=np.int64)[:, None] * s[None, :]) % m) * (2.0 * math.pi / m)
    c1, s1 = jnp.asarray(np.cos(ang1), F32)[:, None, :], jnp.asarray(np.sin(ang1), F32)[:, None, :]
    c0, s0 = jnp.asarray(np.cos(ang0), F32)[None, :, :], jnp.asarray(np.sin(ang0), F32)[None, :, :]
    cmat = (c1 * c0 - s1 * s0).reshape(seq, seq).astype(BF16)
    smat = (s1 * c0 + c1 * s0).reshape(seq, seq).astype(BF16)
    return cmat, smat


def _relayout_kernel(w_ref, o_ref, *, s1, s2, s3):
    chunk = 512
    width = w_ref.shape[1]
    f = ROPE_FREQS
    o_ref[:, 0:s1] = w_ref[:, 0:s1].astype(BF16)
    x = w_ref[:, s1:s1 + LANES]
    lane = lax.broadcasted_iota(I32, x.shape, 1)
    keep = lane < (s2 - s1)
    partner = jnp.where((lane & f) == 0, pltpu.roll(x, LANES - f, axis=1), pltpu.roll(x, f, axis=1))
    o_ref[:, s1:s1 + LANES] = jnp.where(keep, x, 0.0).astype(BF16)
    o_ref[:, s1 + LANES:SMALL_W] = jnp.where(keep, partner, 0.0).astype(BF16)
    dst = SMALL_W
    for lo, hi in ((s3, width), (s2, s3)):
        for c in range(lo, hi, chunk):
            n = min(chunk, hi - c)
            o_ref[:, dst:dst + n] = w_ref[:, c:c + n].astype(BF16)
            dst += n


def _relayout_w_in(w_in, s1, s2, s3):
    d, width = w_in.shape
    tr = 256
    return pl.pallas_call(
        functools.partial(_relayout_kernel, s1=s1, s2=s2, s3=s3),
        grid=(d // tr,),
        in_specs=[pl.BlockSpec((tr, width), lambda i: (i, 0))],
        out_specs=pl.BlockSpec((tr, SMALL_W + BIG_W), lambda i: (i, 0)),
        out_shape=jax.ShapeDtypeStruct((d, SMALL_W + BIG_W), BF16),
        compiler_params=_cparams(("arbitrary",)),
        name="w_in_relayout",
    )(w_in)


def _prep_weights(p):
    w = {}
    w_in = p['w_in']
    s0, s1, s2, s3, s4 = (Q_LORA, Q_LORA + KV_LORA, Q_LORA + KV_LORA + QK_ROPE,
                          Q_LORA + KV_LORA + QK_ROPE + 3 * HY_WIDTH,
                          Q_LORA + KV_LORA + QK_ROPE + 3 * HY_WIDTH + D_MODEL)
    w['w_in'] = _relayout_w_in(w_in, s1, s2, s3)
    scale = (QK_NOPE + QK_ROPE) ** -0.5 * math.log2(math.e)
    wq = (p['w_uq'] * scale).reshape(Q_LORA, MLA_HEADS, QK_NOPE + QK_ROPE)
    zq = jnp.zeros((Q_LORA, MLA_HEADS, HEAD_PAD - QK_NOPE - QK_ROPE), F32)
    w['wa'] = jnp.concatenate([wq, zq], axis=-1).reshape(Q_LORA, MLA_HEADS * HEAD_PAD).astype(BF16)
    w['wb'] = jnp.concatenate([_swap_pairs(wq[..., QK_NOPE:]), zq], axis=-1).reshape(
        Q_LORA, MLA_HEADS * LANES).astype(BF16)
    wkv = p['w_ukv'].reshape(KV_LORA, MLA_HEADS, QK_NOPE + V_HEAD)
    w['wk'] = wkv[..., :QK_NOPE].reshape(KV_LORA, MLA_HEADS * QK_NOPE).astype(BF16)
    w['wv'] = wkv[..., QK_NOPE:].reshape(KV_LORA, MLA_HEADS * V_HEAD).astype(BF16)
    w['woa'] = p['w_o_mla'].astype(BF16)
    w['woh'] = p['w_o_hy'].astype(BF16)
    w['wout'] = p['w_out'].astype(BF16)
    wr_hi = p['w_router'].astype(BF16)
    wr_lo = (p['w_router'] - wr_hi.astype(F32)).astype(BF16)
    w['wr_pack'] = jnp.pad(jnp.concatenate([wr_hi, wr_lo], axis=1), ((0, 0), (0, LANES - 2 * N_EXPERTS)))
    w['hy_w'] = p['hy_short_w'].reshape(3, 3, HY_WIDTH).transpose(1, 0, 2)
    w['hy_b'] = p['hy_short_b'].reshape(3, 1, HY_WIDTH)
    return w


def _trunk(x3, mods, mod_row, tm_in, p, w, cache_kv, cache_kr, rope, ffn_w):
    b, seq, d = x3.shape
    n = b * seq
    x = x3.reshape(n, d)
    if ffn_w is None:
        shapes = [p[k].shape for k in ('w_gate', 'w_up', 'w_down')]
        flat = [p[k].reshape(-1, p[k].shape[-1]) for k in ('w_gate', 'w_up', 'w_down')]
        small, big, cast = _inproj(x, mods, mod_row, w['w_in'], tm_in, flat)
        ffn_w = tuple(a.reshape(s) for a, s in zip(cast, shapes))
    else:
        small, big, _ = _inproj(x, mods, mod_row, w['w_in'], tm_in)
    cos128, sin128 = _rope_tables(seq) if rope else (None, None)
    q, ckv, kr = _qprep(small, p['q_norm_g'].reshape(1, -1), p['kv_norm_g'].reshape(1, -1),
                        w['wa'], w['wb'], cos128, sin128, seq, rope)
    ckv3 = ckv.reshape(b, seq, KV_LORA)
    kr3 = kr.reshape(b, seq, LANES)
    if cache_kv is None:
        kv_all, kr_all = ckv3, kr3
    else:
        kv_all = jnp.concatenate([cache_kv, ckv3], axis=1)
        kr_all = jnp.concatenate([jnp.pad(cache_kr, ((0, 0), (0, 0), (0, LANES - QK_ROPE))), kr3], axis=1)
    kh, vh = _kvup(kv_all, kr_all, w['wk'], w['wv'])
    attn = _attention(q.reshape(b, seq, MLA_HEADS * HEAD_PAD), kh, vh).reshape(n, MLA_HEADS * V_HEAD)

    nb = HY_TIME_BLOCKS if seq >= 512 else 1
    x0c, z, zn = _hypre(big.reshape(b, seq, BIG_W), w['hy_w'], w['hy_b'], nb)
    hf_td, hb_td, ext = _filter_td(seq, p, nb)
    cmat, smat = _dft_tables(seq // nb)
    kc, ks, kn = _kspec(cmat, smat, hf_td, hb_td, ext, nb)
    pc, ps = _hy_fwd(cmat, smat, z, kc, ks, nb)
    hy = _hy_inv(cmat, smat, pc, ps, z, x0c, zn, kn, p['hy_skip'].reshape(1, -1), nb).reshape(n, HY_WIDTH)

    merged = _merge(attn, hy, big, w['woa'], w['woh'])
    x1, h2, lg = _ln1(x, merged, mods, mod_row, w['wout'], p['ln1_g'].reshape(1, -1),
                      p['ln1_b'].reshape(1, -1), w['wr_pack'])
    cap = EC_CAPACITY * n // N_EXPERTS
    lg3 = lg[:, :N_EXPERTS].T.reshape(N_EXPERTS, n // LANES, LANES)
    idx, qpos, gate, start = _route(lg3, cap)
    pairs = _ffn(h2, idx.reshape(-1), qpos.reshape(-1), gate.reshape(-1, 1), *ffn_w)
    y = _combine(x1, mods, mod_row, start.reshape(-1), pairs,
                 p['ln2_g'].reshape(1, -1), p['ln2_b'].reshape(1, -1))
    return y.reshape(b, seq, d), ckv3, kr3[..., :QK_ROPE], ffn_w


def kernel(x_prompt, x_sample, cache_kv_c, cache_k_rope, c, c_ctx, w_ada, b_ada, w_in, q_norm_g, kv_norm_g, w_uq, w_ukv, w_o_mla, hy_short_w, hy_short_b, hy_filt_w1, hy_filt_b1, hy_filt_freq1, hy_filt_w2, hy_filt_b2, hy_filt_freq2, hy_filt_w3, hy_filt_b3, hy_skip, w_o_hy, w_out, ln1_g, ln1_b, ln2_g, ln2_b, w_router, w_gate, w_up, w_down):
    params = dict(w_in=w_in, q_norm_g=q_norm_g, kv_norm_g=kv_norm_g, w_uq=w_uq, w_ukv=w_ukv, w_o_mla=w_o_mla,
                  hy_short_w=hy_short_w, hy_short_b=hy_short_b, hy_filt_w1=hy_filt_w1, hy_filt_b1=hy_filt_b1,
                  hy_filt_freq1=hy_filt_freq1, hy_filt_w2=hy_filt_w2, hy_filt_b2=hy_filt_b2,
                  hy_filt_freq2=hy_filt_freq2, hy_filt_w3=hy_filt_w3, hy_filt_b3=hy_filt_b3, hy_skip=hy_skip,
                  w_o_hy=w_o_hy, w_out=w_out, ln1_g=ln1_g, ln1_b=ln1_b, ln2_g=ln2_g, ln2_b=ln2_b,
                  w_router=w_router, w_gate=w_gate, w_up=w_up, w_down=w_down)
    depth = w_in.shape[0]
    dec_b, dec_seq, d = x_sample.shape
    n_rows = 16
    cond = jnp.zeros((n_rows, d), F32).at[:dec_b].set(c).at[dec_b].set(c_ctx)
    y_prompt, y_sample = x_prompt, x_sample
    kv_list, kr_list = [], []
    for l in range(depth):
        p = {k: v[l] for k, v in params.items()}
        w = _prep_weights(p)
        mods = _ada_mod(cond, w_ada[l], b_ada[l]).reshape(n_rows, N_MOD, d)
        n_prompt = x_prompt.shape[0] * x_prompt.shape[1]
        y_sample, _, _, ffn_w = _trunk(y_sample, mods, lambda r: r // dec_seq, min(1024, dec_seq), p, w,
                                       cache_kv_c[:, l], cache_k_rope[:, l], True, None)
        y_prompt, c_kv, k_rope, _ = _trunk(y_prompt, mods, lambda r: dec_b, min(1024, n_prompt), p, w,
                                           None, None, False, ffn_w)
        kv_list.append(c_kv)
        kr_list.append(k_rope)
    return (y_prompt, y_sample, jnp.stack(kv_list, axis=1), jnp.stack(kr_list, axis=1))
```

```python
import functools
import math

import numpy as np
import jax
import jax.numpy as jnp
from jax import lax
from jax.experimental import pallas as pl
from jax.experimental.pallas import tpu as pltpu

F32 = jnp.float32
BF16 = jnp.bfloat16
I32 = jnp.int32
HIGHEST = lax.Precision.HIGHEST

D_MODEL = 2048
GRID_W = 64
MLA_HEADS = 8
QK_NOPE = 128
QK_ROPE = 64
V_HEAD = 128
Q_LORA = 512
KV_LORA = 256
ROPE_THETA = 10000.0
ROPE_FREQS = QK_ROPE // 4
HY_WIDTH = 1024
HY_EMB = 33
HY_BANDS = (HY_EMB - 1) // 2
HY_HIDDEN = 64
HY_FAST_DECAY = 0.3
HY_SLOW_DECAY = 1.5
HY_TARGET = 0.01
N_EXPERTS = 16
EXPERT_FF = 1024
EC_CAPACITY = 2
EPS = 1e-6
DEPTH = 1
ALPHA = (2 * DEPTH) ** 0.25
N_MOD = 6

LANES = 128
HEAD_PAD = 256
SMALL_W = 1024
BIG_W = 2 * D_MODEL + 3 * HY_WIDTH
VMEM_LIMIT = 56 * 1024 * 1024
HY_TIME_BLOCKS = 4


def _cparams(sem, vmem=VMEM_LIMIT):
    return pltpu.CompilerParams(dimension_semantics=sem, vmem_limit_bytes=vmem)


def _sigmoid(x):
    return 1.0 / (1.0 + jnp.exp(-x))


def _const_spec(shape):
    nd = len(shape)
    return pl.BlockSpec(shape, lambda *_: (0,) * nd, pipeline_mode=pl.Buffered(1))


def _ada_kernel(c_ref, w_ref, b_ref, o_ref):
    c = c_ref[...]
    s = (c * _sigmoid(c)).astype(BF16)
    o_ref[...] = jnp.dot(s, w_ref[...].astype(BF16), preferred_element_type=F32) + b_ref[...]


def _ada_mod(cond, w_ada, b_ada):
    r, d = cond.shape
    n = w_ada.shape[1]
    tn = 512
    return pl.pallas_call(
        _ada_kernel,
        grid=(n // tn,),
        in_specs=[pl.BlockSpec((r, d), lambda j: (0, 0)),
                  pl.BlockSpec((d, tn), lambda j: (0, j)),
                  pl.BlockSpec((1, tn), lambda j: (0, j))],
        out_specs=pl.BlockSpec((r, tn), lambda j: (0, j)),
        out_shape=jax.ShapeDtypeStruct((r, n), F32),
        compiler_params=_cparams(("arbitrary",)),
        name="ada_mod",
    )(cond, w_ada, b_ada.reshape(1, n))


def _inproj_kernel(x_ref, mod_ref, w_ref, *rest, n_small, n_cast):
    cast_in = rest[:n_cast]
    small_ref, big_ref = rest[n_cast:n_cast + 2]
    cast_out = rest[n_cast + 2:2 * n_cast + 2]
    h_scr = rest[-1]
    for src, dst in zip(cast_in, cast_out):
        dst[...] = src[...].astype(BF16)
    j = pl.program_id(1)

    @pl.when(j == 0)
    def _():
        m = mod_ref[0]
        h_scr[...] = (x_ref[...] * (1.0 + m[1:2]) + m[0:1]).astype(BF16)

    acc = jnp.dot(h_scr[...], w_ref[...], preferred_element_type=F32)

    @pl.when(j < n_small)
    def _():
        small_ref[...] = acc

    @pl.when(j >= n_small)
    def _():
        big_ref[...] = acc.astype(BF16)


def _inproj(x, mods, mod_row, w_r, tm, cast=()):
    n, d = x.shape
    tn = 1024
    n_small = SMALL_W // tn
    n_cols = w_r.shape[1] // tn
    steps = (n // tm) * n_cols
    side = lambda a: pl.BlockSpec((a.shape[0] // steps, a.shape[1]), lambda i, j: (i * n_cols + j, 0))
    outs = pl.pallas_call(
        functools.partial(_inproj_kernel, n_small=n_small, n_cast=len(cast)),
        grid=(n // tm, n_cols),
        in_specs=[pl.BlockSpec((tm, d), lambda i, j: (i, 0)),
                  pl.BlockSpec((1, N_MOD, d), lambda i, j: (mod_row(i * tm), 0, 0)),
                  pl.BlockSpec((d, tn), lambda i, j: (0, j))] + [side(a) for a in cast],
        out_specs=[pl.BlockSpec((tm, tn), lambda i, j: (i, jnp.minimum(j, n_small - 1))),
                   pl.BlockSpec((tm, tn), lambda i, j: (i, jnp.maximum(j - n_small, 0)))]
        + [side(a) for a in cast],
        out_shape=[jax.ShapeDtypeStruct((n, SMALL_W), F32),
                   jax.ShapeDtypeStruct((n, BIG_W), BF16)]
        + [jax.ShapeDtypeStruct(a.shape, BF16) for a in cast],
        scratch_shapes=[pltpu.VMEM((tm, d), BF16)],
        compiler_params=_cparams(("arbitrary", "arbitrary")),
        name="inproj",
    )(x, mods, w_r, *cast)
    return outs[0], outs[1], tuple(outs[2:])


def _qprep_kernel(*refs, rope):
    if rope:
        small_ref, qg_ref, kvg_ref, wa_ref, wb_ref, c_ref, s_ref, q_ref, ckv_ref, kr_ref = refs
    else:
        small_ref, qg_ref, kvg_ref, wa_ref, q_ref, ckv_ref, kr_ref = refs
    qc = small_ref[:, 0:Q_LORA]
    kvc = small_ref[:, Q_LORA:Q_LORA + KV_LORA]
    kr = small_ref[:, Q_LORA + KV_LORA:Q_LORA + KV_LORA + LANES]
    qn = (qc * lax.rsqrt(jnp.mean(qc * qc, axis=-1, keepdims=True) + EPS) * qg_ref[...]).astype(BF16)
    ckv_ref[...] = kvc * lax.rsqrt(jnp.mean(kvc * kvc, axis=-1, keepdims=True) + EPS) * kvg_ref[...]
    qa = jnp.dot(qn, wa_ref[...], preferred_element_type=F32)
    if rope:
        krs = small_ref[:, Q_LORA + KV_LORA + LANES:SMALL_W]
        qb = jnp.dot(qn, wb_ref[...], preferred_element_type=F32)
        cs = c_ref[...]
        sn = s_ref[...]
        kr_ref[...] = kr * cs + krs * sn
        for h in range(MLA_HEADS):
            lo = h * HEAD_PAD
            q_ref[:, lo:lo + LANES] = qa[:, lo:lo + LANES].astype(BF16)
            q_ref[:, lo + LANES:lo + HEAD_PAD] = (
                qa[:, lo + LANES:lo + HEAD_PAD] * cs + qb[:, h * LANES:(h + 1) * LANES] * sn).astype(BF16)
    else:
        kr_ref[...] = kr
        q_ref[...] = qa.astype(BF16)


def _qprep(small, qg, kvg, wa, wb, cos128, sin128, seq, rope):
    n = small.shape[0]
    tm = min(512, seq)
    nblk = seq // tm
    in_specs = [pl.BlockSpec((tm, SMALL_W), lambda i: (i, 0)),
                pl.BlockSpec((1, Q_LORA), lambda i: (0, 0)),
                pl.BlockSpec((1, KV_LORA), lambda i: (0, 0)),
                pl.BlockSpec(wa.shape, lambda i: (0, 0))]
    args = [small, qg, kvg, wa]
    if rope:
        in_specs += [pl.BlockSpec(wb.shape, lambda i: (0, 0)),
                     pl.BlockSpec((tm, LANES), lambda i: (i % nblk, 0)),
                     pl.BlockSpec((tm, LANES), lambda i: (i % nblk, 0))]
        args += [wb, cos128, sin128]
    return pl.pallas_call(
        functools.partial(_qprep_kernel, rope=rope),
        grid=(n // tm,),
        in_specs=in_specs,
        out_specs=[pl.BlockSpec((tm, MLA_HEADS * HEAD_PAD), lambda i: (i, 0)),
                   pl.BlockSpec((tm, KV_LORA), lambda i: (i, 0)),
                   pl.BlockSpec((tm, LANES), lambda i: (i, 0))],
        out_shape=[jax.ShapeDtypeStruct((n, MLA_HEADS * HEAD_PAD), BF16),
                   jax.ShapeDtypeStruct((n, KV_LORA), F32),
                   jax.ShapeDtypeStruct((n, LANES), F32)],
        compiler_params=_cparams(("arbitrary",)),
        name="qprep",
    )(*args)


def _kvup_kernel(kv_ref, kr_ref, wk_ref, wv_ref, k_ref, v_ref):
    kv = kv_ref[0].astype(BF16)
    kn = jnp.dot(kv, wk_ref[...], preferred_element_type=F32)
    vv = jnp.dot(kv, wv_ref[...], preferred_element_type=F32)
    krp = kr_ref[0].astype(BF16)
    lane = lax.broadcasted_iota(I32, (kv.shape[0], LANES), 1)
    ones_col = jnp.where(lane == 0, 1.0, 0.0).astype(BF16)
    for h in range(MLA_HEADS):
        k_ref[0, h, :, 0:LANES] = kn[:, h * QK_NOPE:(h + 1) * QK_NOPE].astype(BF16)
        k_ref[0, h, :, LANES:HEAD_PAD] = krp
        v_ref[0, h, :, 0:V_HEAD] = vv[:, h * V_HEAD:(h + 1) * V_HEAD].astype(BF16)
        v_ref[0, h, :, V_HEAD:V_HEAD + LANES] = ones_col


def _kvup(kv_all, kr_all, wk, wv):
    b, lk, _ = kv_all.shape
    tm = 512 if lk % 512 == 0 else 256
    return pl.pallas_call(
        _kvup_kernel,
        grid=(b, lk // tm),
        in_specs=[pl.BlockSpec((1, tm, KV_LORA), lambda bi, i: (bi, i, 0)),
                  pl.BlockSpec((1, tm, LANES), lambda bi, i: (bi, i, 0)),
                  pl.BlockSpec(wk.shape, lambda bi, i: (0, 0)),
                  pl.BlockSpec(wv.shape, lambda bi, i: (0, 0))],
        out_specs=[pl.BlockSpec((1, MLA_HEADS, tm, HEAD_PAD), lambda bi, i: (bi, 0, i, 0)),
                   pl.BlockSpec((1, MLA_HEADS, tm, V_HEAD + LANES), lambda bi, i: (bi, 0, i, 0))],
        out_shape=[jax.ShapeDtypeStruct((b, MLA_HEADS, lk, HEAD_PAD), BF16),
                   jax.ShapeDtypeStruct((b, MLA_HEADS, lk, V_HEAD + LANES), BF16)],
        compiler_params=_cparams(("arbitrary", "arbitrary")),
        name="kvup",
    )(kv_all, kr_all, wk, wv)


def _attn_kernel(q_ref, k_ref, v_ref, o_ref, *, rows):
    for h in range(k_ref.shape[1]):
        k = k_ref[0, h]
        v = v_ref[0, h]
        for r0 in range(0, q_ref.shape[1], rows):
            q = q_ref[0, r0:r0 + rows, h * HEAD_PAD:(h + 1) * HEAD_PAD]
            s = lax.dot_general(q, k, (((1,), (1,)), ((), ())), preferred_element_type=F32)
            p = jnp.exp2(s - jnp.max(s, axis=-1, keepdims=True)).astype(BF16)
            o = jnp.dot(p, v, preferred_element_type=F32)
            o_ref[0, r0:r0 + rows, h * V_HEAD:(h + 1) * V_HEAD] = (
                o[:, 0:V_HEAD] / o[:, V_HEAD:V_HEAD + 1]).astype(BF16)


def _attention(q, k, v):
    b, l, _ = q.shape
    lk = k.shape[2]
    rows = 256
    chains = 16
    tq = min(l, chains * rows)
    hb = min(MLA_HEADS, max(1, chains * rows // tq))
    return pl.pallas_call(
        functools.partial(_attn_kernel, rows=rows),
        grid=(b, MLA_HEADS // hb, l // tq),
        in_specs=[pl.BlockSpec((1, tq, hb * HEAD_PAD), lambda bi, h, i: (bi, i, h)),
                  pl.BlockSpec((1, hb, lk, HEAD_PAD), lambda bi, h, i: (bi, h, 0, 0)),
                  pl.BlockSpec((1, hb, lk, V_HEAD + LANES), lambda bi, h, i: (bi, h, 0, 0))],
        out_specs=pl.BlockSpec((1, tq, hb * V_HEAD), lambda bi, h, i: (bi, i, h)),
        out_shape=jax.ShapeDtypeStruct((b, l, MLA_HEADS * V_HEAD), BF16),
        compiler_params=_cparams(("arbitrary", "arbitrary", "arbitrary")),
        name="attention",
    )(q, k, v)


def _hypre_kernel(x0_ref, x1_ref, v_ref, w_ref, b_ref, x0o_ref, z_ref, zn_ref, *, nb):
    seq = x0_ref.shape[1]
    tb = seq // nb
    row = lax.broadcasted_iota(I32, (seq, 1), 0)

    def conv(u_ref, g):
        u = u_ref[0].astype(F32)
        up = jnp.where(row == 0, 0.0, pltpu.roll(u, 1, axis=0))
        un = jnp.where(row == seq - 1, 0.0, pltpu.roll(u, seq - 1, axis=0))
        w = w_ref[g]
        return up * w[0:1] + u * w[1:2] + un * w[2:3] + b_ref[g]

    x0o_ref[0] = conv(x0_ref, 0).astype(BF16)
    z = conv(x1_ref, 1) * conv(v_ref, 2)
    z_ref[0] = z.astype(BF16)
    zs = z * (1 - 2 * (row & 1)).astype(F32)
    for j in range(nb):
        zn_ref[0, j:j + 1, :] = jnp.sum(zs[j * tb:(j + 1) * tb], axis=0, keepdims=True)


def _hypre(big3, w3, b3, nb):
    b, l, _ = big3.shape
    cb = 256
    ncb = HY_WIDTH // cb
    off = 2 * D_MODEL // cb
    return pl.pallas_call(
        functools.partial(_hypre_kernel, nb=nb),
        grid=(b, ncb),
        in_specs=[pl.BlockSpec((1, l, cb), lambda bi, j: (bi, 0, off + j)),
                  pl.BlockSpec((1, l, cb), lambda bi, j: (bi, 0, off + j + ncb)),
                  pl.BlockSpec((1, l, cb), lambda bi, j: (bi, 0, off + j + 2 * ncb)),
                  pl.BlockSpec((3, 3, cb), lambda bi, j: (0, 0, j)),
                  pl.BlockSpec((3, 1, cb), lambda bi, j: (0, 0, j))],
        out_specs=[pl.BlockSpec((1, l, cb), lambda bi, j: (bi, 0, j)),
                   pl.BlockSpec((1, l, cb), lambda bi, j: (bi, 0, j)),
                   pl.BlockSpec((1, nb, cb), lambda bi, j: (bi, 0, j))],
        out_shape=[jax.ShapeDtypeStruct((b, l, HY_WIDTH), BF16),
                   jax.ShapeDtypeStruct((b, l, HY_WIDTH), BF16),
                   jax.ShapeDtypeStruct((b, nb, HY_WIDTH), F32)],
        compiler_params=_cparams(("arbitrary", "arbitrary")),
        name="hyena_pre",
    )(big3, big3, big3, w3, b3)


def _filter_kernel(feat_ref, w1_ref, b1_ref, f1_ref, w2_ref, b2_ref, f2_ref,
                   w3f_ref, w3b_ref, b3f_ref, b3b_ref, dl_ref, hf_ref, hb_ref, ext_ref, hd_scr, *, nb):
    seq = feat_ref.shape[0]
    tb = seq // nb

    @pl.when(pl.program_id(0) == 0)
    def _():
        h1 = jnp.sin(f1_ref[...] * (jnp.dot(feat_ref[...], w1_ref[...], precision=HIGHEST,
                                            preferred_element_type=F32) + b1_ref[...]))
        hd_scr[...] = jnp.sin(f2_ref[...] * (jnp.dot(h1, w2_ref[...], precision=HIGHEST,
                                                     preferred_element_type=F32) + b2_ref[...]))

    hd = hd_scr[...]
    row = lax.broadcasted_iota(I32, (seq, 1), 0)
    t = row.astype(F32) / float(max(seq - 1, 1))
    window = jnp.exp(-t * dl_ref[...])
    hf = (jnp.dot(hd, w3f_ref[...], precision=HIGHEST, preferred_element_type=F32) + b3f_ref[...]) * window
    hb = (jnp.dot(hd, w3b_ref[...], precision=HIGHEST, preferred_element_type=F32) + b3b_ref[...]) * window
    hb = jnp.where(row == 0, 0.0, hb)
    den = jnp.sum(jnp.abs(hf) + jnp.abs(hb), axis=0, keepdims=True) + EPS
    hf = hf / den
    hb = hb / den
    hf_ref[...] = hf.astype(BF16)
    hb_ref[...] = hb.astype(BF16)
    sign = (1 - 2 * (row & 1)).astype(F32)
    for j in range(nb):
        sl = slice(j * tb, (j + 1) * tb)
        ext_ref[0, j:j + 1, :] = hf[j * tb:j * tb + 1].astype(BF16).astype(F32)
        ext_ref[1, j:j + 1, :] = hb[j * tb:j * tb + 1].astype(BF16).astype(F32)
        ext_ref[2, j:j + 1, :] = jnp.sum(hf[sl] * sign[sl], axis=0, keepdims=True)
        ext_ref[3, j:j + 1, :] = jnp.sum(hb[sl] * sign[sl], axis=0, keepdims=True)


def _filter_td(seq, p, nb):
    pos = np.arange(seq, dtype=np.float64)
    t = pos / max(seq - 1, 1)
    bands = np.linspace(1e-4, HY_BANDS - 1, HY_BANDS)
    w = 2.0 * math.pi * pos / seq
    feats = np.concatenate([t[:, None], np.cos(w[:, None] * bands), -np.sin(w[:, None] * bands)], axis=-1)
    feats = jnp.asarray(np.pad(feats, ((0, 0), (0, LANES - HY_EMB))), F32)
    w1 = jnp.pad(p['hy_filt_w1'], ((0, LANES - HY_EMB), (0, 0)))
    deltas = jnp.asarray(np.abs(np.linspace(math.log(HY_TARGET) / HY_SLOW_DECAY, math.log(HY_TARGET) / HY_FAST_DECAY,
                                            HY_WIDTH)).reshape(1, HY_WIDTH), F32)
    cb = 256
    ncb = HY_WIDTH // cb
    row = lambda v: v.reshape(1, -1)
    c2 = lambda j: (0, 0)
    return pl.pallas_call(
        functools.partial(_filter_kernel, nb=nb),
        grid=(ncb,),
        in_specs=[pl.BlockSpec((seq, LANES), c2),
                  pl.BlockSpec((LANES, HY_HIDDEN), c2), pl.BlockSpec((1, HY_HIDDEN), c2),
                  pl.BlockSpec((1, HY_HIDDEN), c2),
                  pl.BlockSpec((HY_HIDDEN, HY_HIDDEN), c2), pl.BlockSpec((1, HY_HIDDEN), c2),
                  pl.BlockSpec((1, HY_HIDDEN), c2),
                  pl.BlockSpec((HY_HIDDEN, cb), lambda j: (0, j)),
                  pl.BlockSpec((HY_HIDDEN, cb), lambda j: (0, j + ncb)),
                  pl.BlockSpec((1, cb), lambda j: (0, j)),
                  pl.BlockSpec((1, cb), lambda j: (0, j + ncb)),
                  pl.BlockSpec((1, cb), lambda j: (0, j))],
        out_specs=[pl.BlockSpec((seq, cb), lambda j: (0, j)),
                   pl.BlockSpec((seq, cb), lambda j: (0, j)),
                   pl.BlockSpec((4, nb, cb), lambda j: (0, 0, j))],
        out_shape=[jax.ShapeDtypeStruct((seq, HY_WIDTH), BF16),
                   jax.ShapeDtypeStruct((seq, HY_WIDTH), BF16),
                   jax.ShapeDtypeStruct((4, nb, HY_WIDTH), F32)],
        scratch_shapes=[pltpu.VMEM((seq, HY_HIDDEN), F32)],
        compiler_params=_cparams(("arbitrary",)),
        name="hyena_filter",
    )(feats, w1, row(p['hy_filt_b1']), row(p['hy_filt_freq1']),
      p['hy_filt_w2'], row(p['hy_filt_b2']), row(p['hy_filt_freq2']),
      p['hy_filt_w3'], p['hy_filt_w3'], row(p['hy_filt_b3']), row(p['hy_filt_b3']), deltas)


def _kspec_kernel(c_ref, s_ref, hf_ref, hb_ref, ext_ref, kc_ref, ks_ref, kn_ref, *, nb):
    tb = c_ref.shape[0]
    f = lax.broadcasted_iota(I32, (tb, 1), 0)
    wgt = jnp.where(f == 0, 1.0, 2.0) / float(2 * tb)
    sig = (1 - 2 * (f & 1)).astype(F32)
    cm = c_ref[...]
    sm = s_ref[...]
    blk = lambda ref, j: ref[j * tb:(j + 1) * tb, :]
    chf = [jnp.dot(cm, blk(hf_ref, j), preferred_element_type=F32) for j in range(nb)]
    shf = [jnp.dot(sm, blk(hf_ref, j), preferred_element_type=F32) for j in range(nb)]
    chb = [jnp.dot(cm, blk(hb_ref, j), preferred_element_type=F32) for j in range(nb)]
    shb = [jnp.dot(sm, blk(hb_ref, j), preferred_element_type=F32) for j in range(nb)]
    first_f = lambda j: ext_ref[0, j:j + 1, :]
    first_b = lambda j: ext_ref[1, j:j + 1, :]
    alt_f = lambda j: ext_ref[2, j:j + 1, :]
    alt_b = lambda j: ext_ref[3, j:j + 1, :]
    for d in range(-(nb - 1), nb):
        k = d + nb - 1
        if d == 0:
            kc, ks, kn = chf[0] + chb[0], shf[0] - shb[0], alt_f(0) + alt_b(0)
        elif d > 0:
            kc = chf[d] + sig * (chf[d - 1] - first_f(d - 1))
            ks = shf[d] + sig * shf[d - 1]
            kn = alt_f(d) + alt_f(d - 1) - first_f(d - 1)
        else:
            e = -d
            kc = chb[e] + sig * (chb[e - 1] - first_b(e - 1))
            ks = -(shb[e] + sig * shb[e - 1])
            kn = alt_b(e) + alt_b(e - 1) - first_b(e - 1)
        kc_ref[k] = kc * wgt
        ks_ref[k] = ks * wgt
        kn_ref[k:k + 1, :] = kn * (1.0 / float(2 * tb))


def _kspec(cmat, smat, hf, hb, ext, nb):
    tb = cmat.shape[0]
    seq = hf.shape[0]
    nd = 2 * nb - 1
    cb = 256 if nb > 1 else 512
    return pl.pallas_call(
        functools.partial(_kspec_kernel, nb=nb),
        grid=(HY_WIDTH // cb,),
        in_specs=[pl.BlockSpec((tb, tb), lambda j: (0, 0)),
                  pl.BlockSpec((tb, tb), lambda j: (0, 0)),
                  pl.BlockSpec((seq, cb), lambda j: (0, j)),
                  pl.BlockSpec((seq, cb), lambda j: (0, j)),
                  pl.BlockSpec((4, nb, cb), lambda j: (0, 0, j))],
        out_specs=[pl.BlockSpec((nd, tb, cb), lambda j: (0, 0, j)),
                   pl.BlockSpec((nd, tb, cb), lambda j: (0, 0, j)),
                   pl.BlockSpec((nd, cb), lambda j: (0, j))],
        out_shape=[jax.ShapeDtypeStruct((nd, tb, HY_WIDTH), F32),
                   jax.ShapeDtypeStruct((nd, tb, HY_WIDTH), F32),
                   jax.ShapeDtypeStruct((nd, HY_WIDTH), F32)],
        compiler_params=_cparams(("arbitrary",)),
        name="hyena_kspec",
    )(cmat, smat, hf, hb, ext)


def _fwd_kernel(c_ref, s_ref, z_ref, kc_ref, ks_ref, pc_ref, ps_ref, *, nb):
    tb = c_ref.shape[0]
    cm = c_ref[...]
    sm = s_ref[...]
    zc = [jnp.dot(cm, z_ref[0, j * tb:(j + 1) * tb, :], preferred_element_type=F32) for j in range(nb)]
    zs = [jnp.dot(sm, z_ref[0, j * tb:(j + 1) * tb, :], preferred_element_type=F32) for j in range(nb)]
    for i in range(nb):
        pc = None
        ps = None
        for j in range(nb):
            kc = kc_ref[i - j + nb - 1]
            ks = ks_ref[i - j + nb - 1]
            tc = zc[j] * kc - zs[j] * ks
            ts = zc[j] * ks + zs[j] * kc
            pc = tc if pc is None else pc + tc
            ps = ts if ps is None else ps + ts
        pc_ref[0, i * tb:(i + 1) * tb, :] = pc.astype(BF16)
        ps_ref[0, i * tb:(i + 1) * tb, :] = ps.astype(BF16)


def _hy_fwd(cmat, smat, z, kc, ks, nb):
    b, seq, _ = z.shape
    tb = seq // nb
    nd = 2 * nb - 1
    cb = 256 if nb > 1 else HY_WIDTH
    return pl.pallas_call(
        functools.partial(_fwd_kernel, nb=nb),
        grid=(HY_WIDTH // cb, b),
        in_specs=[pl.BlockSpec((tb, tb), lambda j, bi: (0, 0)),
                  pl.BlockSpec((tb, tb), lambda j, bi: (0, 0)),
                  pl.BlockSpec((1, seq, cb), lambda j, bi: (bi, 0, j)),
                  pl.BlockSpec((nd, tb, cb), lambda j, bi: (0, 0, j)),
                  pl.BlockSpec((nd, tb, cb), lambda j, bi: (0, 0, j))],
        out_specs=[pl.BlockSpec((1, seq, cb), lambda j, bi: (bi, 0, j)),
                   pl.BlockSpec((1, seq, cb), lambda j, bi: (bi, 0, j))],
        out_shape=[jax.ShapeDtypeStruct((b, seq, HY_WIDTH), BF16),
                   jax.ShapeDtypeStruct((b, seq, HY_WIDTH), BF16)],
        compiler_params=_cparams(("arbitrary", "arbitrary")),
        name="hyena_fwd",
    )(cmat, smat, z, kc, ks)


def _inv_kernel(c_ref, s_ref, pc_ref, ps_ref, z_ref, x0_ref, zn_ref, kn_ref, skip_ref, o_ref, *, nb):
    tb = c_ref.shape[0]
    cm = c_ref[...]
    sm = s_ref[...]
    t = lax.broadcasted_iota(I32, (tb, 1), 0)
    sign = (1 - 2 * (t & 1)).astype(F32)
    ys = [jnp.dot(cm, pc_ref[0, i * tb:(i + 1) * tb, :], preferred_element_type=F32)
          + jnp.dot(sm, ps_ref[0, i * tb:(i + 1) * tb, :], preferred_element_type=F32) for i in range(nb)]
    for i in range(nb):
        rs = slice(i * tb, (i + 1) * tb)
        y = ys[i]
        nyq = None
        for j in range(nb):
            k = i - j + nb - 1
            term = zn_ref[0, j:j + 1, :] * kn_ref[k:k + 1, :]
            nyq = term if nyq is None else nyq + term
        y = y + sign * nyq + z_ref[0, rs, :].astype(F32) * skip_ref[...]
        o_ref[0, rs, :] = (x0_ref[0, rs, :].astype(F32) * y).astype(BF16)


def _hy_inv(cmat, smat, pc, ps, z, x0, zn, kn, skip, nb):
    b, seq, _ = z.shape
    tb = seq // nb
    nd = 2 * nb - 1
    cb = 256 if nb > 1 else HY_WIDTH
    blk = pl.BlockSpec((1, seq, cb), lambda j, bi: (bi, 0, j))
    return pl.pallas_call(
        functools.partial(_inv_kernel, nb=nb),
        grid=(HY_WIDTH // cb, b),
        in_specs=[pl.BlockSpec((tb, tb), lambda j, bi: (0, 0)),
                  pl.BlockSpec((tb, tb), lambda j, bi: (0, 0)),
                  blk, blk, blk, blk,
                  pl.BlockSpec((1, nb, cb), lambda j, bi: (bi, 0, j)),
                  pl.BlockSpec((nd, cb), lambda j, bi: (0, j)),
                  pl.BlockSpec((1, cb), lambda j, bi: (0, j))],
        out_specs=blk,
        out_shape=jax.ShapeDtypeStruct((b, seq, HY_WIDTH), BF16),
        compiler_params=_cparams(("arbitrary", "arbitrary")),
        name="hyena_inv",
    )(cmat, smat, pc, ps, z, x0, zn, kn, skip)


def _layer_norm(v, g, b):
    mu = jnp.mean(v, axis=-1, keepdims=True)
    vc = v - mu
    var = jnp.mean(vc * vc, axis=-1, keepdims=True)
    return vc * lax.rsqrt(var + EPS) * g + b


def _merge_kernel(at_ref, hy_ref, ga_ref, gb_ref, woa_ref, woh_ref, o_ref):
    a = jnp.dot(at_ref[...], woa_ref[...], preferred_element_type=F32)
    hh = jnp.dot(hy_ref[...], woh_ref[...], preferred_element_type=F32)
    merged = _sigmoid(ga_ref[...].astype(F32)) * a + _sigmoid(gb_ref[...].astype(F32)) * hh
    o_ref[...] = merged.astype(BF16)


def _merge(attn, hy, big, woa, woh):
    n = attn.shape[0]
    d = woa.shape[1]
    tm = 512
    return pl.pallas_call(
        _merge_kernel,
        grid=(n // tm,),
        in_specs=[pl.BlockSpec((tm, attn.shape[1]), lambda i: (i, 0)),
                  pl.BlockSpec((tm, hy.shape[1]), lambda i: (i, 0)),
                  pl.BlockSpec((tm, d), lambda i: (i, 0)),
                  pl.BlockSpec((tm, d), lambda i: (i, 1)),
                  _const_spec(woa.shape), _const_spec(woh.shape)],
        out_specs=pl.BlockSpec((tm, d), lambda i: (i, 0)),
        out_shape=jax.ShapeDtypeStruct((n, d), BF16),
        compiler_params=_cparams(("arbitrary",)),
        name="merge",
    )(attn, hy, big, big, woa, woh)


def _ln1_kernel(x_ref, mg_ref, mod_ref, wout_ref, g_ref, b_ref, wr_ref, x1_ref, h2_ref, lg_ref, *, n_sub):
    m = mod_ref[0]
    rows = x_ref.shape[0] // n_sub
    mixes = [jnp.dot(mg_ref[t * rows:(t + 1) * rows, :], wout_ref[...], preferred_element_type=F32)
             for t in range(n_sub)]
    for t in range(n_sub):
        rs = slice(t * rows, (t + 1) * rows)
        x1 = _layer_norm(ALPHA * x_ref[rs, :] + m[2:3] * mixes[t], g_ref[...], b_ref[...])
        x1_ref[rs, :] = x1
        h2 = x1 * (1.0 + m[4:5]) + m[3:4]
        h2_ref[rs, :] = h2
        hi = h2.astype(BF16)
        lo = (h2 - hi.astype(F32)).astype(BF16)
        lg = (jnp.dot(hi, wr_ref[...], preferred_element_type=F32)
              + jnp.dot(lo, wr_ref[...], preferred_element_type=F32))
        lg_ref[rs, :] = lg + pltpu.roll(lg, LANES - N_EXPERTS, axis=1)


def _ln1(x, merged, mods, mod_row, wout, g, b, wr_pack):
    n, d = x.shape
    tm = 512
    return pl.pallas_call(
        functools.partial(_ln1_kernel, n_sub=2),
        grid=(n // tm,),
        in_specs=[pl.BlockSpec((tm, d), lambda i: (i, 0)),
                  pl.BlockSpec((tm, d), lambda i: (i, 0)),
                  pl.BlockSpec((1, N_MOD, d), lambda i: (mod_row(i * tm), 0, 0)),
                  _const_spec(wout.shape), _const_spec((1, d)), _const_spec((1, d)),
                  _const_spec(wr_pack.shape)],
        out_specs=[pl.BlockSpec((tm, d), lambda i: (i, 0)),
                   pl.BlockSpec((tm, d), lambda i: (i, 0)),
                   pl.BlockSpec((tm, LANES), lambda i: (i, 0))],
        out_shape=[jax.ShapeDtypeStruct((n, d), F32),
                   jax.ShapeDtypeStruct((n, d), F32),
                   jax.ShapeDtypeStruct((n, LANES), F32)],
        compiler_params=_cparams(("arbitrary",)),
        name="ln1_router",
    )(x, merged, mods, wout, g, b, wr_pack)


def _split3(t):
    t1 = t.astype(BF16)
    r = t - t1.astype(F32)
    t2 = r.astype(BF16)
    t3 = (r - t2.astype(F32)).astype(BF16)
    return t1, t2, t3


def _dot_exact(sel01, table):
    t1, t2, t3 = _split3(table)
    out = jnp.dot(sel01, t1, preferred_element_type=F32)
    out = out + jnp.dot(sel01, t2, preferred_element_type=F32)
    return out + jnp.dot(sel01, t3, preferred_element_type=F32)


def _route_kernel(lg_ref, idx_ref, q_ref, gate_ref, start_ref, aff_scr, sel_scr, pos_scr, *, cap):
    ne, nc, _ = lg_ref.shape
    lg = lg_ref[...]
    mx = jnp.max(lg, axis=0, keepdims=True)
    ex = jnp.exp(lg - mx)
    aff = ex / jnp.sum(ex, axis=0, keepdims=True)
    aff_scr[...] = aff

    def tbody(k, thr):
        cand = thr | lax.shift_left(jnp.int32(1), 30 - k)
        ge = jnp.where(aff >= lax.bitcast_convert_type(cand, F32), 1.0, 0.0)
        cnt = jnp.sum(jnp.sum(ge, axis=1, keepdims=True), axis=2, keepdims=True)
        return jnp.where(cnt >= float(cap), cand, thr)

    thr = lax.fori_loop(0, 31, tbody, jnp.zeros((ne, 1, 1), I32))
    lo = lax.bitcast_convert_type(thr, F32)
    above = aff >= lax.bitcast_convert_type(thr + 1, F32)
    sel_scr[...] = jnp.where(above, 1.0, 0.0)
    pos_scr[...] = jnp.where((aff >= lo) & jnp.logical_not(above), aff, -1.0)

    r_i = lax.broadcasted_iota(I32, (LANES, LANES), 0)
    c_i = lax.broadcasted_iota(I32, (LANES, LANES), 1)
    upper_incl = jnp.where(r_i <= c_i, 1.0, 0.0).astype(BF16)
    ones_sq = jnp.ones((LANES, LANES), BF16)
    rr_i = lax.broadcasted_iota(I32, (nc, nc), 0)
    cc_i = lax.broadcasted_iota(I32, (nc, nc), 1)
    lower_strict = jnp.where(cc_i < rr_i, 1.0, 0.0).astype(BF16)
    upper_rows = jnp.where(rr_i <= cc_i, 1.0, 0.0).astype(BF16)

    def prefix(x01):
        xb = x01.astype(BF16)
        p1 = jnp.dot(xb, upper_incl, preferred_element_type=F32)
        totb = jnp.dot(xb, ones_sq, preferred_element_type=F32)
        offs = jnp.dot(lower_strict, totb.astype(BF16), preferred_element_type=F32)
        return p1, offs

    tok = (lax.broadcasted_iota(I32, (nc, LANES), 0) * LANES
           + lax.broadcasted_iota(I32, (nc, LANES), 1)).astype(F32)

    def reduce2(fn, v):
        return fn(fn(v, axis=0, keepdims=True), axis=1, keepdims=True)

    def pass1(e, start):
        above_e = sel_scr[e]
        need = cap - jnp.sum(above_e).astype(I32)

        def pick(_, carry):
            vals, picked = carry
            first = reduce2(jnp.min, jnp.where(vals == reduce2(jnp.max, vals), tok, float(nc * LANES)))
            hit = tok == first
            return jnp.where(hit, -1.0, vals), picked + jnp.where(hit, 1.0, 0.0)

        _, picked = lax.fori_loop(0, need, pick, (pos_scr[e], jnp.zeros((nc, LANES), F32)))
        sel = above_e + picked
        sel_scr[e] = sel
        s1, soffs = prefix(sel)
        return start + (s1 + soffs - sel)

    start = lax.fori_loop(0, ne, pass1, jnp.zeros((nc, LANES), F32))
    start_ref[...] = start.astype(I32)

    p_col = lax.broadcasted_iota(I32, (cap, 1), 0).astype(F32)
    lane_row = lax.broadcasted_iota(I32, (1, LANES), 1).astype(F32)
    chunk_row = lax.broadcasted_iota(I32, (1, nc), 1).astype(F32)
    ones8 = jnp.ones((8, LANES), BF16)

    def row_sums(m):
        ones = jnp.ones((8, m.shape[1]), BF16)
        nt = lambda part: lax.dot_general(ones, part, (((1,), (1,)), ((), ())), preferred_element_type=F32)
        m1, m2, m3 = _split3(m)
        return (nt(m1) + nt(m2) + nt(m3))[0:1]

    def pass2(e, before):
        sel = sel_scr[e]
        selb = sel.astype(BF16)
        p1, offs = prefix(sel)
        tot_row = lax.dot_general(ones8, selb, (((1,), (1,)), ((), ())), preferred_element_type=F32)
        cum_row = jnp.dot(tot_row.astype(BF16), upper_rows, preferred_element_type=F32)
        cum1 = cum_row[0:1]
        prev1 = cum1 - tot_row[0:1]
        in_chunk = (prev1 <= p_col) & (p_col < cum1)
        ohc = jnp.where(in_chunk, 1.0, 0.0).astype(BF16)
        keyg = jnp.dot(ohc, (p1 * sel).astype(BF16), preferred_element_type=F32)
        offg = _dot_exact(ohc, offs)
        ohl = keyg == (p_col - offg + 1.0)
        l_p = row_sums(jnp.where(ohl, lane_row, 0.0))
        c_p = row_sums(jnp.where(in_chunk, chunk_row, 0.0))
        idx_ref[e] = (c_p * float(LANES) + l_p).astype(I32)
        qtab = _dot_exact(ohc, start) + jnp.dot(ohc, before.astype(BF16), preferred_element_type=F32)
        q_ref[e] = row_sums(jnp.where(ohl, qtab, 0.0)).astype(I32)
        afg = _dot_exact(ohc, aff_scr[e])
        gate_ref[e] = row_sums(jnp.where(ohl, afg, 0.0))
        return before + sel

    lax.fori_loop(0, ne, pass2, jnp.zeros((nc, LANES), F32))


def _route(lg3, cap):
    ne, nc, _ = lg3.shape
    full = lambda s: pl.BlockSpec(s, lambda: (0,) * len(s))
    return pl.pallas_call(
        functools.partial(_route_kernel, cap=cap),
        in_specs=[full(lg3.shape)],
        out_specs=[full((ne, 1, cap)), full((ne, 1, cap)), full((ne, 1, cap)), full((nc, LANES))],
        out_shape=[jax.ShapeDtypeStruct((ne, 1, cap), I32),
                   jax.ShapeDtypeStruct((ne, 1, cap), I32),
                   jax.ShapeDtypeStruct((ne, 1, cap), F32),
                   jax.ShapeDtypeStruct((nc, LANES), I32)],
        scratch_shapes=[pltpu.VMEM((ne, nc, LANES), F32),
                        pltpu.VMEM((ne, nc, LANES), F32),
                        pltpu.VMEM((ne, nc, LANES), F32)],
        compiler_params=pltpu.CompilerParams(vmem_limit_bytes=VMEM_LIMIT),
        name="route",
    )(lg3)


def _ffn_kernel(idx_ref, q_ref, gate_ref, wg_ref, wu_ref, wd_ref, h_hbm, y_hbm,
                xb, xa, yb, ya, gsem, ssem, *, tm, npair, total):
    step = pl.program_id(0) * npair + pl.program_id(1)
    t0 = 2 * step
    t1 = t0 + 1

    def gather_row(base, r, buf, s):
        return pltpu.make_async_copy(h_hbm.at[pl.ds(idx_ref[base + r], 1), :], buf.at[pl.ds(r, 1), :], gsem.at[s])

    def scatter_row(base, r, buf, s):
        return pltpu.make_async_copy(buf.at[pl.ds(r, 1), :], y_hbm.at[pl.ds(q_ref[base + r], 1), :], ssem.at[s])

    def wait_gather(buf, s):
        pltpu.make_async_copy(h_hbm.at[pl.ds(0, tm), :], buf, gsem.at[s]).wait()

    def wait_scatter(buf, s):
        pltpu.make_async_copy(buf, y_hbm.at[pl.ds(0, tm), :], ssem.at[s]).wait()

    n_chunk = 4
    fc = EXPERT_FF // n_chunk
    per = tm // n_chunk

    def compute(x_buf, y_buf, r0, gather_base, gather_buf, scatter_base, scatter_buf, s):
        x = x_buf[...].astype(BF16)
        gate = gate_ref[r0:r0 + tm, :]
        def up(c):
            g = jnp.dot(x, wg_ref[0, :, c * fc:(c + 1) * fc], preferred_element_type=F32)
            u = jnp.dot(x, wu_ref[0, :, c * fc:(c + 1) * fc], preferred_element_type=F32)
            return g, u

        nxt = up(0)
        for c in range(n_chunk):
            for r in range(c * per, (c + 1) * per):
                gather_row(gather_base, r, gather_buf, s).start()
            for r in range(c * per, (c + 1) * per):
                scatter_row(scatter_base, r, scatter_buf, s).start()
            g, u = nxt
            if c + 1 < n_chunk:
                nxt = up(c + 1)
            hid = (g * _sigmoid(g) * u * gate).astype(BF16)
            part = jnp.dot(hid, wd_ref[0, c * fc:(c + 1) * fc, :], preferred_element_type=F32)
            if c == 0:
                y_buf[...] = part
            else:
                y_buf[...] += part

    @pl.when(step == 0)
    def _():
        yb[...] = jnp.zeros(yb.shape, F32)

        def body(r, carry):
            gather_row(0, r, xa, 0).start()
            return carry

        lax.fori_loop(0, tm, body, 0)

    wait_gather(xa, 0)

    @pl.when(step >= 1)
    def _():
        wait_scatter(ya, 0)

    base1 = t1 * tm
    pbase = jnp.maximum(t0 - 1, 0) * tm
    compute(xa, ya, 0, base1, xb, pbase, yb, 1)

    wait_gather(xb, 1)
    wait_scatter(yb, 1)
    nbase = jnp.minimum(t1 + 1, total - 1) * tm
    base0 = t0 * tm
    compute(xb, yb, tm, nbase, xa, base0, ya, 0)

    @pl.when(t1 == total - 1)
    def _():
        wait_gather(xa, 0)
        wait_scatter(ya, 0)

        def body(r, carry):
            scatter_row(base1, r, yb, 1).start()
            return carry

        lax.fori_loop(0, tm, body, 0)
        wait_scatter(yb, 1)


def _ffn(h2, idx, qpos, gate, wg, wu, wd):
    n, d = h2.shape
    ne = wg.shape[0]
    cap = idx.shape[0] // ne
    tm = min(512, cap // 2)
    npair = cap // (2 * tm)
    total = 2 * ne * npair
    grid_spec = pltpu.PrefetchScalarGridSpec(
        num_scalar_prefetch=2,
        grid=(ne, npair),
        in_specs=[pl.BlockSpec((2 * tm, 1), lambda e, i, *_: (e * npair + i, 0)),
                  pl.BlockSpec((1, d, EXPERT_FF), lambda e, i, *_: (e, 0, 0)),
                  pl.BlockSpec((1, d, EXPERT_FF), lambda e, i, *_: (e, 0, 0)),
                  pl.BlockSpec((1, EXPERT_FF, d), lambda e, i, *_: (e, 0, 0)),
                  pl.BlockSpec(memory_space=pl.ANY)],
        out_specs=pl.BlockSpec(memory_space=pl.ANY),
        scratch_shapes=[pltpu.VMEM((tm, d), F32), pltpu.VMEM((tm, d), F32),
                        pltpu.VMEM((tm, d), F32), pltpu.VMEM((tm, d), F32),
                        pltpu.SemaphoreType.DMA((2,)),
                        pltpu.SemaphoreType.DMA((2,))],
    )
    return pl.pallas_call(
        functools.partial(_ffn_kernel, tm=tm, npair=npair, total=total),
        grid_spec=grid_spec,
        out_shape=jax.ShapeDtypeStruct((ne * cap, d), F32),
        compiler_params=_cparams(("arbitrary", "arbitrary")),
        name="expert_ffn",
    )(idx, qpos, gate, wg, wu, wd, h2)


def _combine_kernel(cs_ref, x1_ref, mod_ref, st_ref, en_ref, g_ref, b_ref, y_hbm, o_ref,
                    ybuf, acc, sem, *, win, total, nsteps):
    i = pl.program_id(0)
    slot = i % 2
    lo0 = cs_ref[i]
    hi = cs_ref[i + 1]
    adv = win - 8

    def win_start(lo):
        return pl.multiple_of((jnp.minimum(lo, total - win) // 8) * 8, 8)

    def window(lo, sl):
        return pltpu.make_async_copy(y_hbm.at[pl.ds(win_start(lo), win), :], ybuf.at[sl], sem.at[sl])

    @pl.when(i == 0)
    def _():
        window(lo0, 0).start()

    @pl.when(i + 1 < nsteps)
    def _():
        window(hi, 1 - slot).start()

    ntrip = jnp.maximum((hi - lo0 + adv - 1) // adv, 1)
    acc[...] = jnp.zeros_like(acc)
    st = st_ref[...]
    en = en_ref[...]

    def body(k, carry):
        lo = lo0 + k * adv

        @pl.when(k > 0)
        def _():
            window(lo, slot).start()

        window(lo, slot).wait()
        qj = win_start(lo) + lax.broadcasted_iota(I32, (1, win), 1)
        first = jnp.maximum(st, lo)
        last = jnp.minimum(en, lo + adv)
        seg = (qj >= first) & (qj < last)
        acc[...] += jnp.dot(jnp.where(seg, 1.0, 0.0).astype(BF16), ybuf[slot].astype(BF16),
                            preferred_element_type=F32)
        return carry

    lax.fori_loop(0, ntrip, body, 0)
    m = mod_ref[0]
    o_ref[...] = _layer_norm(ALPHA * x1_ref[...] + m[5:6] * acc[...], g_ref[...], b_ref[...])


def _combine(x1, mods, mod_row, start, pairs, g, b):
    n, d = x1.shape
    total = pairs.shape[0]
    tc = 256
    win = 640
    st = start.reshape(n, 1)
    en = jnp.concatenate([start[1:], jnp.full((1,), total, I32)]).reshape(n, 1)
    cs = jnp.concatenate([start[::tc], jnp.full((1,), total, I32)])
    grid_spec = pltpu.PrefetchScalarGridSpec(
        num_scalar_prefetch=1,
        grid=(n // tc,),
        in_specs=[pl.BlockSpec((tc, d), lambda i, *_: (i, 0)),
                  pl.BlockSpec((1, N_MOD, d), lambda i, *_: (mod_row(i * tc), 0, 0)),
                  pl.BlockSpec((tc, 1), lambda i, *_: (i, 0)),
                  pl.BlockSpec((tc, 1), lambda i, *_: (i, 0)),
                  pl.BlockSpec((1, d), lambda i, *_: (0, 0)),
                  pl.BlockSpec((1, d), lambda i, *_: (0, 0)),
                  pl.BlockSpec(memory_space=pl.ANY)],
        out_specs=pl.BlockSpec((tc, d), lambda i, *_: (i, 0)),
        scratch_shapes=[pltpu.VMEM((2, win, d), F32),
                        pltpu.VMEM((tc, d), F32),
                        pltpu.SemaphoreType.DMA((2,))],
    )
    return pl.pallas_call(
        functools.partial(_combine_kernel, win=win, total=total, nsteps=n // tc),
        grid_spec=grid_spec,
        out_shape=jax.ShapeDtypeStruct((n, d), F32),
        compiler_params=_cparams(("arbitrary",)),
        name="combine_ln2",
    )(cs, x1, mods, st, en, g, b, pairs)


def _swap_pairs(w):
    f = ROPE_FREQS
    return jnp.concatenate([w[..., f:2 * f], w[..., 0:f], w[..., 3 * f:4 * f], w[..., 2 * f:3 * f]], axis=-1)


def _rope_tables(seq):
    n_rows = seq // GRID_W
    row = np.repeat(np.arange(n_rows, dtype=np.float64), GRID_W)
    col = np.tile(np.arange(GRID_W, dtype=np.float64), n_rows)
    inv = ROPE_THETA ** (-np.arange(ROPE_FREQS, dtype=np.float64) / ROPE_FREQS)
    ar = row[:, None] * inv
    ac = col[:, None] * inv
    z = np.zeros((seq, LANES - QK_ROPE))
    cos128 = np.concatenate([np.cos(ar), np.cos(ar), np.cos(ac), np.cos(ac), z], axis=1)
    sin128 = np.concatenate([-np.sin(ar), np.sin(ar), -np.sin(ac), np.sin(ac), z], axis=1)
    return jnp.asarray(cos128, F32), jnp.asarray(sin128, F32)


def _dft_tables(seq):
    m = 2 * seq
    blk = 64
    s = np.arange(seq, dtype=np.int64)
    ang1 = ((np.arange(0, seq, blk, dtype=np.int64)[:, None] * s[None, :]) % m) * (2.0 * math.pi / m)
    ang0 = ((np.arange(blk, dtype=np.int64)[:, None] * s[None, :]) % m) * (2.0 * math.pi / m)
    c1, s1 = jnp.asarray(np.cos(ang1), F32)[:, None, :], jnp.asarray(np.sin(ang1), F32)[:, None, :]
    c0, s0 = jnp.asarray(np.cos(ang0), F32)[None, :, :], jnp.asarray(np.sin(ang0), F32)[None, :, :]
    cmat = (c1 * c0 - s1 * s0).reshape(seq, seq).astype(BF16)
    smat = (s1 * c0 + c1 * s0).reshape(seq, seq).astype(BF16)
    return cmat, smat


def _relayout_kernel(w_ref, o_ref, *, s1, s2, s3):
    chunk = 512
    width = w_ref.shape[1]
    f = ROPE_FREQS
    o_ref[:, 0:s1] = w_ref[:, 0:s1].astype(BF16)
    x = w_ref[:, s1:s1 + LANES]
    lane = lax.broadcasted_iota(I32, x.shape, 1)
    keep = lane < (s2 - s1)
    partner = jnp.where((lane & f) == 0, pltpu.roll(x, LANES - f, axis=1), pltpu.roll(x, f, axis=1))
    o_ref[:, s1:s1 + LANES] = jnp.where(keep, x, 0.0).astype(BF16)
    o_ref[:, s1 + LANES:SMALL_W] = jnp.where(keep, partner, 0.0).astype(BF16)
    dst = SMALL_W
    for lo, hi in ((s3, width), (s2, s3)):
        for c in range(lo, hi, chunk):
            n = min(chunk, hi - c)
            o_ref[:, dst:dst + n] = w_ref[:, c:c + n].astype(BF16)
            dst += n


def _relayout_w_in(w_in, s1, s2, s3):
    d, width = w_in.shape
    tr = 256
    return pl.pallas_call(
        functools.partial(_relayout_kernel, s1=s1, s2=s2, s3=s3),
        grid=(d // tr,),
        in_specs=[pl.BlockSpec((tr, width), lambda i: (i, 0))],
        out_specs=pl.BlockSpec((tr, SMALL_W + BIG_W), lambda i: (i, 0)),
        out_shape=jax.ShapeDtypeStruct((d, SMALL_W + BIG_W), BF16),
        compiler_params=_cparams(("arbitrary",)),
        name="w_in_relayout",
    )(w_in)


def _prep_weights(p):
    w = {}
    w_in = p['w_in']
    s0, s1, s2, s3, s4 = (Q_LORA, Q_LORA + KV_LORA, Q_LORA + KV_LORA + QK_ROPE,
                          Q_LORA + KV_LORA + QK_ROPE + 3 * HY_WIDTH,
                          Q_LORA + KV_LORA + QK_ROPE + 3 * HY_WIDTH + D_MODEL)
    w['w_in'] = _relayout_w_in(w_in, s1, s2, s3)
    scale = (QK_NOPE + QK_ROPE) ** -0.5 * math.log2(math.e)
    wq = (p['w_uq'] * scale).reshape(Q_LORA, MLA_HEADS, QK_NOPE + QK_ROPE)
    zq = jnp.zeros((Q_LORA, MLA_HEADS, HEAD_PAD - QK_NOPE - QK_ROPE), F32)
    w['wa'] = jnp.concatenate([wq, zq], axis=-1).reshape(Q_LORA, MLA_HEADS * HEAD_PAD).astype(BF16)
    w['wb'] = jnp.concatenate([_swap_pairs(wq[..., QK_NOPE:]), zq], axis=-1).reshape(
        Q_LORA, MLA_HEADS * LANES).astype(BF16)
    wkv = p['w_ukv'].reshape(KV_LORA, MLA_HEADS, QK_NOPE + V_HEAD)
    w['wk'] = wkv[..., :QK_NOPE].reshape(KV_LORA, MLA_HEADS * QK_NOPE).astype(BF16)
    w['wv'] = wkv[..., QK_NOPE:].reshape(KV_LORA, MLA_HEADS * V_HEAD).astype(BF16)
    w['woa'] = p['w_o_mla'].astype(BF16)
    w['woh'] = p['w_o_hy'].astype(BF16)
    w['wout'] = p['w_out'].astype(BF16)
    wr_hi = p['w_router'].astype(BF16)
    wr_lo = (p['w_router'] - wr_hi.astype(F32)).astype(BF16)
    w['wr_pack'] = jnp.pad(jnp.concatenate([wr_hi, wr_lo], axis=1), ((0, 0), (0, LANES - 2 * N_EXPERTS)))
    w['hy_w'] = p['hy_short_w'].reshape(3, 3, HY_WIDTH).transpose(1, 0, 2)
    w['hy_b'] = p['hy_short_b'].reshape(3, 1, HY_WIDTH)
    return w


def _trunk(x3, mods, mod_row, tm_in, p, w, cache_kv, cache_kr, rope, ffn_w):
    b, seq, d = x3.shape
    n = b * seq
    x = x3.reshape(n, d)
    if ffn_w is None:
        shapes = [p[k].shape for k in ('w_gate', 'w_up', 'w_down')]
        flat = [p[k].reshape(-1, p[k].shape[-1]) for k in ('w_gate', 'w_up', 'w_down')]
        small, big, cast = _inproj(x, mods, mod_row, w['w_in'], tm_in, flat)
        ffn_w = tuple(a.reshape(s) for a, s in zip(cast, shapes))
    else:
        small, big, _ = _inproj(x, mods, mod_row, w['w_in'], tm_in)
    cos128, sin128 = _rope_tables(seq) if rope else (None, None)
    q, ckv, kr = _qprep(small, p['q_norm_g'].reshape(1, -1), p['kv_norm_g'].reshape(1, -1),
                        w['wa'], w['wb'], cos128, sin128, seq, rope)
    ckv3 = ckv.reshape(b, seq, KV_LORA)
    kr3 = kr.reshape(b, seq, LANES)
    if cache_kv is None:
        kv_all, kr_all = ckv3, kr3
    else:
        kv_all = jnp.concatenate([cache_kv, ckv3], axis=1)
        kr_all = jnp.concatenate([jnp.pad(cache_kr, ((0, 0), (0, 0), (0, LANES - QK_ROPE))), kr3], axis=1)
    kh, vh = _kvup(kv_all, kr_all, w['wk'], w['wv'])
    attn = _attention(q.reshape(b, seq, MLA_HEADS * HEAD_PAD), kh, vh).reshape(n, MLA_HEADS * V_HEAD)

    nb = HY_TIME_BLOCKS if seq >= 512 else 1
    x0c, z, zn = _hypre(big.reshape(b, seq, BIG_W), w['hy_w'], w['hy_b'], nb)
    hf_td, hb_td, ext = _filter_td(seq, p, nb)
    cmat, smat = _dft_tables(seq // nb)
    kc, ks, kn = _kspec(cmat, smat, hf_td, hb_td, ext, nb)
    pc, ps = _hy_fwd(cmat, smat, z, kc, ks, nb)
    hy = _hy_inv(cmat, smat, pc, ps, z, x0c, zn, kn, p['hy_skip'].reshape(1, -1), nb).reshape(n, HY_WIDTH)

    merged = _merge(attn, hy, big, w['woa'], w['woh'])
    x1, h2, lg = _ln1(x, merged, mods, mod_row, w['wout'], p['ln1_g'].reshape(1, -1),
                      p['ln1_b'].reshape(1, -1), w['wr_pack'])
    cap = EC_CAPACITY * n // N_EXPERTS
    lg3 = lg[:, :N_EXPERTS].T.reshape(N_EXPERTS, n // LANES, LANES)
    idx, qpos, gate, start = _route(lg3, cap)
    pairs = _ffn(h2, idx.reshape(-1), qpos.reshape(-1), gate.reshape(-1, 1), *ffn_w)
    y = _combine(x1, mods, mod_row, start.reshape(-1), pairs,
                 p['ln2_g'].reshape(1, -1), p['ln2_b'].reshape(1, -1))
    return y.reshape(b, seq, d), ckv3, kr3[..., :QK_ROPE], ffn_w


def kernel(x_prompt, x_sample, cache_kv_c, cache_k_rope, c, c_ctx, w_ada, b_ada, w_in, q_norm_g, kv_norm_g, w_uq, w_ukv, w_o_mla, hy_short_w, hy_short_b, hy_filt_w1, hy_filt_b1, hy_filt_freq1, hy_filt_w2, hy_filt_b2, hy_filt_freq2, hy_filt_w3, hy_filt_b3, hy_skip, w_o_hy, w_out, ln1_g, ln1_b, ln2_g, ln2_b, w_router, w_gate, w_up, w_down):
    params = dict(w_in=w_in, q_norm_g=q_norm_g, kv_norm_g=kv_norm_g, w_uq=w_uq, w_ukv=w_ukv, w_o_mla=w_o_mla,
                  hy_short_w=hy_short_w, hy_short_b=hy_short_b, hy_filt_w1=hy_filt_w1, hy_filt_b1=hy_filt_b1,
                  hy_filt_freq1=hy_filt_freq1, hy_filt_w2=hy_filt_w2, hy_filt_b2=hy_filt_b2,
                  hy_filt_freq2=hy_filt_freq2, hy_filt_w3=hy_filt_w3, hy_filt_b3=hy_filt_b3, hy_skip=hy_skip,
                  w_o_hy=w_o_hy, w_out=w_out, ln1_g=ln1_g, ln1_b=ln1_b, ln2_g=ln2_g, ln2_b=ln2_b,
                  w_router=w_router, w_gate=w_gate, w_up=w_up, w_down=w_down)
    depth = w_in.shape[0]
    dec_b, dec_seq, d = x_sample.shape
    n_rows = 16
    cond = jnp.zeros((n_rows, d), F32).at[:dec_b].set(c).at[dec_b].set(c_ctx)
    y_prompt, y_sample = x_prompt, x_sample
    kv_list, kr_list = [], []
    for l in range(depth):
        p = {k: v[l] for k, v in params.items()}
        w = _prep_weights(p)
        mods = _ada_mod(cond, w_ada[l], b_ada[l]).reshape(n_rows, N_MOD, d)
        n_prompt = x_prompt.shape[0] * x_prompt.shape[1]
        y_sample, _, _, ffn_w = _trunk(y_sample, mods, lambda r: r // dec_seq, min(1024, dec_seq), p, w,
                                       cache_kv_c[:, l], cache_k_rope[:, l], True, None)
        y_prompt, c_kv, k_rope, _ = _trunk(y_prompt, mods, lambda r: dec_b, min(1024, n_prompt), p, w,
                                           None, None, False, ffn_w)
        kv_list.append(c_kv)
        kr_list.append(k_rope)
    return (y_prompt, y_sample, jnp.stack(kv_list, axis=1), jnp.stack(kr_list, axis=1))
```

```python
import functools
import math

import numpy as np
import jax
import jax.numpy as jnp
from jax import lax
from jax.experimental import pallas as pl
from jax.experimental.pallas import tpu as pltpu

F32 = jnp.float32
BF16 = jnp.bfloat16
I32 = jnp.int32
HIGHEST = lax.Precision.HIGHEST

D_MODEL = 2048
GRID_W = 64
MLA_HEADS = 8
QK_NOPE = 128
QK_ROPE = 64
V_HEAD = 128
Q_LORA = 512
KV_LORA = 256
ROPE_THETA = 10000.0
ROPE_FREQS = QK_ROPE // 4
HY_WIDTH = 1024
HY_EMB = 33
HY_BANDS = (HY_EMB - 1) // 2
HY_HIDDEN = 64
HY_FAST_DECAY = 0.3
HY_SLOW_DECAY = 1.5
HY_TARGET = 0.01
N_EXPERTS = 16
EXPERT_FF = 1024
EC_CAPACITY = 2
EPS = 1e-6
DEPTH = 1
ALPHA = (2 * DEPTH) ** 0.25
N_MOD = 6

LANES = 128
HEAD_PAD = 256
SMALL_W = 1024
BIG_W = 2 * D_MODEL + 3 * HY_WIDTH
VMEM_LIMIT = 56 * 1024 * 1024
HY_TIME_BLOCKS = 4


def _cparams(sem, vmem=VMEM_LIMIT):
    return pltpu.CompilerParams(dimension_semantics=sem, vmem_limit_bytes=vmem)


def _sigmoid(x):
    return 1.0 / (1.0 + jnp.exp(-x))


def _const_spec(shape):
    nd = len(shape)
    return pl.BlockSpec(shape, lambda *_: (0,) * nd, pipeline_mode=pl.Buffered(1))


def _ada_kernel(c_ref, w_ref, b_ref, o_ref):
    c = c_ref[...]
    s = (c * _sigmoid(c)).astype(BF16)
    o_ref[...] = jnp.dot(s, w_ref[...].astype(BF16), preferred_element_type=F32) + b_ref[...]


def _ada_mod(cond, w_ada, b_ada):
    r, d = cond.shape
    n = w_ada.shape[1]
    tn = 512
    return pl.pallas_call(
        _ada_kernel,
        grid=(n // tn,),
        in_specs=[pl.BlockSpec((r, d), lambda j: (0, 0)),
                  pl.BlockSpec((d, tn), lambda j: (0, j)),
                  pl.BlockSpec((1, tn), lambda j: (0, j))],
        out_specs=pl.BlockSpec((r, tn), lambda j: (0, j)),
        out_shape=jax.ShapeDtypeStruct((r, n), F32),
        compiler_params=_cparams(("arbitrary",)),
        name="ada_mod",
    )(cond, w_ada, b_ada.reshape(1, n))


def _inproj_kernel(x_ref, mod_ref, w_ref, *rest, n_small, n_cast):
    cast_in = rest[:n_cast]
    small_ref, big_ref = rest[n_cast:n_cast + 2]
    cast_out = rest[n_cast + 2:2 * n_cast + 2]
    h_scr = rest[-1]
    for src, dst in zip(cast_in, cast_out):
        dst[...] = src[...].astype(BF16)
    j = pl.program_id(1)

    @pl.when(j == 0)
    def _():
        m = mod_ref[0]
        h_scr[...] = (x_ref[...] * (1.0 + m[1:2]) + m[0:1]).astype(BF16)

    acc = jnp.dot(h_scr[...], w_ref[...], preferred_element_type=F32)

    @pl.when(j < n_small)
    def _():
        small_ref[...] = acc

    @pl.when(j >= n_small)
    def _():
        big_ref[...] = acc.astype(BF16)


def _inproj(x, mods, mod_row, w_r, tm, cast=()):
    n, d = x.shape
    tn = 1024
    n_small = SMALL_W // tn
    n_cols = w_r.shape[1] // tn
    steps = (n // tm) * n_cols
    side = lambda a: pl.BlockSpec((a.shape[0] // steps, a.shape[1]), lambda i, j: (i * n_cols + j, 0))
    outs = pl.pallas_call(
        functools.partial(_inproj_kernel, n_small=n_small, n_cast=len(cast)),
        grid=(n // tm, n_cols),
        in_specs=[pl.BlockSpec((tm, d), lambda i, j: (i, 0)),
                  pl.BlockSpec((1, N_MOD, d), lambda i, j: (mod_row(i * tm), 0, 0)),
                  pl.BlockSpec((d, tn), lambda i, j: (0, j))] + [side(a) for a in cast],
        out_specs=[pl.BlockSpec((tm, tn), lambda i, j: (i, jnp.minimum(j, n_small - 1))),
                   pl.BlockSpec((tm, tn), lambda i, j: (i, jnp.maximum(j - n_small, 0)))]
        + [side(a) for a in cast],
        out_shape=[jax.ShapeDtypeStruct((n, SMALL_W), F32),
                   jax.ShapeDtypeStruct((n, BIG_W), BF16)]
        + [jax.ShapeDtypeStruct(a.shape, BF16) for a in cast],
        scratch_shapes=[pltpu.VMEM((tm, d), BF16)],
        compiler_params=_cparams(("arbitrary", "arbitrary")),
        name="inproj",
    )(x, mods, w_r, *cast)
    return outs[0], outs[1], tuple(outs[2:])


def _qprep_kernel(*refs, rope):
    if rope:
        small_ref, qg_ref, kvg_ref, wa_ref, wb_ref, c_ref, s_ref, q_ref, ckv_ref, kr_ref = refs
    else:
        small_ref, qg_ref, kvg_ref, wa_ref, q_ref, ckv_ref, kr_ref = refs
    qc = small_ref[:, 0:Q_LORA]
    kvc = small_ref[:, Q_LORA:Q_LORA + KV_LORA]
    kr = small_ref[:, Q_LORA + KV_LORA:Q_LORA + KV_LORA + LANES]
    qn = (qc * lax.rsqrt(jnp.mean(qc * qc, axis=-1, keepdims=True) + EPS) * qg_ref[...]).astype(BF16)
    ckv_ref[...] = kvc * lax.rsqrt(jnp.mean(kvc * kvc, axis=-1, keepdims=True) + EPS) * kvg_ref[...]
    qa = jnp.dot(qn, wa_ref[...], preferred_element_type=F32)
    if rope:
        krs = small_ref[:, Q_LORA + KV_LORA + LANES:SMALL_W]
        qb = jnp.dot(qn, wb_ref[...], preferred_element_type=F32)
        cs = c_ref[...]
        sn = s_ref[...]
        kr_ref[...] = kr * cs + krs * sn
        for h in range(MLA_HEADS):
            lo = h * HEAD_PAD
            q_ref[:, lo:lo + LANES] = qa[:, lo:lo + LANES].astype(BF16)
            q_ref[:, lo + LANES:lo + HEAD_PAD] = (
                qa[:, lo + LANES:lo + HEAD_PAD] * cs + qb[:, h * LANES:(h + 1) * LANES] * sn).astype(BF16)
    else:
        kr_ref[...] = kr
        q_ref[...] = qa.astype(BF16)


def _qprep(small, qg, kvg, wa, wb, cos128, sin128, seq, rope):
    n = small.shape[0]
    tm = min(512, seq)
    nblk = seq // tm
    in_specs = [pl.BlockSpec((tm, SMALL_W), lambda i: (i, 0)),
                pl.BlockSpec((1, Q_LORA), lambda i: (0, 0)),
                pl.BlockSpec((1, KV_LORA), lambda i: (0, 0)),
                pl.BlockSpec(wa.shape, lambda i: (0, 0))]
    args = [small, qg, kvg, wa]
    if rope:
        in_specs += [pl.BlockSpec(wb.shape, lambda i: (0, 0)),
                     pl.BlockSpec((tm, LANES), lambda i: (i % nblk, 0)),
                     pl.BlockSpec((tm, LANES), lambda i: (i % nblk, 0))]
        args += [wb, cos128, sin128]
    return pl.pallas_call(
        functools.partial(_qprep_kernel, rope=rope),
        grid=(n // tm,),
        in_specs=in_specs,
        out_specs=[pl.BlockSpec((tm, MLA_HEADS * HEAD_PAD), lambda i: (i, 0)),
                   pl.BlockSpec((tm, KV_LORA), lambda i: (i, 0)),
                   pl.BlockSpec((tm, LANES), lambda i: (i, 0))],
        out_shape=[jax.ShapeDtypeStruct((n, MLA_HEADS * HEAD_PAD), BF16),
                   jax.ShapeDtypeStruct((n, KV_LORA), F32),
                   jax.ShapeDtypeStruct((n, LANES), F32)],
        compiler_params=_cparams(("arbitrary",)),
        name="qprep",
    )(*args)


def _kvup_kernel(kv_ref, kr_ref, wk_ref, wv_ref, k_ref, v_ref):
    kv = kv_ref[0].astype(BF16)
    kn = jnp.dot(kv, wk_ref[...], preferred_element_type=F32)
    vv = jnp.dot(kv, wv_ref[...], preferred_element_type=F32)
    krp = kr_ref[0].astype(BF16)
    lane = lax.broadcasted_iota(I32, (kv.shape[0], LANES), 1)
    ones_col = jnp.where(lane == 0, 1.0, 0.0).astype(BF16)
    for h in range(MLA_HEADS):
        k_ref[0, h, :, 0:LANES] = kn[:, h * QK_NOPE:(h + 1) * QK_NOPE].astype(BF16)
        k_ref[0, h, :, LANES:HEAD_PAD] = krp
        v_ref[0, h, :, 0:V_HEAD] = vv[:, h * V_HEAD:(h + 1) * V_HEAD].astype(BF16)
        v_ref[0, h, :, V_HEAD:V_HEAD + LANES] = ones_col


def _kvup(kv_all, kr_all, wk, wv):
    b, lk, _ = kv_all.shape
    tm = 512 if lk % 512 == 0 else 256
    return pl.pallas_call(
        _kvup_kernel,
        grid=(b, lk // tm),
        in_specs=[pl.BlockSpec((1, tm, KV_LORA), lambda bi, i: (bi, i, 0)),
                  pl.BlockSpec((1, tm, LANES), lambda bi, i: (bi, i, 0)),
                  pl.BlockSpec(wk.shape, lambda bi, i: (0, 0)),
                  pl.BlockSpec(wv.shape, lambda bi, i: (0, 0))],
        out_specs=[pl.BlockSpec((1, MLA_HEADS, tm, HEAD_PAD), lambda bi, i: (bi, 0, i, 0)),
                   pl.BlockSpec((1, MLA_HEADS, tm, V_HEAD + LANES), lambda bi, i: (bi, 0, i, 0))],
        out_shape=[jax.ShapeDtypeStruct((b, MLA_HEADS, lk, HEAD_PAD), BF16),
                   jax.ShapeDtypeStruct((b, MLA_HEADS, lk, V_HEAD + LANES), BF16)],
        compiler_params=_cparams(("arbitrary", "arbitrary")),
        name="kvup",
    )(kv_all, kr_all, wk, wv)


def _attn_kernel(q_ref, k_ref, v_ref, o_ref, *, rows):
    for h in range(k_ref.shape[1]):
        k = k_ref[0, h]
        v = v_ref[0, h]
        for r0 in range(0, q_ref.shape[1], rows):
            q = q_ref[0, r0:r0 + rows, h * HEAD_PAD:(h + 1) * HEAD_PAD]
            s = lax.dot_general(q, k, (((1,), (1,)), ((), ())), preferred_element_type=F32)
            p = jnp.exp2(s - jnp.max(s, axis=-1, keepdims=True)).astype(BF16)
            o = jnp.dot(p, v, preferred_element_type=F32)
            o_ref[0, r0:r0 + rows, h * V_HEAD:(h + 1) * V_HEAD] = (
                o[:, 0:V_HEAD] / o[:, V_HEAD:V_HEAD + 1]).astype(BF16)


def _attention(q, k, v):
    b, l, _ = q.shape
    lk = k.shape[2]
    rows = 256
    chains = 16
    tq = min(l, chains * rows)
    hb = min(MLA_HEADS, max(1, chains * rows // tq))
    return pl.pallas_call(
        functools.partial(_attn_kernel, rows=rows),
        grid=(b, MLA_HEADS // hb, l // tq),
        in_specs=[pl.BlockSpec((1, tq, hb * HEAD_PAD), lambda bi, h, i: (bi, i, h)),
                  pl.BlockSpec((1, hb, lk, HEAD_PAD), lambda bi, h, i: (bi, h, 0, 0)),
                  pl.BlockSpec((1, hb, lk, V_HEAD + LANES), lambda bi, h, i: (bi, h, 0, 0))],
        out_specs=pl.BlockSpec((1, tq, hb * V_HEAD), lambda bi, h, i: (bi, i, h)),
        out_shape=jax.ShapeDtypeStruct((b, l, MLA_HEADS * V_HEAD), BF16),
        compiler_params=_cparams(("arbitrary", "arbitrary", "arbitrary")),
        name="attention",
    )(q, k, v)


def _hypre_kernel(x0_ref, x1_ref, v_ref, w_ref, b_ref, x0o_ref, z_ref, zn_ref, *, nb):
    seq = x0_ref.shape[1]
    tb = seq // nb
    row = lax.broadcasted_iota(I32, (seq, 1), 0)

    def conv(u_ref, g):
        u = u_ref[0].astype(F32)
        up = jnp.where(row == 0, 0.0, pltpu.roll(u, 1, axis=0))
        un = jnp.where(row == seq - 1, 0.0, pltpu.roll(u, seq - 1, axis=0))
        w = w_ref[g]
        return up * w[0:1] + u * w[1:2] + un * w[2:3] + b_ref[g]

    x0o_ref[0] = conv(x0_ref, 0).astype(BF16)
    z = conv(x1_ref, 1) * conv(v_ref, 2)
    z_ref[0] = z.astype(BF16)
    zs = z * (1 - 2 * (row & 1)).astype(F32)
    for j in range(nb):
        zn_ref[0, j:j + 1, :] = jnp.sum(zs[j * tb:(j + 1) * tb], axis=0, keepdims=True)


def _hypre(big3, w3, b3, nb):
    b, l, _ = big3.shape
    cb = 256
    ncb = HY_WIDTH // cb
    off = 2 * D_MODEL // cb
    return pl.pallas_call(
        functools.partial(_hypre_kernel, nb=nb),
        grid=(b, ncb),
        in_specs=[pl.BlockSpec((1, l, cb), lambda bi, j: (bi, 0, off + j)),
                  pl.BlockSpec((1, l, cb), lambda bi, j: (bi, 0, off + j + ncb)),
                  pl.BlockSpec((1, l, cb), lambda bi, j: (bi, 0, off + j + 2 * ncb)),
                  pl.BlockSpec((3, 3, cb), lambda bi, j: (0, 0, j)),
                  pl.BlockSpec((3, 1, cb), lambda bi, j: (0, 0, j))],
        out_specs=[pl.BlockSpec((1, l, cb), lambda bi, j: (bi, 0, j)),
                   pl.BlockSpec((1, l, cb), lambda bi, j: (bi, 0, j)),
                   pl.BlockSpec((1, nb, cb), lambda bi, j: (bi, 0, j))],
        out_shape=[jax.ShapeDtypeStruct((b, l, HY_WIDTH), BF16),
                   jax.ShapeDtypeStruct((b, l, HY_WIDTH), BF16),
                   jax.ShapeDtypeStruct((b, nb, HY_WIDTH), F32)],
        compiler_params=_cparams(("arbitrary", "arbitrary")),
        name="hyena_pre",
    )(big3, big3, big3, w3, b3)


def _filter_kernel(feat_ref, w1_ref, b1_ref, f1_ref, w2_ref, b2_ref, f2_ref,
                   w3f_ref, w3b_ref, b3f_ref, b3b_ref, dl_ref, hf_ref, hb_ref, ext_ref, hd_scr, *, nb):
    seq = feat_ref.shape[0]
    tb = seq // nb

    @pl.when(pl.program_id(0) == 0)
    def _():
        h1 = jnp.sin(f1_ref[...] * (jnp.dot(feat_ref[...], w1_ref[...], precision=HIGHEST,
                                            preferred_element_type=F32) + b1_ref[...]))
        hd_scr[...] = jnp.sin(f2_ref[...] * (jnp.dot(h1, w2_ref[...], precision=HIGHEST,
                                                     preferred_element_type=F32) + b2_ref[...]))

    hd = hd_scr[...]
    row = lax.broadcasted_iota(I32, (seq, 1), 0)
    t = row.astype(F32) / float(max(seq - 1, 1))
    window = jnp.exp(-t * dl_ref[...])
    hf = (jnp.dot(hd, w3f_ref[...], precision=HIGHEST, preferred_element_type=F32) + b3f_ref[...]) * window
    hb = (jnp.dot(hd, w3b_ref[...], precision=HIGHEST, preferred_element_type=F32) + b3b_ref[...]) * window
    hb = jnp.where(row == 0, 0.0, hb)
    den = jnp.sum(jnp.abs(hf) + jnp.abs(hb), axis=0, keepdims=True) + EPS
    hf = hf / den
    hb = hb / den
    hf_ref[...] = hf.astype(BF16)
    hb_ref[...] = hb.astype(BF16)
    sign = (1 - 2 * (row & 1)).astype(F32)
    for j in range(nb):
        sl = slice(j * tb, (j + 1) * tb)
        ext_ref[0, j:j + 1, :] = hf[j * tb:j * tb + 1].astype(BF16).astype(F32)
        ext_ref[1, j:j + 1, :] = hb[j * tb:j * tb + 1].astype(BF16).astype(F32)
        ext_ref[2, j:j + 1, :] = jnp.sum(hf[sl] * sign[sl], axis=0, keepdims=True)
        ext_ref[3, j:j + 1, :] = jnp.sum(hb[sl] * sign[sl], axis=0, keepdims=True)


def _filter_td(seq, p, nb):
    pos = np.arange(seq, dtype=np.float64)
    t = pos / max(seq - 1, 1)
    bands = np.linspace(1e-4, HY_BANDS - 1, HY_BANDS)
    w = 2.0 * math.pi * pos / seq
    feats = np.concatenate([t[:, None], np.cos(w[:, None] * bands), -np.sin(w[:, None] * bands)], axis=-1)
    feats = jnp.asarray(np.pad(feats, ((0, 0), (0, LANES - HY_EMB))), F32)
    w1 = jnp.pad(p['hy_filt_w1'], ((0, LANES - HY_EMB), (0, 0)))
    deltas = jnp.asarray(np.abs(np.linspace(math.log(HY_TARGET) / HY_SLOW_DECAY, math.log(HY_TARGET) / HY_FAST_DECAY,
                                            HY_WIDTH)).reshape(1, HY_WIDTH), F32)
    cb = 256
    ncb = HY_WIDTH // cb
    row = lambda v: v.reshape(1, -1)
    c2 = lambda j: (0, 0)
    return pl.pallas_call(
        functools.partial(_filter_kernel, nb=nb),
        grid=(ncb,),
        in_specs=[pl.BlockSpec((seq, LANES), c2),
                  pl.BlockSpec((LANES, HY_HIDDEN), c2), pl.BlockSpec((1, HY_HIDDEN), c2),
                  pl.BlockSpec((1, HY_HIDDEN), c2),
                  pl.BlockSpec((HY_HIDDEN, HY_HIDDEN), c2), pl.BlockSpec((1, HY_HIDDEN), c2),
                  pl.BlockSpec((1, HY_HIDDEN), c2),
                  pl.BlockSpec((HY_HIDDEN, cb), lambda j: (0, j)),
                  pl.BlockSpec((HY_HIDDEN, cb), lambda j: (0, j + ncb)),
                  pl.BlockSpec((1, cb), lambda j: (0, j)),
                  pl.BlockSpec((1, cb), lambda j: (0, j + ncb)),
                  pl.BlockSpec((1, cb), lambda j: (0, j))],
        out_specs=[pl.BlockSpec((seq, cb), lambda j: (0, j)),
                   pl.BlockSpec((seq, cb), lambda j: (0, j)),
                   pl.BlockSpec((4, nb, cb), lambda j: (0, 0, j))],
        out_shape=[jax.ShapeDtypeStruct((seq, HY_WIDTH), BF16),
                   jax.ShapeDtypeStruct((seq, HY_WIDTH), BF16),
                   jax.ShapeDtypeStruct((4, nb, HY_WIDTH), F32)],
        scratch_shapes=[pltpu.VMEM((seq, HY_HIDDEN), F32)],
        compiler_params=_cparams(("arbitrary",)),
        name="hyena_filter",
    )(feats, w1, row(p['hy_filt_b1']), row(p['hy_filt_freq1']),
      p['hy_filt_w2'], row(p['hy_filt_b2']), row(p['hy_filt_freq2']),
      p['hy_filt_w3'], p['hy_filt_w3'], row(p['hy_filt_b3']), row(p['hy_filt_b3']), deltas)


def _kspec_kernel(c_ref, s_ref, hf_ref, hb_ref, ext_ref, kc_ref, ks_ref, kn_ref, *, nb):
    tb = c_ref.shape[0]
    f = lax.broadcasted_iota(I32, (tb, 1), 0)
    wgt = jnp.where(f == 0, 1.0, 2.0) / float(2 * tb)
    sig = (1 - 2 * (f & 1)).astype(F32)
    cm = c_ref[...]
    sm = s_ref[...]
    blk = lambda ref, j: ref[j * tb:(j + 1) * tb, :]
    chf = [jnp.dot(cm, blk(hf_ref, j), preferred_element_type=F32) for j in range(nb)]
    shf = [jnp.dot(sm, blk(hf_ref, j), preferred_element_type=F32) for j in range(nb)]
    chb = [jnp.dot(cm, blk(hb_ref, j), preferred_element_type=F32) for j in range(nb)]
    shb = [jnp.dot(sm, blk(hb_ref, j), preferred_element_type=F32) for j in range(nb)]
    first_f = lambda j: ext_ref[0, j:j + 1, :]
    first_b = lambda j: ext_ref[1, j:j + 1, :]
    alt_f = lambda j: ext_ref[2, j:j + 1, :]
    alt_b = lambda j: ext_ref[3, j:j + 1, :]
    for d in range(-(nb - 1), nb):
        k = d + nb - 1
        if d == 0:
            kc, ks, kn = chf[0] + chb[0], shf[0] - shb[0], alt_f(0) + alt_b(0)
        elif d > 0:
            kc = chf[d] + sig * (chf[d - 1] - first_f(d - 1))
            ks = shf[d] + sig * shf[d - 1]
            kn = alt_f(d) + alt_f(d - 1) - first_f(d - 1)
        else:
            e = -d
            kc = chb[e] + sig * (chb[e - 1] - first_b(e - 1))
            ks = -(shb[e] + sig * shb[e - 1])
            kn = alt_b(e) + alt_b(e - 1) - first_b(e - 1)
        kc_ref[k] = kc * wgt
        ks_ref[k] = ks * wgt
        kn_ref[k:k + 1, :] = kn * (1.0 / float(2 * tb))


def _kspec(cmat, smat, hf, hb, ext, nb):
    tb = cmat.shape[0]
    seq = hf.shape[0]
    nd = 2 * nb - 1
    cb = 256 if nb > 1 else 512
    return pl.pallas_call(
        functools.partial(_kspec_kernel, nb=nb),
        grid=(HY_WIDTH // cb,),
        in_specs=[pl.BlockSpec((tb, tb), lambda j: (0, 0)),
                  pl.BlockSpec((tb, tb), lambda j: (0, 0)),
                  pl.BlockSpec((seq, cb), lambda j: (0, j)),
                  pl.BlockSpec((seq, cb), lambda j: (0, j)),
                  pl.BlockSpec((4, nb, cb), lambda j: (0, 0, j))],
        out_specs=[pl.BlockSpec((nd, tb, cb), lambda j: (0, 0, j)),
                   pl.BlockSpec((nd, tb, cb), lambda j: (0, 0, j)),
                   pl.BlockSpec((nd, cb), lambda j: (0, j))],
        out_shape=[jax.ShapeDtypeStruct((nd, tb, HY_WIDTH), F32),
                   jax.ShapeDtypeStruct((nd, tb, HY_WIDTH), F32),
                   jax.ShapeDtypeStruct((nd, HY_WIDTH), F32)],
        compiler_params=_cparams(("arbitrary",)),
        name="hyena_kspec",
    )(cmat, smat, hf, hb, ext)


def _fwd_kernel(c_ref, s_ref, z_ref, kc_ref, ks_ref, pc_ref, ps_ref, *, nb):
    tb = c_ref.shape[0]
    cm = c_ref[...]
    sm = s_ref[...]
    zc = [jnp.dot(cm, z_ref[0, j * tb:(j + 1) * tb, :], preferred_element_type=F32) for j in range(nb)]
    zs = [jnp.dot(sm, z_ref[0, j * tb:(j + 1) * tb, :], preferred_element_type=F32) for j in range(nb)]
    for i in range(nb):
        pc = None
        ps = None
        for j in range(nb):
            kc = kc_ref[i - j + nb - 1]
            ks = ks_ref[i - j + nb - 1]
            tc = zc[j] * kc - zs[j] * ks
            ts = zc[j] * ks + zs[j] * kc
            pc = tc if pc is None else pc + tc
            ps = ts if ps is None else ps + ts
        pc_ref[0, i * tb:(i + 1) * tb, :] = pc.astype(BF16)
        ps_ref[0, i * tb:(i + 1) * tb, :] = ps.astype(BF16)


def _hy_fwd(cmat, smat, z, kc, ks, nb):
    b, seq, _ = z.shape
    tb = seq // nb
    nd = 2 * nb - 1
    cb = 256 if nb > 1 else HY_WIDTH
    return pl.pallas_call(
        functools.partial(_fwd_kernel, nb=nb),
        grid=(HY_WIDTH // cb, b),
        in_specs=[pl.BlockSpec((tb, tb), lambda j, bi: (0, 0)),
                  pl.BlockSpec((tb, tb), lambda j, bi: (0, 0)),
                  pl.BlockSpec((1, seq, cb), lambda j, bi: (bi, 0, j)),
                  pl.BlockSpec((nd, tb, cb), lambda j, bi: (0, 0, j)),
                  pl.BlockSpec((nd, tb, cb), lambda j, bi: (0, 0, j))],
        out_specs=[pl.BlockSpec((1, seq, cb), lambda j, bi: (bi, 0, j)),
                   pl.BlockSpec((1, seq, cb), lambda j, bi: (bi, 0, j))],
        out_shape=[jax.ShapeDtypeStruct((b, seq, HY_WIDTH), BF16),
                   jax.ShapeDtypeStruct((b, seq, HY_WIDTH), BF16)],
        compiler_params=_cparams(("arbitrary", "arbitrary")),
        name="hyena_fwd",
    )(cmat, smat, z, kc, ks)


def _inv_kernel(c_ref, s_ref, pc_ref, ps_ref, z_ref, x0_ref, zn_ref, kn_ref, skip_ref, o_ref, *, nb):
    tb = c_ref.shape[0]
    cm = c_ref[...]
    sm = s_ref[...]
    t = lax.broadcasted_iota(I32, (tb, 1), 0)
    sign = (1 - 2 * (t & 1)).astype(F32)
    ys = [jnp.dot(cm, pc_ref[0, i * tb:(i + 1) * tb, :], preferred_element_type=F32)
          + jnp.dot(sm, ps_ref[0, i * tb:(i + 1) * tb, :], preferred_element_type=F32) for i in range(nb)]
    for i in range(nb):
        rs = slice(i * tb, (i + 1) * tb)
        y = ys[i]
        nyq = None
        for j in range(nb):
            k = i - j + nb - 1
            term = zn_ref[0, j:j + 1, :] * kn_ref[k:k + 1, :]
            nyq = term if nyq is None else nyq + term
        y = y + sign * nyq + z_ref[0, rs, :].astype(F32) * skip_ref[...]
        o_ref[0, rs, :] = (x0_ref[0, rs, :].astype(F32) * y).astype(BF16)


def _hy_inv(cmat, smat, pc, ps, z, x0, zn, kn, skip, nb):
    b, seq, _ = z.shape
    tb = seq // nb
    nd = 2 * nb - 1
    cb = 256 if nb > 1 else HY_WIDTH
    blk = pl.BlockSpec((1, seq, cb), lambda j, bi: (bi, 0, j))
    return pl.pallas_call(
        functools.partial(_inv_kernel, nb=nb),
        grid=(HY_WIDTH // cb, b),
        in_specs=[pl.BlockSpec((tb, tb), lambda j, bi: (0, 0)),
                  pl.BlockSpec((tb, tb), lambda j, bi: (0, 0)),
                  blk, blk, blk, blk,
                  pl.BlockSpec((1, nb, cb), lambda j, bi: (bi, 0, j)),
                  pl.BlockSpec((nd, cb), lambda j, bi: (0, j)),
                  pl.BlockSpec((1, cb), lambda j, bi: (0, j))],
        out_specs=blk,
        out_shape=jax.ShapeDtypeStruct((b, seq, HY_WIDTH), BF16),
        compiler_params=_cparams(("arbitrary", "arbitrary")),
        name="hyena_inv",
    )(cmat, smat, pc, ps, z, x0, zn, kn, skip)


def _layer_norm(v, g, b):
    mu = jnp.mean(v, axis=-1, keepdims=True)
    vc = v - mu
    var = jnp.mean(vc * vc, axis=-1, keepdims=True)
    return vc * lax.rsqrt(var + EPS) * g + b


def _merge_kernel(at_ref, hy_ref, ga_ref, gb_ref, woa_ref, woh_ref, o_ref):
    a = jnp.dot(at_ref[...], woa_ref[...], preferred_element_type=F32)
    hh = jnp.dot(hy_ref[...], woh_ref[...], preferred_element_type=F32)
    merged = _sigmoid(ga_ref[...].astype(F32)) * a + _sigmoid(gb_ref[...].astype(F32)) * hh
    o_ref[...] = merged.astype(BF16)


def _merge(attn, hy, big, woa, woh):
    n = attn.shape[0]
    d = woa.shape[1]
    tm = 512
    return pl.pallas_call(
        _merge_kernel,
        grid=(n // tm,),
        in_specs=[pl.BlockSpec((tm, attn.shape[1]), lambda i: (i, 0)),
                  pl.BlockSpec((tm, hy.shape[1]), lambda i: (i, 0)),
                  pl.BlockSpec((tm, d), lambda i: (i, 0)),
                  pl.BlockSpec((tm, d), lambda i: (i, 1)),
                  _const_spec(woa.shape), _const_spec(woh.shape)],
        out_specs=pl.BlockSpec((tm, d), lambda i: (i, 0)),
        out_shape=jax.ShapeDtypeStruct((n, d), BF16),
        compiler_params=_cparams(("arbitrary",)),
        name="merge",
    )(attn, hy, big, big, woa, woh)


def _ln1_kernel(x_ref, mg_ref, mod_ref, wout_ref, g_ref, b_ref, wr_ref, x1_ref, h2_ref, lg_ref, *, n_sub):
    m = mod_ref[0]
    rows = x_ref.shape[0] // n_sub
    mixes = [jnp.dot(mg_ref[t * rows:(t + 1) * rows, :], wout_ref[...], preferred_element_type=F32)
             for t in range(n_sub)]
    for t in range(n_sub):
        rs = slice(t * rows, (t + 1) * rows)
        x1 = _layer_norm(ALPHA * x_ref[rs, :] + m[2:3] * mixes[t], g_ref[...], b_ref[...])
        x1_ref[rs, :] = x1
        h2 = x1 * (1.0 + m[4:5]) + m[3:4]
        h2_ref[rs, :] = h2
        hi = h2.astype(BF16)
        lo = (h2 - hi.astype(F32)).astype(BF16)
        lg = (jnp.dot(hi, wr_ref[...], preferred_element_type=F32)
              + jnp.dot(lo, wr_ref[...], preferred_element_type=F32))
        lg_ref[rs, :] = lg + pltpu.roll(lg, LANES - N_EXPERTS, axis=1)


def _ln1(x, merged, mods, mod_row, wout, g, b, wr_pack):
    n, d = x.shape
    tm = 512
    return pl.pallas_call(
        functools.partial(_ln1_kernel, n_sub=2),
        grid=(n // tm,),
        in_specs=[pl.BlockSpec((tm, d), lambda i: (i, 0)),
                  pl.BlockSpec((tm, d), lambda i: (i, 0)),
                  pl.BlockSpec((1, N_MOD, d), lambda i: (mod_row(i * tm), 0, 0)),
                  _const_spec(wout.shape), _const_spec((1, d)), _const_spec((1, d)),
                  _const_spec(wr_pack.shape)],
        out_specs=[pl.BlockSpec((tm, d), lambda i: (i, 0)),
                   pl.BlockSpec((tm, d), lambda i: (i, 0)),
                   pl.BlockSpec((tm, LANES), lambda i: (i, 0))],
        out_shape=[jax.ShapeDtypeStruct((n, d), F32),
                   jax.ShapeDtypeStruct((n, d), F32),
                   jax.ShapeDtypeStruct((n, LANES), F32)],
        compiler_params=_cparams(("arbitrary",)),
        name="ln1_router",
    )(x, merged, mods, wout, g, b, wr_pack)


def _split3(t):
    t1 = t.astype(BF16)
    r = t - t1.astype(F32)
    t2 = r.astype(BF16)
    t3 = (r - t2.astype(F32)).astype(BF16)
    return t1, t2, t3


def _dot_exact(sel01, table):
    t1, t2, t3 = _split3(table)
    out = jnp.dot(sel01, t1, preferred_element_type=F32)
    out = out + jnp.dot(sel01, t2, preferred_element_type=F32)
    return out + jnp.dot(sel01, t3, preferred_element_type=F32)


def _route_kernel(lg_ref, idx_ref, q_ref, gate_ref, start_ref, aff_scr, sel_scr, pos_scr, *, cap):
    ne, nc, _ = lg_ref.shape
    lg = lg_ref[...]
    mx = jnp.max(lg, axis=0, keepdims=True)
    ex = jnp.exp(lg - mx)
    aff = ex / jnp.sum(ex, axis=0, keepdims=True)
    aff_scr[...] = aff

    def tbody(k, thr):
        cand = thr | lax.shift_left(jnp.int32(1), 30 - k)
        ge = jnp.where(aff >= lax.bitcast_convert_type(cand, F32), 1.0, 0.0)
        cnt = jnp.sum(jnp.sum(ge, axis=1, keepdims=True), axis=2, keepdims=True)
        return jnp.where(cnt >= float(cap), cand, thr)

    thr = lax.fori_loop(0, 31, tbody, jnp.zeros((ne, 1, 1), I32))
    lo = lax.bitcast_convert_type(thr, F32)
    above = aff >= lax.bitcast_convert_type(thr + 1, F32)
    sel_scr[...] = jnp.where(above, 1.0, 0.0)
    pos_scr[...] = jnp.where((aff >= lo) & jnp.logical_not(above), aff, -1.0)

    r_i = lax.broadcasted_iota(I32, (LANES, LANES), 0)
    c_i = lax.broadcasted_iota(I32, (LANES, LANES), 1)
    upper_incl = jnp.where(r_i <= c_i, 1.0, 0.0).astype(BF16)
    ones_sq = jnp.ones((LANES, LANES), BF16)
    rr_i = lax.broadcasted_iota(I32, (nc, nc), 0)
    cc_i = lax.broadcasted_iota(I32, (nc, nc), 1)
    lower_strict = jnp.where(cc_i < rr_i, 1.0, 0.0).astype(BF16)
    upper_rows = jnp.where(rr_i <= cc_i, 1.0, 0.0).astype(BF16)

    def prefix(x01):
        xb = x01.astype(BF16)
        p1 = jnp.dot(xb, upper_incl, preferred_element_type=F32)
        totb = jnp.dot(xb, ones_sq, preferred_element_type=F32)
        offs = jnp.dot(lower_strict, totb.astype(BF16), preferred_element_type=F32)
        return p1, offs

    tok = (lax.broadcasted_iota(I32, (nc, LANES), 0) * LANES
           + lax.broadcasted_iota(I32, (nc, LANES), 1)).astype(F32)

    def reduce2(fn, v):
        return fn(fn(v, axis=0, keepdims=True), axis=1, keepdims=True)

    def pass1(e, start):
        above_e = sel_scr[e]
        need = cap - jnp.sum(above_e).astype(I32)

        def pick(_, carry):
            vals, picked = carry
            first = reduce2(jnp.min, jnp.where(vals == reduce2(jnp.max, vals), tok, float(nc * LANES)))
            hit = tok == first
            return jnp.where(hit, -1.0, vals), picked + jnp.where(hit, 1.0, 0.0)

        _, picked = lax.fori_loop(0, need, pick, (pos_scr[e], jnp.zeros((nc, LANES), F32)))
        sel = above_e + picked
        sel_scr[e] = sel
        s1, soffs = prefix(sel)
        return start + (s1 + soffs - sel)

    start = lax.fori_loop(0, ne, pass1, jnp.zeros((nc, LANES), F32))
    start_ref[...] = start.astype(I32)

    p_col = lax.broadcasted_iota(I32, (cap, 1), 0).astype(F32)
    lane_row = lax.broadcasted_iota(I32, (1, LANES), 1).astype(F32)
    chunk_row = lax.broadcasted_iota(I32, (1, nc), 1).astype(F32)
    ones8 = jnp.ones((8, LANES), BF16)

    def row_sums(m):
        ones = jnp.ones((8, m.shape[1]), BF16)
        nt = lambda part: lax.dot_general(ones, part, (((1,), (1,)), ((), ())), preferred_element_type=F32)
        m1, m2, m3 = _split3(m)
        return (nt(m1) + nt(m2) + nt(m3))[0:1]

    def pass2(e, before):
        sel = sel_scr[e]
        selb = sel.astype(BF16)
        p1, offs = prefix(sel)
        tot_row = lax.dot_general(ones8, selb, (((1,), (1,)), ((), ())), preferred_element_type=F32)
        cum_row = jnp.dot(tot_row.astype(BF16), upper_rows, preferred_element_type=F32)
        cum1 = cum_row[0:1]
        prev1 = cum1 - tot_row[0:1]
        in_chunk = (prev1 <= p_col) & (p_col < cum1)
        ohc = jnp.where(in_chunk, 1.0, 0.0).astype(BF16)
        keyg = jnp.dot(ohc, (p1 * sel).astype(BF16), preferred_element_type=F32)
        offg = _dot_exact(ohc, offs)
        ohl = keyg == (p_col - offg + 1.0)
        l_p = row_sums(jnp.where(ohl, lane_row, 0.0))
        c_p = row_sums(jnp.where(in_chunk, chunk_row, 0.0))
        idx_ref[e] = (c_p * float(LANES) + l_p).astype(I32)
        qtab = _dot_exact(ohc, start) + jnp.dot(ohc, before.astype(BF16), preferred_element_type=F32)
        q_ref[e] = row_sums(jnp.where(ohl, qtab, 0.0)).astype(I32)
        afg = _dot_exact(ohc, aff_scr[e])
        gate_ref[e] = row_sums(jnp.where(ohl, afg, 0.0))
        return before + sel

    lax.fori_loop(0, ne, pass2, jnp.zeros((nc, LANES), F32))


def _route(lg3, cap):
    ne, nc, _ = lg3.shape
    full = lambda s: pl.BlockSpec(s, lambda: (0,) * len(s))
    return pl.pallas_call(
        functools.partial(_route_kernel, cap=cap),
        in_specs=[full(lg3.shape)],
        out_specs=[full((ne, 1, cap)), full((ne, 1, cap)), full((ne, 1, cap)), full((nc, LANES))],
        out_shape=[jax.ShapeDtypeStruct((ne, 1, cap), I32),
                   jax.ShapeDtypeStruct((ne, 1, cap), I32),
                   jax.ShapeDtypeStruct((ne, 1, cap), F32),
                   jax.ShapeDtypeStruct((nc, LANES), I32)],
        scratch_shapes=[pltpu.VMEM((ne, nc, LANES), F32),
                        pltpu.VMEM((ne, nc, LANES), F32),
                        pltpu.VMEM((ne, nc, LANES), F32)],
        compiler_params=pltpu.CompilerParams(vmem_limit_bytes=VMEM_LIMIT),
        name="route",
    )(lg3)


def _ffn_kernel(idx_ref, q_ref, gate_ref, wg_ref, wu_ref, wd_ref, h_hbm, y_hbm,
                xb, xa, yb, ya, gsem, ssem, *, tm, npair, total):
    step = pl.program_id(0) * npair + pl.program_id(1)
    t0 = 2 * step
    t1 = t0 + 1

    def gather_row(base, r, buf, s):
        return pltpu.make_async_copy(h_hbm.at[pl.ds(idx_ref[base + r], 1), :], buf.at[pl.ds(r, 1), :], gsem.at[s])

    def scatter_row(base, r, buf, s):
        return pltpu.make_async_copy(buf.at[pl.ds(r, 1), :], y_hbm.at[pl.ds(q_ref[base + r], 1), :], ssem.at[s])

    def wait_gather(buf, s):
        pltpu.make_async_copy(h_hbm.at[pl.ds(0, tm), :], buf, gsem.at[s]).wait()

    def wait_scatter(buf, s):
        pltpu.make_async_copy(buf, y_hbm.at[pl.ds(0, tm), :], ssem.at[s]).wait()

    n_chunk = 4
    fc = EXPERT_FF // n_chunk
    per = tm // n_chunk

    def compute(x_buf, y_buf, r0, gather_base, gather_buf, scatter_base, scatter_buf, s):
        x = x_buf[...].astype(BF16)
        gate = gate_ref[r0:r0 + tm, :]
        def up(c):
            g = jnp.dot(x, wg_ref[0, :, c * fc:(c + 1) * fc], preferred_element_type=F32)
            u = jnp.dot(x, wu_ref[0, :, c * fc:(c + 1) * fc], preferred_element_type=F32)
            return g, u

        nxt = up(0)
        for c in range(n_chunk):
            for r in range(c * per, (c + 1) * per):
                gather_row(gather_base, r, gather_buf, s).start()
            for r in range(c * per, (c + 1) * per):
                scatter_row(scatter_base, r, scatter_buf, s).start()
            g, u = nxt
            if c + 1 < n_chunk:
                nxt = up(c + 1)
            hid = (g * _sigmoid(g) * u * gate).astype(BF16)
            part = jnp.dot(hid, wd_ref[0, c * fc:(c + 1) * fc, :], preferred_element_type=F32)
            if c == 0:
                y_buf[...] = part
            else:
                y_buf[...] += part

    @pl.when(step == 0)
    def _():
        yb[...] = jnp.zeros(yb.shape, F32)

        def body(r, carry):
            gather_row(0, r, xa, 0).start()
            return carry

        lax.fori_loop(0, tm, body, 0)

    wait_gather(xa, 0)

    @pl.when(step >= 1)
    def _():
        wait_scatter(ya, 0)

    base1 = t1 * tm
    pbase = jnp.maximum(t0 - 1, 0) * tm
    compute(xa, ya, 0, base1, xb, pbase, yb, 1)

    wait_gather(xb, 1)
    wait_scatter(yb, 1)
    nbase = jnp.minimum(t1 + 1, total - 1) * tm
    base0 = t0 * tm
    compute(xb, yb, tm, nbase, xa, base0, ya, 0)

    @pl.when(t1 == total - 1)
    def _():
        wait_gather(xa, 0)
        wait_scatter(ya, 0)

        def body(r, carry):
            scatter_row(base1, r, yb, 1).start()
            return carry

        lax.fori_loop(0, tm, body, 0)
        wait_scatter(yb, 1)


def _ffn(h2, idx, qpos, gate, wg, wu, wd):
    n, d = h2.shape
    ne = wg.shape[0]
    cap = idx.shape[0] // ne
    tm = min(512, cap // 2)
    npair = cap // (2 * tm)
    total = 2 * ne * npair
    grid_spec = pltpu.PrefetchScalarGridSpec(
        num_scalar_prefetch=2,
        grid=(ne, npair),
        in_specs=[pl.BlockSpec((2 * tm, 1), lambda e, i, *_: (e * npair + i, 0)),
                  pl.BlockSpec((1, d, EXPERT_FF), lambda e, i, *_: (e, 0, 0)),
                  pl.BlockSpec((1, d, EXPERT_FF), lambda e, i, *_: (e, 0, 0)),
                  pl.BlockSpec((1, EXPERT_FF, d), lambda e, i, *_: (e, 0, 0)),
                  pl.BlockSpec(memory_space=pl.ANY)],
        out_specs=pl.BlockSpec(memory_space=pl.ANY),
        scratch_shapes=[pltpu.VMEM((tm, d), F32), pltpu.VMEM((tm, d), F32),
                        pltpu.VMEM((tm, d), F32), pltpu.VMEM((tm, d), F32),
                        pltpu.SemaphoreType.DMA((2,)),
                        pltpu.SemaphoreType.DMA((2,))],
    )
    return pl.pallas_call(
        functools.partial(_ffn_kernel, tm=tm, npair=npair, total=total),
        grid_spec=grid_spec,
        out_shape=jax.ShapeDtypeStruct((ne * cap, d), F32),
        compiler_params=_cparams(("arbitrary", "arbitrary")),
        name="expert_ffn",
    )(idx, qpos, gate, wg, wu, wd, h2)


def _combine_kernel(cs_ref, x1_ref, mod_ref, st_ref, en_ref, g_ref, b_ref, y_hbm, o_ref,
                    ybuf, acc, sem, *, win, total, nsteps):
    i = pl.program_id(0)
    slot = i % 2
    lo0 = cs_ref[i]
    hi = cs_ref[i + 1]
    adv = win - 8

    def win_start(lo):
        return pl.multiple_of((jnp.minimum(lo, total - win) // 8) * 8, 8)

    def window(lo, sl):
        return pltpu.make_async_copy(y_hbm.at[pl.ds(win_start(lo), win), :], ybuf.at[sl], sem.at[sl])

    @pl.when(i == 0)
    def _():
        window(lo0, 0).start()

    @pl.when(i + 1 < nsteps)
    def _():
        window(hi, 1 - slot).start()

    ntrip = jnp.maximum((hi - lo0 + adv - 1) // adv, 1)
    st = st_ref[...]
    en = en_ref[...]

    def trip_sum(lo):
        window(lo, slot).wait()
        qj = win_start(lo) + lax.broadcasted_iota(I32, (1, win), 1)
        first = jnp.maximum(st, lo)
        last = jnp.minimum(en, lo + adv)
        seg = (qj >= first) & (qj < last)
        return jnp.dot(jnp.where(seg, 1.0, 0.0).astype(BF16), ybuf[slot].astype(BF16),
                       preferred_element_type=F32)

    acc[...] = trip_sum(lo0)

    def body(k, carry):
        lo = lo0 + k * adv
        window(lo, slot).start()
        acc[...] += trip_sum(lo)
        return carry

    lax.fori_loop(1, ntrip, body, 0)
    m = mod_ref[0]
    o_ref[...] = _layer_norm(ALPHA * x1_ref[...] + m[5:6] * acc[...], g_ref[...], b_ref[...])


def _combine(x1, mods, mod_row, start, pairs, g, b):
    n, d = x1.shape
    total = pairs.shape[0]
    tc = 256
    win = 640
    st = start.reshape(n, 1)
    en = jnp.concatenate([start[1:], jnp.full((1,), total, I32)]).reshape(n, 1)
    cs = jnp.concatenate([start[::tc], jnp.full((1,), total, I32)])
    grid_spec = pltpu.PrefetchScalarGridSpec(
        num_scalar_prefetch=1,
        grid=(n // tc,),
        in_specs=[pl.BlockSpec((tc, d), lambda i, *_: (i, 0)),
                  pl.BlockSpec((1, N_MOD, d), lambda i, *_: (mod_row(i * tc), 0, 0)),
                  pl.BlockSpec((tc, 1), lambda i, *_: (i, 0)),
                  pl.BlockSpec((tc, 1), lambda i, *_: (i, 0)),
                  pl.BlockSpec((1, d), lambda i, *_: (0, 0)),
                  pl.BlockSpec((1, d), lambda i, *_: (0, 0)),
                  pl.BlockSpec(memory_space=pl.ANY)],
        out_specs=pl.BlockSpec((tc, d), lambda i, *_: (i, 0)),
        scratch_shapes=[pltpu.VMEM((2, win, d), F32),
                        pltpu.VMEM((tc, d), F32),
                        pltpu.SemaphoreType.DMA((2,))],
    )
    return pl.pallas_call(
        functools.partial(_combine_kernel, win=win, total=total, nsteps=n // tc),
        grid_spec=grid_spec,
        out_shape=jax.ShapeDtypeStruct((n, d), F32),
        compiler_params=_cparams(("arbitrary",)),
        name="combine_ln2",
    )(cs, x1, mods, st, en, g, b, pairs)


def _swap_pairs(w):
    f = ROPE_FREQS
    return jnp.concatenate([w[..., f:2 * f], w[..., 0:f], w[..., 3 * f:4 * f], w[..., 2 * f:3 * f]], axis=-1)


def _rope_tables(seq):
    n_rows = seq // GRID_W
    row = np.repeat(np.arange(n_rows, dtype=np.float64), GRID_W)
    col = np.tile(np.arange(GRID_W, dtype=np.float64), n_rows)
    inv = ROPE_THETA ** (-np.arange(ROPE_FREQS, dtype=np.float64) / ROPE_FREQS)
    ar = row[:, None] * inv
    ac = col[:, None] * inv
    z = np.zeros((seq, LANES - QK_ROPE))
    cos128 = np.concatenate([np.cos(ar), np.cos(ar), np.cos(ac), np.cos(ac), z], axis=1)
    sin128 = np.concatenate([-np.sin(ar), np.sin(ar), -np.sin(ac), np.sin(ac), z], axis=1)
    return jnp.asarray(cos128, F32), jnp.asarray(sin128, F32)


def _dft_tables(seq):
    m = 2 * seq
    blk = 64
    s = np.arange(seq, dtype=np.int64)
    ang1 = ((np.arange(0, seq, blk, dtype=np.int64)[:, None] * s[None, :]) % m) * (2.0 * math.pi / m)
    ang0 = ((np.arange(blk, dtype=np.int64)[:, None] * s[None, :]) % m) * (2.0 * math.pi / m)
    c1, s1 = jnp.asarray(np.cos(ang1), F32)[:, None, :], jnp.asarray(np.sin(ang1), F32)[:, None, :]
    c0, s0 = jnp.asarray(np.cos(ang0), F32)[None, :, :], jnp.asarray(np.sin(ang0), F32)[None, :, :]
    cmat = (c1 * c0 - s1 * s0).reshape(seq, seq).astype(BF16)
    smat = (s1 * c0 + c1 * s0).reshape(seq, seq).astype(BF16)
    return cmat, smat


def _relayout_kernel(w_ref, o_ref, *, s1, s2, s3):
    chunk = 512
    width = w_ref.shape[1]
    f = ROPE_FREQS
    o_ref[:, 0:s1] = w_ref[:, 0:s1].astype(BF16)
    x = w_ref[:, s1:s1 + LANES]
    lane = lax.broadcasted_iota(I32, x.shape, 1)
    keep = lane < (s2 - s1)
    partner = jnp.where((lane & f) == 0, pltpu.roll(x, LANES - f, axis=1), pltpu.roll(x, f, axis=1))
    o_ref[:, s1:s1 + LANES] = jnp.where(keep, x, 0.0).astype(BF16)
    o_ref[:, s1 + LANES:SMALL_W] = jnp.where(keep, partner, 0.0).astype(BF16)
    dst = SMALL_W
    for lo, hi in ((s3, width), (s2, s3)):
        for c in range(lo, hi, chunk):
            n = min(chunk, hi - c)
            o_ref[:, dst:dst + n] = w_ref[:, c:c + n].astype(BF16)
            dst += n


def _relayout_w_in(w_in, s1, s2, s3):
    d, width = w_in.shape
    tr = 256
    return pl.pallas_call(
        functools.partial(_relayout_kernel, s1=s1, s2=s2, s3=s3),
        grid=(d // tr,),
        in_specs=[pl.BlockSpec((tr, width), lambda i: (i, 0))],
        out_specs=pl.BlockSpec((tr, SMALL_W + BIG_W), lambda i: (i, 0)),
        out_shape=jax.ShapeDtypeStruct((d, SMALL_W + BIG_W), BF16),
        compiler_params=_cparams(("arbitrary",)),
        name="w_in_relayout",
    )(w_in)


def _prep_weights(p):
    w = {}
    w_in = p['w_in']
    s0, s1, s2, s3, s4 = (Q_LORA, Q_LORA + KV_LORA, Q_LORA + KV_LORA + QK_ROPE,
                          Q_LORA + KV_LORA + QK_ROPE + 3 * HY_WIDTH,
                          Q_LORA + KV_LORA + QK_ROPE + 3 * HY_WIDTH + D_MODEL)
    w['w_in'] = _relayout_w_in(w_in, s1, s2, s3)
    scale = (QK_NOPE + QK_ROPE) ** -0.5 * math.log2(math.e)
    wq = (p['w_uq'] * scale).reshape(Q_LORA, MLA_HEADS, QK_NOPE + QK_ROPE)
    zq = jnp.zeros((Q_LORA, MLA_HEADS, HEAD_PAD - QK_NOPE - QK_ROPE), F32)
    w['wa'] = jnp.concatenate([wq, zq], axis=-1).reshape(Q_LORA, MLA_HEADS * HEAD_PAD).astype(BF16)
    w['wb'] = jnp.concatenate([_swap_pairs(wq[..., QK_NOPE:]), zq], axis=-1).reshape(
        Q_LORA, MLA_HEADS * LANES).astype(BF16)
    wkv = p['w_ukv'].reshape(KV_LORA, MLA_HEADS, QK_NOPE + V_HEAD)
    w['wk'] = wkv[..., :QK_NOPE].reshape(KV_LORA, MLA_HEADS * QK_NOPE).astype(BF16)
    w['wv'] = wkv[..., QK_NOPE:].reshape(KV_LORA, MLA_HEADS * V_HEAD).astype(BF16)
    w['woa'] = p['w_o_mla'].astype(BF16)
    w['woh'] = p['w_o_hy'].astype(BF16)
    w['wout'] = p['w_out'].astype(BF16)
    wr_hi = p['w_router'].astype(BF16)
    wr_lo = (p['w_router'] - wr_hi.astype(F32)).astype(BF16)
    w['wr_pack'] = jnp.pad(jnp.concatenate([wr_hi, wr_lo], axis=1), ((0, 0), (0, LANES - 2 * N_EXPERTS)))
    w['hy_w'] = p['hy_short_w'].reshape(3, 3, HY_WIDTH).transpose(1, 0, 2)
    w['hy_b'] = p['hy_short_b'].reshape(3, 1, HY_WIDTH)
    return w


def _trunk(x3, mods, mod_row, tm_in, p, w, cache_kv, cache_kr, rope, ffn_w):
    b, seq, d = x3.shape
    n = b * seq
    x = x3.reshape(n, d)
    if ffn_w is None:
        shapes = [p[k].shape for k in ('w_gate', 'w_up', 'w_down')]
        flat = [p[k].reshape(-1, p[k].shape[-1]) for k in ('w_gate', 'w_up', 'w_down')]
        small, big, cast = _inproj(x, mods, mod_row, w['w_in'], tm_in, flat)
        ffn_w = tuple(a.reshape(s) for a, s in zip(cast, shapes))
    else:
        small, big, _ = _inproj(x, mods, mod_row, w['w_in'], tm_in)
    cos128, sin128 = _rope_tables(seq) if rope else (None, None)
    q, ckv, kr = _qprep(small, p['q_norm_g'].reshape(1, -1), p['kv_norm_g'].reshape(1, -1),
                        w['wa'], w['wb'], cos128, sin128, seq, rope)
    ckv3 = ckv.reshape(b, seq, KV_LORA)
    kr3 = kr.reshape(b, seq, LANES)
    if cache_kv is None:
        kv_all, kr_all = ckv3, kr3
    else:
        kv_all = jnp.concatenate([cache_kv, ckv3], axis=1)
        kr_all = jnp.concatenate([jnp.pad(cache_kr, ((0, 0), (0, 0), (0, LANES - QK_ROPE))), kr3], axis=1)
    kh, vh = _kvup(kv_all, kr_all, w['wk'], w['wv'])
    attn = _attention(q.reshape(b, seq, MLA_HEADS * HEAD_PAD), kh, vh).reshape(n, MLA_HEADS * V_HEAD)

    nb = HY_TIME_BLOCKS if seq >= 512 else 1
    x0c, z, zn = _hypre(big.reshape(b, seq, BIG_W), w['hy_w'], w['hy_b'], nb)
    hf_td, hb_td, ext = _filter_td(seq, p, nb)
    cmat, smat = _dft_tables(seq // nb)
    kc, ks, kn = _kspec(cmat, smat, hf_td, hb_td, ext, nb)
    pc, ps = _hy_fwd(cmat, smat, z, kc, ks, nb)
    hy = _hy_inv(cmat, smat, pc, ps, z, x0c, zn, kn, p['hy_skip'].reshape(1, -1), nb).reshape(n, HY_WIDTH)

    merged = _merge(attn, hy, big, w['woa'], w['woh'])
    x1, h2, lg = _ln1(x, merged, mods, mod_row, w['wout'], p['ln1_g'].reshape(1, -1),
                      p['ln1_b'].reshape(1, -1), w['wr_pack'])
    cap = EC_CAPACITY * n // N_EXPERTS
    lg3 = lg[:, :N_EXPERTS].T.reshape(N_EXPERTS, n // LANES, LANES)
    idx, qpos, gate, start = _route(lg3, cap)
    pairs = _ffn(h2, idx.reshape(-1), qpos.reshape(-1), gate.reshape(-1, 1), *ffn_w)
    y = _combine(x1, mods, mod_row, start.reshape(-1), pairs,
                 p['ln2_g'].reshape(1, -1), p['ln2_b'].reshape(1, -1))
    return y.reshape(b, seq, d), ckv3, kr3[..., :QK_ROPE], ffn_w


def kernel(x_prompt, x_sample, cache_kv_c, cache_k_rope, c, c_ctx, w_ada, b_ada, w_in, q_norm_g, kv_norm_g, w_uq, w_ukv, w_o_mla, hy_short_w, hy_short_b, hy_filt_w1, hy_filt_b1, hy_filt_freq1, hy_filt_w2, hy_filt_b2, hy_filt_freq2, hy_filt_w3, hy_filt_b3, hy_skip, w_o_hy, w_out, ln1_g, ln1_b, ln2_g, ln2_b, w_router, w_gate, w_up, w_down):
    params = dict(w_in=w_in, q_norm_g=q_norm_g, kv_norm_g=kv_norm_g, w_uq=w_uq, w_ukv=w_ukv, w_o_mla=w_o_mla,
                  hy_short_w=hy_short_w, hy_short_b=hy_short_b, hy_filt_w1=hy_filt_w1, hy_filt_b1=hy_filt_b1,
                  hy_filt_freq1=hy_filt_freq1, hy_filt_w2=hy_filt_w2, hy_filt_b2=hy_filt_b2,
                  hy_filt_freq2=hy_filt_freq2, hy_filt_w3=hy_filt_w3, hy_filt_b3=hy_filt_b3, hy_skip=hy_skip,
                  w_o_hy=w_o_hy, w_out=w_out, ln1_g=ln1_g, ln1_b=ln1_b, ln2_g=ln2_g, ln2_b=ln2_b,
                  w_router=w_router, w_gate=w_gate, w_up=w_up, w_down=w_down)
    depth = w_in.shape[0]
    dec_b, dec_seq, d = x_sample.shape
    n_rows = 16
    cond = jnp.zeros((n_rows, d), F32).at[:dec_b].set(c).at[dec_b].set(c_ctx)
    y_prompt, y_sample = x_prompt, x_sample
    kv_list, kr_list = [], []
    for l in range(depth):
        p = {k: v[l] for k, v in params.items()}
        w = _prep_weights(p)
        mods = _ada_mod(cond, w_ada[l], b_ada[l]).reshape(n_rows, N_MOD, d)
        n_prompt = x_prompt.shape[0] * x_prompt.shape[1]
        y_sample, _, _, ffn_w = _trunk(y_sample, mods, lambda r: r // dec_seq, min(1024, dec_seq), p, w,
                                       cache_kv_c[:, l], cache_k_rope[:, l], True, None)
        y_prompt, c_kv, k_rope, _ = _trunk(y_prompt, mods, lambda r: dec_b, min(1024, n_prompt), p, w,
                                           None, None, False, ffn_w)
        kv_list.append(c_kv)
        kr_list.append(k_rope)
    return (y_prompt, y_sample, jnp.stack(kv_list, axis=1), jnp.stack(kr_list, axis=1))
```
